```python
import jax, jax.numpy as jnp
from jax import lax
import numpy as np

D_MODEL = 1024
BATCH = 16
SEQ = 4096
DEPTH = 1

GLA_HEADS = 4
GLA_DK = 64
GLA_DV = 128
GLA_KEY_W = GLA_HEADS * GLA_DK
GLA_VAL_W = GLA_HEADS * GLA_DV
GLA_GATE_RANK = 16
GLA_GATE_NORMALIZER = 16.0
GLA_CHUNK = 64
POOL_GROUPS = 4
POOL_WINDOWS = (2, 4, 8, 16)
POOL_W = D_MODEL // 2
POOL_GW = POOL_W // POOL_GROUPS
IN_WIDTHS = (GLA_KEY_W, GLA_KEY_W, GLA_VAL_W, GLA_VAL_W, GLA_GATE_RANK, POOL_W, D_MODEL, D_MODEL)
D_IN = sum(IN_WIDTHS)
N_GROUPS = 4
EXPERTS_PER_GROUP = 8
N_EXPERTS = N_GROUPS * EXPERTS_PER_GROUP
TOP_K_IN_GROUP = 2
D_EXPERT = 256
MOE_BLOCK = 128
NORM_EPS = 1e-6

kernel_name = "hybrid_gla_pool_hmoe_block"


def rmsnorm(x, g):
    xf = x.astype(jnp.float32)
    y = xf * lax.rsqrt(jnp.mean(xf * xf, axis=-1, keepdims=True) + NORM_EPS)
    return (y * g.astype(jnp.float32)).astype(x.dtype)


def gla_mix(q, k, v, log_a):
    B, S, _ = q.shape
    C = GLA_CHUNK
    N = S // C

    def split(t, d):
        return t.astype(jnp.float32).reshape(B, N, C, GLA_HEADS, d).transpose(0, 3, 1, 2, 4)

    qc, kc, gc = split(q, GLA_DK), split(k, GLA_DK), split(log_a, GLA_DK)
    vc = split(v, GLA_DV)
    b = jnp.cumsum(gc, axis=3)
    b_last = b[:, :, :, -1:, :]
    q_dec = qc * jnp.exp(b) * (GLA_DK ** -0.5)
    k_dec = kc * jnp.exp(-b)
    causal = jnp.tril(jnp.ones((C, C), dtype=bool))
    scores = jnp.where(causal, jnp.einsum('bhntd,bhnsd->bhnts', q_dec, k_dec), 0.0)
    o_intra = jnp.einsum('bhnts,bhnsv->bhntv', scores, vc)
    k_end = kc * jnp.exp(b_last - b)
    inc = jnp.einsum('bhnsd,bhnsv->bhndv', k_end, vc)
    decay = jnp.exp(b_last[:, :, :, 0, :])

    def step(state, inp):
        dec, u = inp
        return dec[..., None] * state + u, state

    state0 = jnp.zeros((B, GLA_HEADS, GLA_DK, GLA_DV), jnp.float32)
    _, s_prev = lax.scan(step, state0, (jnp.moveaxis(decay, 2, 0), jnp.moveaxis(inc, 2, 0)))
    s_prev = jnp.moveaxis(s_prev, 0, 2)
    o = o_intra + jnp.einsum('bhntd,bhndv->bhntv', q_dec, s_prev)
    return o.transpose(0, 2, 3, 1, 4).reshape(B, S, GLA_VAL_W)


def head_rmsnorm(o, g):
    B, S, _ = o.shape
    oh = o.reshape(B, S, GLA_HEADS, GLA_DV)
    oh = oh * lax.rsqrt(jnp.mean(oh * oh, axis=-1, keepdims=True) + NORM_EPS)
    return oh.reshape(B, S, GLA_VAL_W) * g.astype(jnp.float32)


def pool_mix(u, pool_w, pool_scale):
    B, S, _ = u.shape
    uf = u.astype(jnp.float32)
    ug = uf.reshape(B, S, POOL_GROUPS, POOL_GW)
    cs = jnp.cumsum(uf, axis=1).reshape(B, S, POOL_GROUPS, POOL_GW)
    pos = jnp.arange(S, dtype=jnp.float32)
    pooled = []
    for gi, w in enumerate(POOL_WINDOWS):
        c = cs[:, :, gi]
        shifted = jnp.pad(c[:, :S - w], ((0, 0), (w, 0), (0, 0)))
        cnt = jnp.minimum(pos + 1.0, float(w))[None, :, None]
        pooled.append((c - shifted) / cnt)
    pooled = jnp.stack(pooled, axis=2)
    mixed = jnp.einsum('bsgc,gcd->bsgd', pooled - ug, pool_w.astype(jnp.float32))
    return (mixed.reshape(B, S, POOL_W) * pool_scale.astype(jnp.float32)).astype(u.dtype)


def hier_moe(h, w_rg, b_rg, w_re, b_re, w_gate, w_up, w_down):
    B, S, D = h.shape
    T = B * S
    ht = h.reshape(T, D)
    g_logits = (ht @ w_rg).astype(jnp.float32) + b_rg.astype(jnp.float32)
    g_p, g_idx = lax.top_k(jax.nn.softmax(g_logits, axis=-1), 1)
    e_all = jnp.einsum('td,gde->tge', ht, w_re).astype(jnp.float32) + b_re.astype(jnp.float32)
    e_logits = e_all[jnp.arange(T), g_idx[:, 0]]
    e_top, e_idx = lax.top_k(e_logits, TOP_K_IN_GROUP)
    gate = g_p * jax.nn.softmax(e_top, axis=-1)
    A = T * TOP_K_IN_GROUP
    weights = gate.reshape(A)
    expert_ids = (g_idx * EXPERTS_PER_GROUP + e_idx).reshape(A).astype(jnp.int32)
    token_ids = jnp.repeat(jnp.arange(T, dtype=jnp.int32), TOP_K_IN_GROUP)

    n_blocks = -(-(A + N_EXPERTS * (MOE_BLOCK - 1)) // MOE_BLOCK)
    R = n_blocks * MOE_BLOCK
    order = jnp.argsort(expert_ids)
    sorted_e = expert_ids[order]
    counts = jnp.zeros((N_EXPERTS,), jnp.int32).at[expert_ids].add(1)
    padded = ((counts + MOE_BLOCK - 1) // MOE_BLOCK) * MOE_BLOCK
    starts = jnp.cumsum(counts) - counts
    pends = jnp.cumsum(padded)
    pstarts = pends - padded
    dest = pstarts[sorted_e] + (jnp.arange(A, dtype=jnp.int32) - starts[sorted_e])
    row_token = jnp.full((R,), T, jnp.int32).at[dest].set(token_ids[order])
    row_w = jnp.zeros((R,), jnp.float32).at[dest].set(weights[order])
    block_e = jnp.minimum(jnp.searchsorted(pends, jnp.arange(n_blocks, dtype=jnp.int32) * MOE_BLOCK,
                                           side='right'), N_EXPERTS - 1).astype(jnp.int32)
    h_pad = jnp.concatenate([ht, jnp.zeros((1, D), ht.dtype)], axis=0)
    x_blocks = h_pad[row_token].reshape(n_blocks, MOE_BLOCK, D)

    def expert_block(args):
        xb, e = args
        return (jax.nn.silu(xb @ w_gate[e]) * (xb @ w_up[e])) @ w_down[e]

    y_rows = lax.map(expert_block, (x_blocks, block_e)).reshape(R, D)
    y_rows = y_rows * row_w[:, None].astype(y_rows.dtype)
    out = jnp.zeros((T + 1, D), y_rows.dtype).at[row_token].add(y_rows)[:T]
    return out.reshape(B, S, D).astype(h.dtype)


def setup_inputs(seed: int = 0) -> dict:
    key = jax.random.key(seed)
    ks = jax.random.split(key, 24)
    L, D = DEPTH, D_MODEL
    nrm = lambda k, shape, s: jax.random.normal(k, shape, jnp.float32) * s
    return {
        "x": nrm(ks[0], (BATCH, SEQ, D), 1.0),
        "norm1_g": 1.0 + nrm(ks[1], (L, D), 0.05),
        "w_in": nrm(ks[2], (L, D, D_IN), D ** -0.5),
        "w_alpha_up": nrm(ks[3], (L, GLA_GATE_RANK, GLA_KEY_W), GLA_GATE_RANK ** -0.5),
        "b_alpha": nrm(ks[4], (L, GLA_KEY_W), 0.1),
        "gla_norm_g": 1.0 + nrm(ks[5], (L, GLA_VAL_W), 0.05),
        "w_gla_branch": nrm(ks[6], (L, GLA_VAL_W, D), GLA_VAL_W ** -0.5),
        "pool_w": nrm(ks[7], (L, POOL_GROUPS, POOL_GW, POOL_GW), POOL_GW ** -0.5),
        "pool_scale": 1.0 + nrm(ks[8], (L, POOL_W), 0.05),
        "w_pool_branch": nrm(ks[9], (L, POOL_W, D), POOL_W ** -0.5),
        "w_out": nrm(ks[10], (L, D, D), D ** -0.5),
        "norm2_g": 1.0 + nrm(ks[11], (L, D), 0.05),
        "w_router_group": nrm(ks[12], (L, D, N_GROUPS), D ** -0.5),
        "b_router_group": nrm(ks[13], (L, N_GROUPS), 0.01),
        "w_router_expert": nrm(ks[14], (L, N_GROUPS, D, EXPERTS_PER_GROUP), D ** -0.5),
        "b_router_expert": nrm(ks[15], (L, N_GROUPS, EXPERTS_PER_GROUP), 0.01),
        "w_exp_gate": nrm(ks[16], (L, N_EXPERTS, D, D_EXPERT), D ** -0.5),
        "w_exp_up": nrm(ks[17], (L, N_EXPERTS, D, D_EXPERT), D ** -0.5),
        "w_exp_down": nrm(ks[18], (L, N_EXPERTS, D_EXPERT, D), D_EXPERT ** -0.5),
        "norm_f_g": 1.0 + nrm(ks[19], (D,), 0.05),
    }


def reference(x, norm1_g, w_in, w_alpha_up, b_alpha, gla_norm_g, w_gla_branch, pool_w, pool_scale,
              w_pool_branch, w_out, norm2_g, w_router_group, b_router_group, w_router_expert,
              b_router_expert, w_exp_gate, w_exp_up, w_exp_down, norm_f_g):
    split_idx = [int(i) for i in np.cumsum(IN_WIDTHS)[:-1]]
    for l in range(DEPTH):
        h = rmsnorm(x, norm1_g[l])
        proj = h @ w_in[l]
        q, k, v, r, a_low, u_pool, gate_gla, gate_pool = jnp.split(proj, split_idx, axis=-1)
        log_a = jax.nn.log_sigmoid((a_low @ w_alpha_up[l] + b_alpha[l]).astype(jnp.float32)) / GLA_GATE_NORMALIZER
        o = gla_mix(q, k, v, log_a)
        o = (head_rmsnorm(o, gla_norm_g[l]) * jax.nn.silu(r.astype(jnp.float32))).astype(x.dtype)
        y_gla = o @ w_gla_branch[l]
        y_pool = pool_mix(u_pool, pool_w[l], pool_scale[l]) @ w_pool_branch[l]
        merged = jax.nn.sigmoid(gate_gla) * y_gla + jax.nn.sigmoid(gate_pool) * y_pool
        x = x + merged @ w_out[l]
        h2 = rmsnorm(x, norm2_g[l])
        x = x + hier_moe(h2, w_router_group[l], b_router_group[l], w_router_expert[l], b_router_expert[l],
                         w_exp_gate[l], w_exp_up[l], w_exp_down[l])
    return rmsnorm(x, norm_f_g)
```

```python
import functools

import jax
import jax.numpy as jnp
from jax import lax
from jax.experimental import pallas as pl
from jax.experimental.pallas import tpu as pltpu

F32 = jnp.float32
BF16 = jnp.bfloat16

D_MODEL = 1024
GLA_HEADS = 4
GLA_DK = 64
GLA_DV = 128
KEY_W = GLA_HEADS * GLA_DK
VAL_W = GLA_HEADS * GLA_DV
GATE_RANK = 16
GATE_NORMALIZER = 16.0
GLA_CHUNK = 64
POOL_WINDOWS = (2, 4, 8, 16)
POOL_W = 512
POOL_GW = 128
N_GROUPS = 4
EXPERTS_PER_GROUP = 8
N_EXPERTS = 32
D_EXPERT = 256
NORM_EPS = 1e-6

LANE = 128
SUBLANE = 8
POOL_HALO = 16
ROUTER_ROWS = 48

MIX_TS = 512
ROW_TILE = 512
EXP_BM = 256
VMEM_LIMIT = 56 * 1024 * 1024


def _dot(a, b):
    return jnp.dot(a, b, preferred_element_type=F32)


def _dot_nt(a, b):
    return lax.dot_general(a, b, (((1,), (1,)), ((), ())), preferred_element_type=F32)


def _dot_tn(a, b):
    return lax.dot_general(a, b, (((0,), (0,)), ((), ())), preferred_element_type=F32)


def _chunk_cumsum(x, chunk):
    n, w = x.shape
    pos = lax.broadcasted_iota(jnp.int32, (n, w), 0) % chunk
    step = 1
    while step < chunk:
        if step < 8:
            shifted = pltpu.roll(x, step, axis=0)
        else:
            shifted = jnp.concatenate([jnp.zeros((step, w), x.dtype), x[:n - step]], axis=0)
        x = x + jnp.where(pos >= step, shifted, 0.0)
        step *= 2
    return x


def _mixer_kernel(x_ref, g1_ref, wqkvr_ref, wa_ref, wu_ref, wg_ref, walpha_ref, balpha_ref,
                  glag_ref, wglab_ref, poolw_ref, pscale_ref, wpoolb_ref, wout_ref, g2_ref,
                  wrt_ref, brt_ref,
                  x1_ref, h2_ref, eid_ref, gate_ref, rank_ref, cnt_ref,
                  state_s, carry_s, cnt_s, qd_s, kd_s, ke_s, v_s, dec_s, o_s):
    ts = x_ref.shape[0]
    b_idx = pl.program_id(0)
    s_idx = pl.program_id(1)

    @pl.when(s_idx == 0)
    def _():
        state_s[...] = jnp.zeros_like(state_s)
        carry_s[...] = jnp.zeros_like(carry_s)

    @pl.when((b_idx == 0) & (s_idx == 0))
    def _():
        cnt_s[...] = jnp.zeros_like(cnt_s)

    xf = x_ref[...]
    ms = jnp.mean(xf * xf, axis=-1, keepdims=True)
    h = (xf * lax.rsqrt(ms + NORM_EPS) * g1_ref[...]).astype(BF16)

    qkvr = _dot(h, wqkvr_ref[...])
    q = qkvr[:, 0:KEY_W]
    k = qkvr[:, KEY_W:2 * KEY_W]
    v = qkvr[:, 2 * KEY_W:2 * KEY_W + VAL_W]
    r = qkvr[:, 2 * KEY_W + VAL_W:]

    a_low = _dot(h, wa_ref[...])
    z = _dot(a_low.astype(BF16), walpha_ref[...]) + balpha_ref[...]
    log_a = (jnp.minimum(z, 0.0) - jnp.log1p(jnp.exp(-jnp.abs(z)))) * (1.0 / GATE_NORMALIZER)

    nc = ts // GLA_CHUNK
    b = _chunk_cumsum(log_a, GLA_CHUNK)
    b3 = b.reshape(nc, GLA_CHUNK, KEY_W)
    b_last = b3[:, GLA_CHUNK - 1:GLA_CHUNK, :]
    qd_s[...] = (q * jnp.exp(b) * (GLA_DK ** -0.5)).astype(BF16)
    kd_s[...] = (k * jnp.exp(-b)).astype(BF16)
    ke_s[...] = (k.reshape(nc, GLA_CHUNK, KEY_W) * jnp.exp(b_last - b3)).reshape(ts, KEY_W).astype(BF16)
    dec_s[...] = jnp.broadcast_to(jnp.exp(b_last), (nc, GLA_CHUNK, KEY_W)).reshape(ts, KEY_W)
    v_s[...] = v.astype(BF16)

    lane = lax.broadcasted_iota(jnp.int32, (GLA_CHUNK, LANE), 1)
    lo_half = lane < GLA_DK
    lane_s = lax.broadcasted_iota(jnp.int32, (GLA_DV, LANE), 1)
    lo_half_s = lane_s < GLA_DK
    tri = (lax.broadcasted_iota(jnp.int32, (GLA_CHUNK, GLA_CHUNK), 0)
           >= lax.broadcasted_iota(jnp.int32, (GLA_CHUNK, GLA_CHUNK), 1))

    def chunk_body(c, carry):
        r0 = pl.multiple_of(c * GLA_CHUNK, GLA_CHUNK)
        rows = pl.ds(r0, GLA_CHUNK)
        for hd in range(GLA_HEADS):
            pair = slice((hd // 2) * LANE, (hd // 2 + 1) * LANE)
            mine = lo_half if hd % 2 == 0 else jnp.logical_not(lo_half)
            mine_s = lo_half_s if hd % 2 == 0 else jnp.logical_not(lo_half_s)
            qd = jnp.where(mine, qd_s[rows, pair], jnp.zeros((), BF16))
            kd = kd_s[rows, pair]
            ke = ke_s[rows, pair]
            vh = v_s[rows, hd * GLA_DV:(hd + 1) * GLA_DV]
            dec = dec_s[pl.ds(r0, 8), pair][0:1, :]
            st = state_s[hd]
            scores = jnp.where(tri, _dot_nt(qd, kd), 0.0)
            o = _dot(scores.astype(BF16), vh) + _dot_nt(qd, st.astype(BF16))
            o_s[rows, hd * GLA_DV:(hd + 1) * GLA_DV] = o
            inc_t = _dot_tn(vh, ke)
            state_s[hd] = jnp.where(mine_s, dec * st + inc_t, 0.0)
        return carry

    lax.fori_loop(0, nc, chunk_body, 0)

    o_all = o_s[...]
    parts = []
    for hd in range(GLA_HEADS):
        oh = o_all[:, hd * GLA_DV:(hd + 1) * GLA_DV]
        parts.append(oh * lax.rsqrt(jnp.mean(oh * oh, axis=-1, keepdims=True) + NORM_EPS))
    o_n = jnp.concatenate(parts, axis=-1) * glag_ref[...]
    o_g = (o_n * (r * jax.nn.sigmoid(r))).astype(BF16)
    y_gla = _dot(o_g, wglab_ref[...])

    u = _dot(h, wu_ref[...])
    ext = jnp.concatenate([carry_s[...], u], axis=0)
    carry_s[...] = u[ts - POOL_HALO:, :]
    pos = (s_idx * ts + lax.broadcasted_iota(jnp.int32, (ts, 1), 0)).astype(F32)
    mixed = []
    for gi, w in enumerate(POOL_WINDOWS):
        a = ext[:, gi * POOL_GW:(gi + 1) * POOL_GW]
        step = 1
        while step < w:
            a = a + pltpu.roll(a, step, axis=0)
            step *= 2
        pooled = a[POOL_HALO:, :] / jnp.minimum(pos + 1.0, float(w))
        diff = pooled - u[:, gi * POOL_GW:(gi + 1) * POOL_GW]
        mixed.append(_dot(diff.astype(BF16), poolw_ref[gi]))
    pm = (jnp.concatenate(mixed, axis=-1) * pscale_ref[...]).astype(BF16)
    y_pool = _dot(pm, wpoolb_ref[...])

    gates = _dot(h, wg_ref[...])
    merged = (jax.nn.sigmoid(gates[:, :D_MODEL]) * y_gla
              + jax.nn.sigmoid(gates[:, D_MODEL:]) * y_pool).astype(BF16)
    x1 = xf + _dot(merged, wout_ref[...])
    x1_ref[...] = x1

    ms2 = jnp.mean(x1 * x1, axis=-1, keepdims=True)
    h2 = x1 * lax.rsqrt(ms2 + NORM_EPS) * g2_ref[...]
    h2_ref[...] = h2
    lt = _dot_nt(wrt_ref[...], h2.astype(BF16)) + brt_ref[:, 0:1]

    l0, l1, l2, l3 = lt[0:1], lt[1:2], lt[2:3], lt[3:4]
    gm = jnp.maximum(jnp.maximum(l0, l1), jnp.maximum(l2, l3))
    gidx = jnp.where(l0 == gm, 0, jnp.where(l1 == gm, 1, jnp.where(l2 == gm, 2, 3)))
    gp = 1.0 / (jnp.exp(l0 - gm) + jnp.exp(l1 - gm) + jnp.exp(l2 - gm) + jnp.exp(l3 - gm))
    el = jnp.where(gidx == 0, lt[8:16],
                   jnp.where(gidx == 1, lt[16:24], jnp.where(gidx == 2, lt[24:32], lt[32:40])))
    row8 = lax.broadcasted_iota(jnp.int32, (EXPERTS_PER_GROUP, ts), 0)
    m1 = jnp.max(el, axis=0, keepdims=True)
    i1 = jnp.min(jnp.where(el == m1, row8, EXPERTS_PER_GROUP), axis=0, keepdims=True)
    el2 = jnp.where(row8 == i1, -jnp.inf, el)
    m2 = jnp.max(el2, axis=0, keepdims=True)
    i2 = jnp.min(jnp.where(el2 == m2, row8, EXPERTS_PER_GROUP), axis=0, keepdims=True)
    dd = jnp.exp(m2 - m1)
    p1 = 1.0 / (1.0 + dd)
    p2 = dd / (1.0 + dd)
    e1 = gidx * EXPERTS_PER_GROUP + i1
    e2 = gidx * EXPERTS_PER_GROUP + i2
    eid_ref[0:1, :] = e1
    eid_ref[1:2, :] = e2
    gate_ref[...] = jnp.concatenate([gp * p1, gp * p2, jnp.zeros((6, ts), F32)], axis=0)

    row32 = lax.broadcasted_iota(jnp.int32, (N_EXPERTS, ts), 0)
    oh1 = row32 == e1
    oh2 = row32 == e2
    member = jnp.where(oh1, 1.0, 0.0) + jnp.where(oh2, 1.0, 0.0)
    before = (lax.broadcasted_iota(jnp.int32, (ts, ts), 0)
              < lax.broadcasted_iota(jnp.int32, (ts, ts), 1))
    prefix = _dot(member.astype(BF16), jnp.where(before, 1.0, 0.0).astype(BF16))
    base = cnt_s[...]
    prefix = prefix + jnp.concatenate([base] * (ts // LANE), axis=1)
    rank_ref[0:1, :] = jnp.sum(jnp.where(oh1, prefix, 0.0), axis=0, keepdims=True).astype(jnp.int32)
    rank_ref[1:2, :] = jnp.sum(jnp.where(oh2, prefix, 0.0), axis=0, keepdims=True).astype(jnp.int32)
    new_cnt = base + jnp.sum(member, axis=1, keepdims=True)
    cnt_s[...] = new_cnt
    cnt_ref[...] = new_cnt


def _const_spec(shape):
    nd = len(shape)
    return pl.BlockSpec(shape, lambda b, s, _nd=nd: (0,) * _nd, pipeline_mode=pl.Buffered(1))


def _mixer_call(x, wts, ts):
    B, S, D = x.shape
    T = B * S
    ns = S // ts
    tok_spec = pl.BlockSpec((None, ts, D), lambda b, s: (b, s, 0))
    flat_tok = lambda rows: pl.BlockSpec((rows, ts), lambda b, s: (0, b * ns + s))
    in_specs = [tok_spec] + [_const_spec(w.shape) for w in wts]
    out_shape = (
        jax.ShapeDtypeStruct((B, S, D), F32),
        jax.ShapeDtypeStruct((B, S, D), F32),
        jax.ShapeDtypeStruct((2, T), jnp.int32),
        jax.ShapeDtypeStruct((8, T), F32),
        jax.ShapeDtypeStruct((2, T), jnp.int32),
        jax.ShapeDtypeStruct((N_EXPERTS, LANE), F32),
    )
    out_specs = (tok_spec, tok_spec, flat_tok(2), flat_tok(8), flat_tok(2),
                 pl.BlockSpec((N_EXPERTS, LANE), lambda b, s: (0, 0)))
    scratch = [
        pltpu.VMEM((GLA_HEADS, GLA_DV, LANE), F32),
        pltpu.VMEM((POOL_HALO, POOL_W), F32),
        pltpu.VMEM((N_EXPERTS, LANE), F32),
        pltpu.VMEM((ts, KEY_W), BF16),
        pltpu.VMEM((ts, KEY_W), BF16),
        pltpu.VMEM((ts, KEY_W), BF16),
        pltpu.VMEM((ts, VAL_W), BF16),
        pltpu.VMEM((ts, KEY_W), F32),
        pltpu.VMEM((ts, VAL_W), F32),
    ]
    return pl.pallas_call(
        _mixer_kernel,
        grid=(B, ns),
        in_specs=in_specs,
        out_specs=out_specs,
        out_shape=out_shape,
        scratch_shapes=scratch,
        compiler_params=pltpu.CompilerParams(
            dimension_semantics=("arbitrary", "arbitrary"), vmem_limit_bytes=VMEM_LIMIT),
        name="mixer",
    )(x, *wts)


_PAD_BITS = tuple(p for p in (1 << i for i in reversed(range((EXP_BM - 1).bit_length()))) if p >= SUBLANE)


def _dispatch_kernel(pst_ref, cnt_ref, dest_ref, h2_ref, xs_ref, zero_s, sem, zsem):
    td = h2_ref.shape[0]
    i = pl.program_id(0)

    def zero_copy(off, p):
        return pltpu.make_async_copy(zero_s.at[pl.ds(0, p)], xs_ref.at[pl.ds(off, p)], zsem)

    def for_each_pad_piece(e, fn):
        cnt = cnt_ref[e]
        start = pst_ref[e] + cnt
        end = start + (EXP_BM - cnt % EXP_BM) % EXP_BM
        aligned = jnp.minimum((start + SUBLANE - 1) // SUBLANE * SUBLANE, end)
        for j in range(SUBLANE - 1):
            @pl.when(start + j < aligned)
            def _(j=j):
                fn(zero_copy(start + j, 1))

        npad = end - aligned
        off = aligned
        for p in _PAD_BITS:
            hit = (npad & p) != 0

            @pl.when(hit)
            def _(off=off, p=p):
                fn(zero_copy(pl.multiple_of(off, SUBLANE), p))

            off = off + jnp.where(hit, p, 0)

    @pl.when(i == 0)
    def _():
        zero_s[...] = jnp.zeros_like(zero_s)

        def start_e(e, c):
            for_each_pad_piece(e, lambda cp: cp.start())
            return c

        def wait_e(e, c):
            for_each_pad_piece(e, lambda cp: cp.wait())
            return c

        lax.fori_loop(0, N_EXPERTS, start_e, 0)
        lax.fori_loop(0, N_EXPERTS, wait_e, 0)

        zrows = zero_s.shape[0]
        last_cnt = cnt_ref[N_EXPERTS - 1]
        used = pst_ref[N_EXPERTS - 1] + (last_cnt + EXP_BM - 1) // EXP_BM * EXP_BM
        first_piece = used // zrows
        n_pieces = xs_ref.shape[0] // zrows

        def tail_copy(t):
            return zero_copy(pl.multiple_of(t * zrows, zrows), zrows)

        lax.fori_loop(first_piece, n_pieces, lambda t, c: (tail_copy(t).start(), c)[1], 0)
        lax.fori_loop(first_piece, n_pieces, lambda t, c: (tail_copy(t).wait(), c)[1], 0)

    def issue(j, c):
        for kk in range(2):
            d = dest_ref[kk, j]
            pltpu.make_async_copy(h2_ref.at[pl.ds(j, 1)], xs_ref.at[pl.ds(d, 1)], sem).start()
        return c

    lax.fori_loop(0, td, issue, 0)
    for kk in range(2):
        pltpu.make_async_copy(h2_ref, xs_ref.at[pl.ds(0, td)], sem).wait()


def _dispatch_call(h2, dest3, pstarts, counts, n_rows):
    T, D = h2.shape
    nt, _, td = dest3.shape
    grid_spec = pltpu.PrefetchScalarGridSpec(
        num_scalar_prefetch=2,
        grid=(nt,),
        in_specs=[
            pl.BlockSpec((None, 2, td), lambda i, *_: (i, 0, 0), memory_space=pltpu.SMEM),
            pl.BlockSpec((td, D), lambda i, *_: (i, 0)),
        ],
        out_specs=pl.BlockSpec(memory_space=pl.ANY),
        scratch_shapes=[
            pltpu.VMEM((_PAD_BITS[0], D), F32),
            pltpu.SemaphoreType.DMA(()),
            pltpu.SemaphoreType.DMA(()),
        ],
    )
    return pl.pallas_call(
        _dispatch_kernel,
        grid_spec=grid_spec,
        out_shape=jax.ShapeDtypeStruct((n_rows, D), F32),
        compiler_params=pltpu.CompilerParams(
            dimension_semantics=("arbitrary",), vmem_limit_bytes=VMEM_LIMIT, has_side_effects=True),
        name="dispatch",
    )(pstarts, counts, dest3, h2)


def _expert_kernel(be_ref, nu_ref, x_ref, wg_ref, wu_ref, wd_ref, y_ref):
    xb = x_ref[...].astype(BF16)
    g = _dot(xb, wg_ref[...])
    u = _dot(xb, wu_ref[...])
    hmid = (g * jax.nn.sigmoid(g) * u).astype(BF16)
    y_ref[...] = _dot(hmid, wd_ref[...])


def _expert_call(xs, block_e, n_used, wg, wu, wd):
    R, D = xs.shape
    nb = R // EXP_BM
    last = lambda b, nu: jnp.minimum(b, nu[0] - 1)
    grid_spec = pltpu.PrefetchScalarGridSpec(
        num_scalar_prefetch=2,
        grid=(nb,),
        in_specs=[
            pl.BlockSpec((EXP_BM, D), lambda b, be, nu: (last(b, nu), 0)),
            pl.BlockSpec((None, D, D_EXPERT), lambda b, be, nu: (be[last(b, nu)], 0, 0)),
            pl.BlockSpec((None, D, D_EXPERT), lambda b, be, nu: (be[last(b, nu)], 0, 0)),
            pl.BlockSpec((None, D_EXPERT, D), lambda b, be, nu: (be[last(b, nu)], 0, 0)),
        ],
        out_specs=pl.BlockSpec((EXP_BM, D), lambda b, be, nu: (b, 0)),
    )
    return pl.pallas_call(
        _expert_kernel,
        grid_spec=grid_spec,
        out_shape=jax.ShapeDtypeStruct((R, D), F32),
        compiler_params=pltpu.CompilerParams(
            dimension_semantics=("arbitrary",), vmem_limit_bytes=VMEM_LIMIT),
        name="experts",
    )(block_e, n_used, xs, wg, wu, wd)


def _combine_kernel(dest_ref, x1_ref, gate_ref, gf_ref, y_ref, out_ref, ybuf, sem):
    tf = x1_ref.shape[0]

    def issue(j, c):
        for kk in range(2):
            d = dest_ref[kk, j]
            pltpu.make_async_copy(y_ref.at[pl.ds(d, 1)], ybuf.at[kk, pl.ds(j, 1)], sem).start()
        return c

    lax.fori_loop(0, tf, issue, 0)
    for kk in range(2):
        pltpu.make_async_copy(y_ref.at[pl.ds(0, tf)], ybuf.at[kk], sem).wait()

    g = jnp.concatenate([gate_ref[...], jnp.zeros((LANE - 8, tf), F32)], axis=0)
    gt = g.T
    xo = x1_ref[...] + (gt[:, 0:1] * ybuf[0] + gt[:, 1:2] * ybuf[1])
    ms = jnp.mean(xo * xo, axis=-1, keepdims=True)
    out_ref[...] = xo * lax.rsqrt(ms + NORM_EPS) * gf_ref[...]


def _combine_call(x1, dest3, gates, gf, ys):
    T, D = x1.shape
    nt, _, tf = dest3.shape
    return pl.pallas_call(
        _combine_kernel,
        grid=(nt,),
        in_specs=[
            pl.BlockSpec((None, 2, tf), lambda i: (i, 0, 0), memory_space=pltpu.SMEM),
            pl.BlockSpec((tf, D), lambda i: (i, 0)),
            pl.BlockSpec((8, tf), lambda i: (0, i)),
            pl.BlockSpec((1, D), lambda i: (0, 0)),
            pl.BlockSpec(memory_space=pl.ANY),
        ],
        out_specs=pl.BlockSpec((tf, D), lambda i: (i, 0)),
        out_shape=jax.ShapeDtypeStruct((T, D), F32),
        scratch_shapes=[pltpu.VMEM((2, tf, D), F32), pltpu.SemaphoreType.DMA(())],
        compiler_params=pltpu.CompilerParams(
            dimension_semantics=("arbitrary",), vmem_limit_bytes=VMEM_LIMIT),
        name="combine",
    )(dest3, x1, gates, gf, ys)


def _pick_tile(n, want):
    t = min(want, n)
    while n % t:
        t //= 2
    return t


def _mixer_weights(norm1_g, w_in, w_alpha_up, b_alpha, gla_norm_g, w_gla_branch, pool_w, pool_scale,
                   w_pool_branch, w_out, norm2_g, w_rg, b_rg, w_re, b_re):
    c0 = 2 * KEY_W + 2 * VAL_W
    c1 = c0 + GATE_RANK
    c2 = c1 + POOL_W
    w_qkvr = w_in[:, :c0].astype(BF16)
    w_a = jnp.pad(w_in[:, c0:c1], ((0, 0), (0, LANE - GATE_RANK))).astype(BF16)
    w_u = w_in[:, c1:c2].astype(BF16)
    w_g = w_in[:, c2:].astype(BF16)
    w_alpha = jnp.pad(w_alpha_up, ((0, LANE - GATE_RANK), (0, 0))).astype(BF16)
    w_re_t = jnp.transpose(w_re, (0, 2, 1)).reshape(N_EXPERTS, D_MODEL)
    wrt = jnp.zeros((ROUTER_ROWS, D_MODEL), F32)
    wrt = wrt.at[0:N_GROUPS].set(w_rg.T).at[8:8 + N_EXPERTS].set(w_re_t).astype(BF16)
    brt = jnp.zeros((ROUTER_ROWS,), F32).at[0:N_GROUPS].set(b_rg).at[8:8 + N_EXPERTS].set(b_re.reshape(-1))
    brt = jnp.broadcast_to(brt[:, None], (ROUTER_ROWS, LANE))
    row = lambda a: a.reshape(1, -1).astype(F32)
    return (row(norm1_g), w_qkvr, w_a, w_u, w_g, w_alpha, row(b_alpha), row(gla_norm_g),
            w_gla_branch.astype(BF16), pool_w.astype(BF16), row(pool_scale), w_pool_branch.astype(BF16),
            w_out.astype(BF16), row(norm2_g), wrt, brt)


def kernel(x, norm1_g, w_in, w_alpha_up, b_alpha, gla_norm_g, w_gla_branch, pool_w, pool_scale,
           w_pool_branch, w_out, norm2_g, w_router_group, b_router_group, w_router_expert,
           b_router_expert, w_exp_gate, w_exp_up, w_exp_down, norm_f_g):
    B, S, D = x.shape
    T = B * S
    depth = w_in.shape[0]
    ts = _pick_tile(S, MIX_TS)
    tr = _pick_tile(T, ROW_TILE)
    n_assign = 2 * T
    n_blocks = -(-(n_assign + N_EXPERTS * (EXP_BM - 1)) // EXP_BM)
    n_rows = n_blocks * EXP_BM

    assert depth == 1, "kernel supports the problem's DEPTH=1"
    for l in range(depth):
        wts = _mixer_weights(norm1_g[l], w_in[l], w_alpha_up[l], b_alpha[l], gla_norm_g[l], w_gla_branch[l],
                             pool_w[l], pool_scale[l], w_pool_branch[l], w_out[l], norm2_g[l],
                             w_router_group[l], b_router_group[l], w_router_expert[l], b_router_expert[l])
        x1, h2, eid, gates, rank, cnt = _mixer_call(x, wts, ts)

        counts = cnt[:, 0].astype(jnp.int32)
        padded = ((counts + EXP_BM - 1) // EXP_BM) * EXP_BM
        pends = jnp.cumsum(padded)
        pstarts = pends - padded
        dest = pstarts[eid] + rank
        dest3 = dest.reshape(2, T // tr, tr).transpose(1, 0, 2)
        blk_start = jnp.arange(n_blocks, dtype=jnp.int32) * EXP_BM
        block_e = jnp.minimum(jnp.searchsorted(pends, blk_start, side='right'), N_EXPERTS - 1).astype(jnp.int32)
        n_used = (pends[-1:] // EXP_BM).astype(jnp.int32)

        xs = _dispatch_call(h2.reshape(T, D), dest3, pstarts.astype(jnp.int32), counts, n_rows)
        ys = _expert_call(xs, block_e, n_used, w_exp_gate[l].astype(BF16), w_exp_up[l].astype(BF16),
                          w_exp_down[l].astype(BF16))
        out = _combine_call(x1.reshape(T, D), dest3, gates, norm_f_g.reshape(1, D).astype(F32), ys)
        x = out.reshape(B, S, D)
    return x
```

```python
import functools

import jax
import jax.numpy as jnp
from jax import lax
from jax.experimental import pallas as pl
from jax.experimental.pallas import tpu as pltpu

F32 = jnp.float32
BF16 = jnp.bfloat16

D_MODEL = 1024
GLA_HEADS = 4
GLA_DK = 64
GLA_DV = 128
KEY_W = GLA_HEADS * GLA_DK
VAL_W = GLA_HEADS * GLA_DV
GATE_RANK = 16
GATE_NORMALIZER = 16.0
GLA_CHUNK = 64
POOL_WINDOWS = (2, 4, 8, 16)
POOL_W = 512
POOL_GW = 128
N_GROUPS = 4
EXPERTS_PER_GROUP = 8
N_EXPERTS = 32
D_EXPERT = 256
NORM_EPS = 1e-6

LANE = 128
SUBLANE = 8
POOL_HALO = 16
ROUTER_ROWS = 48

MIX_TS = 512
ROW_TILE = 512
EXP_BM = 256
VMEM_LIMIT = 56 * 1024 * 1024


def _dot(a, b):
    return jnp.dot(a, b, preferred_element_type=F32)


def _dot_nt(a, b):
    return lax.dot_general(a, b, (((1,), (1,)), ((), ())), preferred_element_type=F32)


def _dot_tn(a, b):
    return lax.dot_general(a, b, (((0,), (0,)), ((), ())), preferred_element_type=F32)


def _chunk_cumsum(x, chunk):
    n, w = x.shape
    pos = lax.broadcasted_iota(jnp.int32, (n, w), 0) % chunk
    step = 1
    while step < chunk:
        if step < 8:
            shifted = pltpu.roll(x, step, axis=0)
        else:
            shifted = jnp.concatenate([jnp.zeros((step, w), x.dtype), x[:n - step]], axis=0)
        x = x + jnp.where(pos >= step, shifted, 0.0)
        step *= 2
    return x


def _mixer_kernel(x_ref, g1_ref, wqkvr_ref, wa_ref, wu_ref, wg_ref, walpha_ref, balpha_ref,
                  glag_ref, wglab_ref, poolw_ref, pscale_ref, wpoolb_ref, wout_ref, g2_ref,
                  wrt_ref, brt_ref,
                  x1_ref, h2_ref, eid_ref, gate_ref, rank_ref, cnt_ref,
                  state_s, carry_s, cnt_s, qd_s, kd_s, ke_s, v_s, dec_s, o_s):
    ts = x_ref.shape[0]
    b_idx = pl.program_id(0)
    s_idx = pl.program_id(1)

    @pl.when(s_idx == 0)
    def _():
        state_s[...] = jnp.zeros_like(state_s)
        carry_s[...] = jnp.zeros_like(carry_s)

    @pl.when((b_idx == 0) & (s_idx == 0))
    def _():
        cnt_s[...] = jnp.zeros_like(cnt_s)

    xf = x_ref[...]
    ms = jnp.mean(xf * xf, axis=-1, keepdims=True)
    h = (xf * lax.rsqrt(ms + NORM_EPS) * g1_ref[...]).astype(BF16)

    qkvr = _dot(h, wqkvr_ref[...])
    q = qkvr[:, 0:KEY_W]
    k = qkvr[:, KEY_W:2 * KEY_W]
    v = qkvr[:, 2 * KEY_W:2 * KEY_W + VAL_W]
    r = qkvr[:, 2 * KEY_W + VAL_W:]

    a_low = _dot(h, wa_ref[...])
    z = _dot(a_low.astype(BF16), walpha_ref[...]) + balpha_ref[...]
    log_a = (jnp.minimum(z, 0.0) - jnp.log1p(jnp.exp(-jnp.abs(z)))) * (1.0 / GATE_NORMALIZER)

    nc = ts // GLA_CHUNK
    b = _chunk_cumsum(log_a, GLA_CHUNK)
    b3 = b.reshape(nc, GLA_CHUNK, KEY_W)
    b_last = b3[:, GLA_CHUNK - 1:GLA_CHUNK, :]
    qd_s[...] = (q * jnp.exp(b) * (GLA_DK ** -0.5)).astype(BF16)
    kd_s[...] = (k * jnp.exp(-b)).astype(BF16)
    ke_s[...] = (k.reshape(nc, GLA_CHUNK, KEY_W) * jnp.exp(b_last - b3)).reshape(ts, KEY_W).astype(BF16)
    dec_s[...] = jnp.broadcast_to(jnp.exp(b_last), (nc, GLA_CHUNK, KEY_W)).reshape(ts, KEY_W)
    v_s[...] = v.astype(BF16)

    lane = lax.broadcasted_iota(jnp.int32, (GLA_CHUNK, LANE), 1)
    lo_half = lane < GLA_DK
    lane_s = lax.broadcasted_iota(jnp.int32, (GLA_DV, LANE), 1)
    lo_half_s = lane_s < GLA_DK
    tri = (lax.broadcasted_iota(jnp.int32, (GLA_CHUNK, GLA_CHUNK), 0)
           >= lax.broadcasted_iota(jnp.int32, (GLA_CHUNK, GLA_CHUNK), 1))

    def chunk_body(c, carry):
        r0 = pl.multiple_of(c * GLA_CHUNK, GLA_CHUNK)
        rows = pl.ds(r0, GLA_CHUNK)
        for hd in range(GLA_HEADS):
            pair = slice((hd // 2) * LANE, (hd // 2 + 1) * LANE)
            mine = lo_half if hd % 2 == 0 else jnp.logical_not(lo_half)
            mine_s = lo_half_s if hd % 2 == 0 else jnp.logical_not(lo_half_s)
            qd = jnp.where(mine, qd_s[rows, pair], jnp.zeros((), BF16))
            kd = kd_s[rows, pair]
            ke = ke_s[rows, pair]
            vh = v_s[rows, hd * GLA_DV:(hd + 1) * GLA_DV]
            dec = dec_s[pl.ds(r0, 8), pair][0:1, :]
            st = state_s[hd]
            scores = jnp.where(tri, _dot_nt(qd, kd), 0.0)
            o = _dot(scores.astype(BF16), vh) + _dot_nt(qd, st.astype(BF16))
            o_s[rows, hd * GLA_DV:(hd + 1) * GLA_DV] = o
            inc_t = _dot_tn(vh, ke)
            state_s[hd] = jnp.where(mine_s, dec * st + inc_t, 0.0)
        return carry

    lax.fori_loop(0, nc, chunk_body, 0)

    o_all = o_s[...]
    parts = []
    for hd in range(GLA_HEADS):
        oh = o_all[:, hd * GLA_DV:(hd + 1) * GLA_DV]
        parts.append(oh * lax.rsqrt(jnp.mean(oh * oh, axis=-1, keepdims=True) + NORM_EPS))
    o_n = jnp.concatenate(parts, axis=-1) * glag_ref[...]
    o_g = (o_n * (r * jax.nn.sigmoid(r))).astype(BF16)
    y_gla = _dot(o_g, wglab_ref[...])

    u = _dot(h, wu_ref[...])
    ext = jnp.concatenate([carry_s[...], u], axis=0)
    carry_s[...] = u[ts - POOL_HALO:, :]
    pos = (s_idx * ts + lax.broadcasted_iota(jnp.int32, (ts, 1), 0)).astype(F32)
    mixed = []
    for gi, w in enumerate(POOL_WINDOWS):
        a = ext[:, gi * POOL_GW:(gi + 1) * POOL_GW]
        step = 1
        while step < w:
            a = a + pltpu.roll(a, step, axis=0)
            step *= 2
        pooled = a[POOL_HALO:, :] / jnp.minimum(pos + 1.0, float(w))
        diff = pooled - u[:, gi * POOL_GW:(gi + 1) * POOL_GW]
        mixed.append(_dot(diff.astype(BF16), poolw_ref[gi]))
    pm = (jnp.concatenate(mixed, axis=-1) * pscale_ref[...]).astype(BF16)
    y_pool = _dot(pm, wpoolb_ref[...])

    gates = _dot(h, wg_ref[...])
    merged = (jax.nn.sigmoid(gates[:, :D_MODEL]) * y_gla
              + jax.nn.sigmoid(gates[:, D_MODEL:]) * y_pool).astype(BF16)
    x1 = xf + _dot(merged, wout_ref[...])
    x1_ref[...] = x1

    ms2 = jnp.mean(x1 * x1, axis=-1, keepdims=True)
    h2 = x1 * lax.rsqrt(ms2 + NORM_EPS) * g2_ref[...]
    h2_ref[...] = h2
    lt = _dot_nt(wrt_ref[...], h2.astype(BF16)) + brt_ref[:, 0:1]

    l0, l1, l2, l3 = lt[0:1], lt[1:2], lt[2:3], lt[3:4]
    gm = jnp.maximum(jnp.maximum(l0, l1), jnp.maximum(l2, l3))
    gidx = jnp.where(l0 == gm, 0, jnp.where(l1 == gm, 1, jnp.where(l2 == gm, 2, 3)))
    gp = 1.0 / (jnp.exp(l0 - gm) + jnp.exp(l1 - gm) + jnp.exp(l2 - gm) + jnp.exp(l3 - gm))
    el = jnp.where(gidx == 0, lt[8:16],
                   jnp.where(gidx == 1, lt[16:24], jnp.where(gidx == 2, lt[24:32], lt[32:40])))
    row8 = lax.broadcasted_iota(jnp.int32, (EXPERTS_PER_GROUP, ts), 0)
    m1 = jnp.max(el, axis=0, keepdims=True)
    i1 = jnp.min(jnp.where(el == m1, row8, EXPERTS_PER_GROUP), axis=0, keepdims=True)
    el2 = jnp.where(row8 == i1, -jnp.inf, el)
    m2 = jnp.max(el2, axis=0, keepdims=True)
    i2 = jnp.min(jnp.where(el2 == m2, row8, EXPERTS_PER_GROUP), axis=0, keepdims=True)
    dd = jnp.exp(m2 - m1)
    p1 = 1.0 / (1.0 + dd)
    p2 = dd / (1.0 + dd)
    e1 = gidx * EXPERTS_PER_GROUP + i1
    e2 = gidx * EXPERTS_PER_GROUP + i2
    eid_ref[0:1, :] = e1
    eid_ref[1:2, :] = e2
    gate_ref[...] = jnp.concatenate([gp * p1, gp * p2, jnp.zeros((6, ts), F32)], axis=0)

    row32 = lax.broadcasted_iota(jnp.int32, (N_EXPERTS, ts), 0)
    oh1 = row32 == e1
    oh2 = row32 == e2
    member = jnp.where(oh1, 1.0, 0.0) + jnp.where(oh2, 1.0, 0.0)
    before = (lax.broadcasted_iota(jnp.int32, (ts, ts), 0)
              < lax.broadcasted_iota(jnp.int32, (ts, ts), 1))
    prefix = _dot(member.astype(BF16), jnp.where(before, 1.0, 0.0).astype(BF16))
    base = cnt_s[...]
    prefix = prefix + jnp.concatenate([base] * (ts // LANE), axis=1)
    rank_ref[0:1, :] = jnp.sum(jnp.where(oh1, prefix, 0.0), axis=0, keepdims=True).astype(jnp.int32)
    rank_ref[1:2, :] = jnp.sum(jnp.where(oh2, prefix, 0.0), axis=0, keepdims=True).astype(jnp.int32)
    new_cnt = base + jnp.sum(member, axis=1, keepdims=True)
    cnt_s[...] = new_cnt
    cnt_ref[...] = new_cnt


def _const_spec(shape):
    nd = len(shape)
    return pl.BlockSpec(shape, lambda b, s, _nd=nd: (0,) * _nd, pipeline_mode=pl.Buffered(1))


def _mixer_call(x, wts, ts):
    B, S, D = x.shape
    T = B * S
    ns = S // ts
    tok_spec = pl.BlockSpec((None, ts, D), lambda b, s: (b, s, 0))
    flat_tok = lambda rows: pl.BlockSpec((rows, ts), lambda b, s: (0, b * ns + s))
    in_specs = [tok_spec] + [_const_spec(w.shape) for w in wts]
    out_shape = (
        jax.ShapeDtypeStruct((B, S, D), F32),
        jax.ShapeDtypeStruct((B, S, D), F32),
        jax.ShapeDtypeStruct((2, T), jnp.int32),
        jax.ShapeDtypeStruct((8, T), F32),
        jax.ShapeDtypeStruct((2, T), jnp.int32),
        jax.ShapeDtypeStruct((N_EXPERTS, LANE), F32),
    )
    out_specs = (tok_spec, tok_spec, flat_tok(2), flat_tok(8), flat_tok(2),
                 pl.BlockSpec((N_EXPERTS, LANE), lambda b, s: (0, 0)))
    scratch = [
        pltpu.VMEM((GLA_HEADS, GLA_DV, LANE), F32),
        pltpu.VMEM((POOL_HALO, POOL_W), F32),
        pltpu.VMEM((N_EXPERTS, LANE), F32),
        pltpu.VMEM((ts, KEY_W), BF16),
        pltpu.VMEM((ts, KEY_W), BF16),
        pltpu.VMEM((ts, KEY_W), BF16),
        pltpu.VMEM((ts, VAL_W), BF16),
        pltpu.VMEM((ts, KEY_W), F32),
        pltpu.VMEM((ts, VAL_W), F32),
    ]
    return pl.pallas_call(
        _mixer_kernel,
        grid=(B, ns),
        in_specs=in_specs,
        out_specs=out_specs,
        out_shape=out_shape,
        scratch_shapes=scratch,
        compiler_params=pltpu.CompilerParams(
            dimension_semantics=("arbitrary", "arbitrary"), vmem_limit_bytes=VMEM_LIMIT),
        name="mixer",
    )(x, *wts)


_PAD_BITS = tuple(p for p in (1 << i for i in reversed(range((EXP_BM - 1).bit_length()))) if p >= SUBLANE)


def _dispatch_kernel(pst_ref, cnt_ref, dest_ref, h2_ref, xs_ref, zero_s, sem, zsem):
    td = h2_ref.shape[0] * SUBLANE
    i = pl.program_id(0)

    def zero_copy(off, p):
        return pltpu.make_async_copy(zero_s.at[pl.ds(0, p)], xs_ref.at[pl.ds(off, p)], zsem)

    def for_each_pad_piece(e, fn):
        cnt = cnt_ref[e]
        start = pst_ref[e] + cnt
        end = start + (EXP_BM - cnt % EXP_BM) % EXP_BM
        aligned = jnp.minimum((start + SUBLANE - 1) // SUBLANE * SUBLANE, end)
        for j in range(SUBLANE - 1):
            @pl.when(start + j < aligned)
            def _(j=j):
                fn(zero_copy(start + j, 1))

        npad = end - aligned
        off = aligned
        for p in _PAD_BITS:
            hit = (npad & p) != 0

            @pl.when(hit)
            def _(off=off, p=p):
                fn(zero_copy(pl.multiple_of(off, SUBLANE), p))

            off = off + jnp.where(hit, p, 0)

    @pl.when(i == 0)
    def _():
        zero_s[...] = jnp.zeros_like(zero_s)

        def start_e(e, c):
            for_each_pad_piece(e, lambda cp: cp.start())
            return c

        def wait_e(e, c):
            for_each_pad_piece(e, lambda cp: cp.wait())
            return c

        lax.fori_loop(0, N_EXPERTS, start_e, 0)
        lax.fori_loop(0, N_EXPERTS, wait_e, 0)

        zrows = zero_s.shape[0]
        last_cnt = cnt_ref[N_EXPERTS - 1]
        used = pst_ref[N_EXPERTS - 1] + (last_cnt + EXP_BM - 1) // EXP_BM * EXP_BM
        first_piece = used // zrows
        n_pieces = xs_ref.shape[0] // zrows

        def tail_copy(t):
            return zero_copy(pl.multiple_of(t * zrows, zrows), zrows)

        lax.fori_loop(first_piece, n_pieces, lambda t, c: (tail_copy(t).start(), c)[1], 0)
        lax.fori_loop(first_piece, n_pieces, lambda t, c: (tail_copy(t).wait(), c)[1], 0)

    def issue(g, c):
        for u in range(SUBLANE):
            for kk in range(2):
                d = dest_ref[kk * td + g * SUBLANE + u]
                pltpu.make_async_copy(h2_ref.at[g, pl.ds(u, 1)], xs_ref.at[pl.ds(d, 1)], sem).start()
        return c

    lax.fori_loop(0, td // SUBLANE, issue, 0)
    for kk in range(2):
        pltpu.make_async_copy(xs_ref.at[pl.ds(0, td)], xs_ref.at[pl.ds(td, td)], sem).wait()


def _dispatch_call(h2, dest3, pstarts, counts, n_rows):
    T, D = h2.shape
    nt, _, td2 = dest3.shape
    td = td2 // 2
    h2 = h2.reshape(T // SUBLANE, SUBLANE, D)
    grid_spec = pltpu.PrefetchScalarGridSpec(
        num_scalar_prefetch=2,
        grid=(nt,),
        in_specs=[
            pl.BlockSpec((None, None, 2 * td), lambda i, *_: (i, 0, 0), memory_space=pltpu.SMEM),
            pl.BlockSpec((td // SUBLANE, SUBLANE, D), lambda i, *_: (i, 0, 0)),
        ],
        out_specs=pl.BlockSpec(memory_space=pl.ANY),
        scratch_shapes=[
            pltpu.VMEM((_PAD_BITS[0], D), F32),
            pltpu.SemaphoreType.DMA(()),
            pltpu.SemaphoreType.DMA(()),
        ],
    )
    return pl.pallas_call(
        _dispatch_kernel,
        grid_spec=grid_spec,
        out_shape=jax.ShapeDtypeStruct((n_rows, D), F32),
        compiler_params=pltpu.CompilerParams(
            dimension_semantics=("arbitrary",), vmem_limit_bytes=VMEM_LIMIT, has_side_effects=True),
        name="dispatch",
    )(pstarts, counts, dest3, h2)


def _expert_kernel(be_ref, nu_ref, x_ref, wg_ref, wu_ref, wd_ref, y_ref):
    xb = x_ref[...].astype(BF16)
    g = _dot(xb, wg_ref[...])
    u = _dot(xb, wu_ref[...])
    hmid = (g * jax.nn.sigmoid(g) * u).astype(BF16)
    y_ref[...] = _dot(hmid, wd_ref[...])


def _expert_call(xs, block_e, n_used, wg, wu, wd):
    R, D = xs.shape
    nb = R // EXP_BM
    last = lambda b, nu: jnp.minimum(b, nu[0] - 1)
    grid_spec = pltpu.PrefetchScalarGridSpec(
        num_scalar_prefetch=2,
        grid=(nb,),
        in_specs=[
            pl.BlockSpec((EXP_BM, D), lambda b, be, nu: (last(b, nu), 0)),
            pl.BlockSpec((None, D, D_EXPERT), lambda b, be, nu: (be[last(b, nu)], 0, 0)),
            pl.BlockSpec((None, D, D_EXPERT), lambda b, be, nu: (be[last(b, nu)], 0, 0)),
            pl.BlockSpec((None, D_EXPERT, D), lambda b, be, nu: (be[last(b, nu)], 0, 0)),
        ],
        out_specs=pl.BlockSpec((EXP_BM, D), lambda b, be, nu: (b, 0)),
    )
    return pl.pallas_call(
        _expert_kernel,
        grid_spec=grid_spec,
        out_shape=jax.ShapeDtypeStruct((R, D), F32),
        compiler_params=pltpu.CompilerParams(
            dimension_semantics=("arbitrary",), vmem_limit_bytes=VMEM_LIMIT),
        name="experts",
    )(block_e, n_used, xs, wg, wu, wd)


def _combine_kernel(dcur_ref, dnext_ref, x1_ref, gate_ref, gf_ref, y_ref, out_ref, ybuf, sem):
    tf = x1_ref.shape[0]
    i = pl.program_id(0)
    n = pl.num_programs(0)

    def issue_all(dref, slot):
        def group(g, c):
            for u in range(SUBLANE):
                for kk in range(2):
                    d = dref[kk * tf + g * SUBLANE + u]
                    pltpu.make_async_copy(y_ref.at[pl.ds(d, 1)], ybuf.at[slot, kk, g, pl.ds(u, 1)],
                                          sem.at[slot]).start()
            return c

        lax.fori_loop(0, tf // SUBLANE, group, 0)

    @pl.when(i == 0)
    def _():
        issue_all(dcur_ref, 0)

    for par in range(2):
        @pl.when((i % 2 == par) & (i + 1 < n))
        def _(par=par):
            issue_all(dnext_ref, 1 - par)

    slot = i % 2
    for kk in range(2):
        pltpu.make_async_copy(y_ref.at[pl.ds(0, tf)], y_ref.at[pl.ds(tf, tf)], sem.at[slot]).wait()

    g = jnp.concatenate([gate_ref[...], jnp.zeros((LANE - 8, tf), F32)], axis=0)
    gt = g.T
    y0 = ybuf[slot, 0].reshape(tf, -1)
    y1 = ybuf[slot, 1].reshape(tf, -1)
    xo = x1_ref[...] + (gt[:, 0:1] * y0 + gt[:, 1:2] * y1)
    ms = jnp.mean(xo * xo, axis=-1, keepdims=True)
    out_ref[...] = xo * lax.rsqrt(ms + NORM_EPS) * gf_ref[...]


def _combine_call(x1, dest3, gates, gf, ys):
    T, D = x1.shape
    nt, _, tf2 = dest3.shape
    tf = tf2 // 2
    return pl.pallas_call(
        _combine_kernel,
        grid=(nt,),
        in_specs=[
            pl.BlockSpec((None, None, 2 * tf), lambda i: (i, 0, 0), memory_space=pltpu.SMEM),
            pl.BlockSpec((None, None, 2 * tf), lambda i: (jnp.minimum(i + 1, nt - 1), 0, 0),
                         memory_space=pltpu.SMEM),
            pl.BlockSpec((tf, D), lambda i: (i, 0)),
            pl.BlockSpec((8, tf), lambda i: (0, i)),
            pl.BlockSpec((1, D), lambda i: (0, 0)),
            pl.BlockSpec(memory_space=pl.ANY),
        ],
        out_specs=pl.BlockSpec((tf, D), lambda i: (i, 0)),
        out_shape=jax.ShapeDtypeStruct((T, D), F32),
        scratch_shapes=[pltpu.VMEM((2, 2, tf // SUBLANE, SUBLANE, D), F32), pltpu.SemaphoreType.DMA((2,))],
        compiler_params=pltpu.CompilerParams(
            dimension_semantics=("arbitrary",), vmem_limit_bytes=VMEM_LIMIT),
        name="combine",
    )(dest3, dest3, x1, gates, gf, ys)


def _pick_tile(n, want):
    t = min(want, n)
    while n % t:
        t //= 2
    return t


def _mixer_weights(norm1_g, w_in, w_alpha_up, b_alpha, gla_norm_g, w_gla_branch, pool_w, pool_scale,
                   w_pool_branch, w_out, norm2_g, w_rg, b_rg, w_re, b_re):
    c0 = 2 * KEY_W + 2 * VAL_W
    c1 = c0 + GATE_RANK
    c2 = c1 + POOL_W
    w_qkvr = w_in[:, :c0].astype(BF16)
    w_a = jnp.pad(w_in[:, c0:c1], ((0, 0), (0, LANE - GATE_RANK))).astype(BF16)
    w_u = w_in[:, c1:c2].astype(BF16)
    w_g = w_in[:, c2:].astype(BF16)
    w_alpha = jnp.pad(w_alpha_up, ((0, LANE - GATE_RANK), (0, 0))).astype(BF16)
    w_re_t = jnp.transpose(w_re, (0, 2, 1)).reshape(N_EXPERTS, D_MODEL)
    wrt = jnp.zeros((ROUTER_ROWS, D_MODEL), F32)
    wrt = wrt.at[0:N_GROUPS].set(w_rg.T).at[8:8 + N_EXPERTS].set(w_re_t).astype(BF16)
    brt = jnp.zeros((ROUTER_ROWS,), F32).at[0:N_GROUPS].set(b_rg).at[8:8 + N_EXPERTS].set(b_re.reshape(-1))
    brt = jnp.broadcast_to(brt[:, None], (ROUTER_ROWS, LANE))
    row = lambda a: a.reshape(1, -1).astype(F32)
    return (row(norm1_g), w_qkvr, w_a, w_u, w_g, w_alpha, row(b_alpha), row(gla_norm_g),
            w_gla_branch.astype(BF16), pool_w.astype(BF16), row(pool_scale), w_pool_branch.astype(BF16),
            w_out.astype(BF16), row(norm2_g), wrt, brt)


def kernel(x, norm1_g, w_in, w_alpha_up, b_alpha, gla_norm_g, w_gla_branch, pool_w, pool_scale,
           w_pool_branch, w_out, norm2_g, w_router_group, b_router_group, w_router_expert,
           b_router_expert, w_exp_gate, w_exp_up, w_exp_down, norm_f_g):
    B, S, D = x.shape
    T = B * S
    depth = w_in.shape[0]
    ts = _pick_tile(S, MIX_TS)
    tr = _pick_tile(T, ROW_TILE)
    n_assign = 2 * T
    n_blocks = -(-(n_assign + N_EXPERTS * (EXP_BM - 1)) // EXP_BM)
    n_rows = n_blocks * EXP_BM

    assert depth == 1, "kernel supports the problem's DEPTH=1"
    for l in range(depth):
        wts = _mixer_weights(norm1_g[l], w_in[l], w_alpha_up[l], b_alpha[l], gla_norm_g[l], w_gla_branch[l],
                             pool_w[l], pool_scale[l], w_pool_branch[l], w_out[l], norm2_g[l],
                             w_router_group[l], b_router_group[l], w_router_expert[l], b_router_expert[l])
        x1, h2, eid, gates, rank, cnt = _mixer_call(x, wts, ts)

        counts = cnt[:, 0].astype(jnp.int32)
        padded = ((counts + EXP_BM - 1) // EXP_BM) * EXP_BM
        pends = jnp.cumsum(padded)
        pstarts = pends - padded
        expert_col = jnp.arange(N_EXPERTS, dtype=jnp.int32)[:, None, None]
        dest = jnp.sum(jnp.where(eid[None] == expert_col, pstarts[:, None, None], 0), axis=0) + rank
        dest3 = dest.reshape(2, T // tr, tr).transpose(1, 0, 2).reshape(T // tr, 1, 2 * tr)
        blk_start = jnp.arange(n_blocks, dtype=jnp.int32) * EXP_BM
        block_e = jnp.minimum(jnp.sum(pends[None, :] <= blk_start[:, None], axis=1), N_EXPERTS - 1).astype(jnp.int32)
        n_used = (pends[-1:] // EXP_BM).astype(jnp.int32)

        xs = _dispatch_call(h2.reshape(T, D), dest3, pstarts.astype(jnp.int32), counts, n_rows)
        ys = _expert_call(xs, block_e, n_used, w_exp_gate[l].astype(BF16), w_exp_up[l].astype(BF16),
                          w_exp_down[l].astype(BF16))
        out = _combine_call(x1.reshape(T, D), dest3, gates, norm_f_g.reshape(1, D).astype(F32), ys)
        x = out.reshape(B, S, D)
    return x
```

```python
import functools

import jax
import jax.numpy as jnp
from jax import lax
from jax.experimental import pallas as pl
from jax.experimental.pallas import tpu as pltpu

F32 = jnp.float32
BF16 = jnp.bfloat16

D_MODEL = 1024
GLA_HEADS = 4
GLA_DK = 64
GLA_DV = 128
KEY_W = GLA_HEADS * GLA_DK
VAL_W = GLA_HEADS * GLA_DV
GATE_RANK = 16
GATE_NORMALIZER = 16.0
GLA_CHUNK = 64
POOL_WINDOWS = (2, 4, 8, 16)
POOL_W = 512
POOL_GW = 128
N_GROUPS = 4
EXPERTS_PER_GROUP = 8
N_EXPERTS = 32
D_EXPERT = 256
NORM_EPS = 1e-6

LANE = 128
SUBLANE = 8
POOL_HALO = 16
ROUTER_ROWS = 48

MIX_TS = 512
ROW_TILE = 512
EXP_BM = 256
VMEM_LIMIT = 56 * 1024 * 1024


def _dot(a, b):
    return jnp.dot(a, b, preferred_element_type=F32)


def _dot_nt(a, b):
    return lax.dot_general(a, b, (((1,), (1,)), ((), ())), preferred_element_type=F32)


def _dot_tn(a, b):
    return lax.dot_general(a, b, (((0,), (0,)), ((), ())), preferred_element_type=F32)


def _bf16_bits(x):
    u = lax.bitcast_convert_type(x, jnp.int32)
    return (u + 0x7FFF + (lax.shift_right_logical(u, 16) & 1)) & -65536


def _pack_rows(x):
    w = x.shape[1] // 2
    return lax.shift_right_logical(_bf16_bits(x[:, :w]), 16) | _bf16_bits(x[:, w:])


def _unpack_rows(words):
    lo = lax.bitcast_convert_type(lax.shift_left(words, 16), F32)
    hi = lax.bitcast_convert_type(words & -65536, F32)
    return lo, hi


def _chunk_cumsum(x, chunk):
    n, w = x.shape
    pos = lax.broadcasted_iota(jnp.int32, (n, w), 0) % chunk
    step = 1
    while step < chunk:
        if step < 8:
            shifted = pltpu.roll(x, step, axis=0)
        else:
            shifted = jnp.concatenate([jnp.zeros((step, w), x.dtype), x[:n - step]], axis=0)
        x = x + jnp.where(pos >= step, shifted, 0.0)
        step *= 2
    return x


def _mixer_kernel(x_ref, g1_ref, wqkvr_ref, wa_ref, wu_ref, wg_ref, walpha_ref, balpha_ref,
                  glag_ref, wglab_ref, poolw_ref, pscale_ref, wpoolb_ref, wout_ref, g2_ref,
                  wrt_ref, brt_ref,
                  x1_ref, h2_ref, eid_ref, gate_ref, rank_ref, cnt_ref,
                  state_s, carry_s, cnt_s, qdm_s, kem_s, kd_s, v_s, dec_s, o_s):
    ts = x_ref.shape[0]
    b_idx = pl.program_id(0)
    s_idx = pl.program_id(1)

    @pl.when(s_idx == 0)
    def _():
        state_s[...] = jnp.zeros_like(state_s)
        carry_s[...] = jnp.zeros_like(carry_s)

    @pl.when((b_idx == 0) & (s_idx == 0))
    def _():
        cnt_s[...] = jnp.zeros_like(cnt_s)

    xf = x_ref[...]
    ms = jnp.mean(xf * xf, axis=-1, keepdims=True)
    h = (xf * lax.rsqrt(ms + NORM_EPS) * g1_ref[...]).astype(BF16)

    qkvr = _dot(h, wqkvr_ref[...])
    q = qkvr[:, 0:KEY_W]
    k = qkvr[:, KEY_W:2 * KEY_W]
    v = qkvr[:, 2 * KEY_W:2 * KEY_W + VAL_W]
    r = qkvr[:, 2 * KEY_W + VAL_W:]

    a_low = _dot(h, wa_ref[...])
    z = _dot(a_low.astype(BF16), walpha_ref[...]) + balpha_ref[...]
    log_a = (jnp.minimum(z, 0.0) - jnp.log1p(jnp.exp(-jnp.abs(z)))) * (1.0 / GATE_NORMALIZER)

    nc = ts // GLA_CHUNK
    b = _chunk_cumsum(log_a, GLA_CHUNK)
    b3 = b.reshape(nc, GLA_CHUNK, KEY_W)
    b_last = b3[:, GLA_CHUNK - 1:GLA_CHUNK, :]
    lane = lax.broadcasted_iota(jnp.int32, (ts, LANE), 1)
    qd = q * jnp.exp(b) * (GLA_DK ** -0.5)
    ke = (k.reshape(nc, GLA_CHUNK, KEY_W) * jnp.exp(b_last - b3)).reshape(ts, KEY_W)
    for hd in range(GLA_HEADS):
        pair = slice((hd // 2) * LANE, (hd // 2 + 1) * LANE)
        mine = (lane < GLA_DK) if hd % 2 == 0 else (lane >= GLA_DK)
        qdm_s[hd] = jnp.where(mine, qd[:, pair], 0.0).astype(BF16)
        kem_s[hd] = jnp.where(mine, ke[:, pair], 0.0).astype(BF16)
    kd_s[...] = (k * jnp.exp(-b)).astype(BF16)
    dec_s[...] = jnp.broadcast_to(jnp.exp(b_last), (nc, SUBLANE, KEY_W))
    v_s[...] = v.astype(BF16)

    tri = (lax.broadcasted_iota(jnp.int32, (GLA_CHUNK, GLA_CHUNK), 0)
           >= lax.broadcasted_iota(jnp.int32, (GLA_CHUNK, GLA_CHUNK), 1))

    def chunk_body(c, carry):
        rows = pl.ds(pl.multiple_of(c * GLA_CHUNK, GLA_CHUNK), GLA_CHUNK)
        for hd in range(GLA_HEADS):
            pair = slice((hd // 2) * LANE, (hd // 2 + 1) * LANE)
            cols = slice(hd * GLA_DV, (hd + 1) * GLA_DV)
            qm = qdm_s[hd, rows, :]
            vh = v_s[rows, cols]
            st = state_s[hd]
            scores = jnp.where(tri, _dot_nt(qm, kd_s[rows, pair]), 0.0)
            o_s[rows, cols] = _dot(scores.astype(BF16), vh) + _dot_nt(qm, st.astype(BF16))
            state_s[hd] = dec_s[c, 0:1, pair] * st + _dot_tn(vh, kem_s[hd, rows, :])
        return carry

    lax.fori_loop(0, nc, chunk_body, 0)

    o_all = o_s[...]
    parts = []
    for hd in range(GLA_HEADS):
        oh = o_all[:, hd * GLA_DV:(hd + 1) * GLA_DV]
        parts.append(oh * lax.rsqrt(jnp.mean(oh * oh, axis=-1, keepdims=True) + NORM_EPS))
    o_n = jnp.concatenate(parts, axis=-1) * glag_ref[...]
    o_g = (o_n * (r * jax.nn.sigmoid(r))).astype(BF16)
    y_gla = _dot(o_g, wglab_ref[...])

    u = _dot(h, wu_ref[...])
    ext = jnp.concatenate([carry_s[...], u], axis=0)
    carry_s[...] = u[ts - POOL_HALO:, :]
    pos = (s_idx * ts + lax.broadcasted_iota(jnp.int32, (ts, 1), 0)).astype(F32)
    mixed = []
    for gi, w in enumerate(POOL_WINDOWS):
        a = ext[:, gi * POOL_GW:(gi + 1) * POOL_GW]
        step = 1
        while step < w:
            a = a + pltpu.roll(a, step, axis=0)
            step *= 2
        pooled = a[POOL_HALO:, :] / jnp.minimum(pos + 1.0, float(w))
        diff = pooled - u[:, gi * POOL_GW:(gi + 1) * POOL_GW]
        mixed.append(_dot(diff.astype(BF16), poolw_ref[gi]))
    pm = (jnp.concatenate(mixed, axis=-1) * pscale_ref[...]).astype(BF16)
    y_pool = _dot(pm, wpoolb_ref[...])

    gates = _dot(h, wg_ref[...])
    merged = (jax.nn.sigmoid(gates[:, :D_MODEL]) * y_gla
              + jax.nn.sigmoid(gates[:, D_MODEL:]) * y_pool).astype(BF16)
    x1 = xf + _dot(merged, wout_ref[...])
    x1_ref[...] = x1

    ms2 = jnp.mean(x1 * x1, axis=-1, keepdims=True)
    h2 = x1 * lax.rsqrt(ms2 + NORM_EPS) * g2_ref[...]
    h2_ref[...] = _pack_rows(h2)
    lt = _dot_nt(wrt_ref[...], h2.astype(BF16)) + brt_ref[:, 0:1]

    l0, l1, l2, l3 = lt[0:1], lt[1:2], lt[2:3], lt[3:4]
    gm = jnp.maximum(jnp.maximum(l0, l1), jnp.maximum(l2, l3))
    gidx = jnp.where(l0 == gm, 0, jnp.where(l1 == gm, 1, jnp.where(l2 == gm, 2, 3)))
    gp = 1.0 / (jnp.exp(l0 - gm) + jnp.exp(l1 - gm) + jnp.exp(l2 - gm) + jnp.exp(l3 - gm))
    el = jnp.where(gidx == 0, lt[8:16],
                   jnp.where(gidx == 1, lt[16:24], jnp.where(gidx == 2, lt[24:32], lt[32:40])))
    row8 = lax.broadcasted_iota(jnp.int32, (EXPERTS_PER_GROUP, ts), 0)
    m1 = jnp.max(el, axis=0, keepdims=True)
    i1 = jnp.min(jnp.where(el == m1, row8, EXPERTS_PER_GROUP), axis=0, keepdims=True)
    el2 = jnp.where(row8 == i1, -jnp.inf, el)
    m2 = jnp.max(el2, axis=0, keepdims=True)
    i2 = jnp.min(jnp.where(el2 == m2, row8, EXPERTS_PER_GROUP), axis=0, keepdims=True)
    dd = jnp.exp(m2 - m1)
    p1 = 1.0 / (1.0 + dd)
    p2 = dd / (1.0 + dd)
    e1 = gidx * EXPERTS_PER_GROUP + i1
    e2 = gidx * EXPERTS_PER_GROUP + i2
    eid_ref[0:1, :] = e1
    eid_ref[1:2, :] = e2
    gate_ref[...] = jnp.concatenate([gp * p1, gp * p2, jnp.zeros((6, ts), F32)], axis=0)

    row32 = lax.broadcasted_iota(jnp.int32, (N_EXPERTS, ts), 0)
    oh1 = row32 == e1
    oh2 = row32 == e2
    member = jnp.where(oh1, 1.0, 0.0) + jnp.where(oh2, 1.0, 0.0)
    before = (lax.broadcasted_iota(jnp.int32, (ts, ts), 0)
              < lax.broadcasted_iota(jnp.int32, (ts, ts), 1))
    prefix = _dot(member.astype(BF16), jnp.where(before, 1.0, 0.0).astype(BF16))
    base = cnt_s[...]
    prefix = prefix + jnp.concatenate([base] * (ts // LANE), axis=1)
    rank_ref[0:1, :] = jnp.sum(jnp.where(oh1, prefix, 0.0), axis=0, keepdims=True).astype(jnp.int32)
    rank_ref[1:2, :] = jnp.sum(jnp.where(oh2, prefix, 0.0), axis=0, keepdims=True).astype(jnp.int32)
    new_cnt = base + jnp.sum(member, axis=1, keepdims=True)
    cnt_s[...] = new_cnt
    cnt_ref[...] = new_cnt


def _const_spec(shape):
    nd = len(shape)
    return pl.BlockSpec(shape, lambda b, s, _nd=nd: (0,) * _nd, pipeline_mode=pl.Buffered(1))


def _mixer_call(x, wts, ts):
    B, S, D = x.shape
    T = B * S
    ns = S // ts
    tok_spec = pl.BlockSpec((None, ts, D), lambda b, s: (b, s, 0))
    flat_tok = lambda rows: pl.BlockSpec((rows, ts), lambda b, s: (0, b * ns + s))
    in_specs = [tok_spec] + [_const_spec(w.shape) for w in wts]
    out_shape = (
        jax.ShapeDtypeStruct((B, S, D), F32),
        jax.ShapeDtypeStruct((B, S, D // 2), jnp.int32),
        jax.ShapeDtypeStruct((2, T), jnp.int32),
        jax.ShapeDtypeStruct((8, T), F32),
        jax.ShapeDtypeStruct((2, T), jnp.int32),
        jax.ShapeDtypeStruct((N_EXPERTS, LANE), F32),
    )
    packed_spec = pl.BlockSpec((None, ts, D // 2), lambda b, s: (b, s, 0))
    out_specs = (tok_spec, packed_spec, flat_tok(2), flat_tok(8), flat_tok(2),
                 pl.BlockSpec((N_EXPERTS, LANE), lambda b, s: (0, 0)))
    scratch = [
        pltpu.VMEM((GLA_HEADS, GLA_DV, LANE), F32),
        pltpu.VMEM((POOL_HALO, POOL_W), F32),
        pltpu.VMEM((N_EXPERTS, LANE), F32),
        pltpu.VMEM((GLA_HEADS, ts, LANE), BF16),
        pltpu.VMEM((GLA_HEADS, ts, LANE), BF16),
        pltpu.VMEM((ts, KEY_W), BF16),
        pltpu.VMEM((ts, VAL_W), BF16),
        pltpu.VMEM((ts // GLA_CHUNK, SUBLANE, KEY_W), F32),
        pltpu.VMEM((ts, VAL_W), F32),
    ]
    return pl.pallas_call(
        _mixer_kernel,
        grid=(B, ns),
        in_specs=in_specs,
        out_specs=out_specs,
        out_shape=out_shape,
        scratch_shapes=scratch,
        compiler_params=pltpu.CompilerParams(
            dimension_semantics=("arbitrary", "arbitrary"), vmem_limit_bytes=VMEM_LIMIT),
        name="mixer",
    )(x, *wts)


_PAD_BITS = tuple(p for p in (1 << i for i in reversed(range((EXP_BM - 1).bit_length()))) if p >= SUBLANE)


def _dispatch_kernel(pst_ref, cnt_ref, dest_ref, h2_ref, xs_ref, zero_s, sem, zsem):
    td = h2_ref.shape[0] * SUBLANE
    i = pl.program_id(0)

    def zero_copy(off, p):
        return pltpu.make_async_copy(zero_s.at[pl.ds(0, p)], xs_ref.at[pl.ds(off, p)], zsem)

    def for_each_pad_piece(e, fn):
        cnt = cnt_ref[e]
        start = pst_ref[e] + cnt
        end = start + (EXP_BM - cnt % EXP_BM) % EXP_BM
        aligned = jnp.minimum((start + SUBLANE - 1) // SUBLANE * SUBLANE, end)
        for j in range(SUBLANE - 1):
            @pl.when(start + j < aligned)
            def _(j=j):
                fn(zero_copy(start + j, 1))

        npad = end - aligned
        off = aligned
        for p in _PAD_BITS:
            hit = (npad & p) != 0

            @pl.when(hit)
            def _(off=off, p=p):
                fn(zero_copy(pl.multiple_of(off, SUBLANE), p))

            off = off + jnp.where(hit, p, 0)

    @pl.when(i == 0)
    def _():
        zero_s[...] = jnp.zeros_like(zero_s)

        def start_e(e, c):
            for_each_pad_piece(e, lambda cp: cp.start())
            return c

        def wait_e(e, c):
            for_each_pad_piece(e, lambda cp: cp.wait())
            return c

        lax.fori_loop(0, N_EXPERTS, start_e, 0)
        lax.fori_loop(0, N_EXPERTS, wait_e, 0)

        zrows = zero_s.shape[0]
        last_cnt = cnt_ref[N_EXPERTS - 1]
        used = pst_ref[N_EXPERTS - 1] + (last_cnt + EXP_BM - 1) // EXP_BM * EXP_BM
        first_piece = used // zrows
        n_pieces = xs_ref.shape[0] // zrows

        def tail_copy(t):
            return zero_copy(pl.multiple_of(t * zrows, zrows), zrows)

        lax.fori_loop(first_piece, n_pieces, lambda t, c: (tail_copy(t).start(), c)[1], 0)
        lax.fori_loop(first_piece, n_pieces, lambda t, c: (tail_copy(t).wait(), c)[1], 0)

    def issue(g, c):
        for u in range(SUBLANE):
            for kk in range(2):
                d = dest_ref[kk * td + g * SUBLANE + u]
                pltpu.make_async_copy(h2_ref.at[g, pl.ds(u, 1)], xs_ref.at[pl.ds(d, 1)], sem).start()
        return c

    lax.fori_loop(0, td // SUBLANE, issue, 0)
    for kk in range(2):
        pltpu.make_async_copy(xs_ref.at[pl.ds(0, td)], xs_ref.at[pl.ds(td, td)], sem).wait()


def _dispatch_call(h2, dest3, pstarts, counts, n_rows):
    T, D = h2.shape
    nt, _, td2 = dest3.shape
    td = td2 // 2
    h2 = h2.reshape(T // SUBLANE, SUBLANE, D)
    grid_spec = pltpu.PrefetchScalarGridSpec(
        num_scalar_prefetch=2,
        grid=(nt,),
        in_specs=[
            pl.BlockSpec((None, None, 2 * td), lambda i, *_: (i, 0, 0), memory_space=pltpu.SMEM),
            pl.BlockSpec((td // SUBLANE, SUBLANE, D), lambda i, *_: (i, 0, 0)),
        ],
        out_specs=pl.BlockSpec(memory_space=pl.ANY),
        scratch_shapes=[
            pltpu.VMEM((_PAD_BITS[0], D), h2.dtype),
            pltpu.SemaphoreType.DMA(()),
            pltpu.SemaphoreType.DMA(()),
        ],
    )
    return pl.pallas_call(
        _dispatch_kernel,
        grid_spec=grid_spec,
        out_shape=jax.ShapeDtypeStruct((n_rows, D), h2.dtype),
        compiler_params=pltpu.CompilerParams(
            dimension_semantics=("arbitrary",), vmem_limit_bytes=VMEM_LIMIT, has_side_effects=True),
        name="dispatch",
    )(pstarts, counts, dest3, h2)


def _expert_kernel(be_ref, nu_ref, x_ref, wg_ref, wu_ref, wd_ref, y_ref):
    lo, hi = _unpack_rows(x_ref[...])
    xb = jnp.concatenate([lo.astype(BF16), hi.astype(BF16)], axis=1)
    g = _dot(xb, wg_ref[...])
    u = _dot(xb, wu_ref[...])
    hmid = (g * jax.nn.sigmoid(g) * u).astype(BF16)
    y_ref[...] = _pack_rows(_dot(hmid, wd_ref[...]))


def _expert_call(xs, block_e, n_used, wg, wu, wd):
    R, W = xs.shape
    D = 2 * W
    nb = R // EXP_BM
    last = lambda b, nu: jnp.minimum(b, nu[0] - 1)
    grid_spec = pltpu.PrefetchScalarGridSpec(
        num_scalar_prefetch=2,
        grid=(nb,),
        in_specs=[
            pl.BlockSpec((EXP_BM, W), lambda b, be, nu: (last(b, nu), 0)),
            pl.BlockSpec((None, D, D_EXPERT), lambda b, be, nu: (be[last(b, nu)], 0, 0)),
            pl.BlockSpec((None, D, D_EXPERT), lambda b, be, nu: (be[last(b, nu)], 0, 0)),
            pl.BlockSpec((None, D_EXPERT, D), lambda b, be, nu: (be[last(b, nu)], 0, 0)),
        ],
        out_specs=pl.BlockSpec((EXP_BM, W), lambda b, be, nu: (b, 0)),
    )
    return pl.pallas_call(
        _expert_kernel,
        grid_spec=grid_spec,
        out_shape=jax.ShapeDtypeStruct((R, W), xs.dtype),
        compiler_params=pltpu.CompilerParams(
            dimension_semantics=("arbitrary",), vmem_limit_bytes=VMEM_LIMIT),
        name="experts",
    )(block_e, n_used, xs, wg, wu, wd)


def _combine_kernel(dcur_ref, dnext_ref, x1_ref, gate_ref, gf_ref, y_ref, out_ref, ybuf, sem):
    tf = x1_ref.shape[0]
    i = pl.program_id(0)
    n = pl.num_programs(0)

    def issue_all(dref, slot):
        def group(g, c):
            for u in range(SUBLANE):
                for kk in range(2):
                    d = dref[kk * tf + g * SUBLANE + u]
                    pltpu.make_async_copy(y_ref.at[pl.ds(d, 1)], ybuf.at[slot, kk, g, pl.ds(u, 1)],
                                          sem.at[slot]).start()
            return c

        lax.fori_loop(0, tf // SUBLANE, group, 0)

    @pl.when(i == 0)
    def _():
        issue_all(dcur_ref, 0)

    for par in range(2):
        @pl.when((i % 2 == par) & (i + 1 < n))
        def _(par=par):
            issue_all(dnext_ref, 1 - par)

    slot = i % 2
    for kk in range(2):
        pltpu.make_async_copy(y_ref.at[pl.ds(0, tf)], y_ref.at[pl.ds(tf, tf)], sem.at[slot]).wait()

    g = jnp.concatenate([gate_ref[...], jnp.zeros((LANE - 8, tf), F32)], axis=0)
    gt = g.T
    w = ybuf.shape[-1]
    y0_lo, y0_hi = _unpack_rows(ybuf[slot, 0].reshape(tf, w))
    y1_lo, y1_hi = _unpack_rows(ybuf[slot, 1].reshape(tf, w))
    g0 = gt[:, 0:1]
    g1 = gt[:, 1:2]
    xo_lo = x1_ref[:, :w] + (g0 * y0_lo + g1 * y1_lo)
    xo_hi = x1_ref[:, w:] + (g0 * y0_hi + g1 * y1_hi)
    ms = (jnp.sum(xo_lo * xo_lo, axis=-1, keepdims=True)
          + jnp.sum(xo_hi * xo_hi, axis=-1, keepdims=True)) * (1.0 / (2 * w))
    scale = lax.rsqrt(ms + NORM_EPS)
    out_ref[:, :w] = xo_lo * scale * gf_ref[:, :w]
    out_ref[:, w:] = xo_hi * scale * gf_ref[:, w:]


def _combine_call(x1, dest3, gates, gf, ys):
    T, D = x1.shape
    nt, _, tf2 = dest3.shape
    tf = tf2 // 2
    return pl.pallas_call(
        _combine_kernel,
        grid=(nt,),
        in_specs=[
            pl.BlockSpec((None, None, 2 * tf), lambda i: (i, 0, 0), memory_space=pltpu.SMEM),
            pl.BlockSpec((None, None, 2 * tf), lambda i: (jnp.minimum(i + 1, nt - 1), 0, 0),
                         memory_space=pltpu.SMEM),
            pl.BlockSpec((tf, D), lambda i: (i, 0)),
            pl.BlockSpec((8, tf), lambda i: (0, i)),
            pl.BlockSpec((1, D), lambda i: (0, 0)),
            pl.BlockSpec(memory_space=pl.ANY),
        ],
        out_specs=pl.BlockSpec((tf, D), lambda i: (i, 0)),
        out_shape=jax.ShapeDtypeStruct((T, D), F32),
        scratch_shapes=[pltpu.VMEM((2, 2, tf // SUBLANE, SUBLANE, ys.shape[1]), ys.dtype),
                        pltpu.SemaphoreType.DMA((2,))],
        compiler_params=pltpu.CompilerParams(
            dimension_semantics=("arbitrary",), vmem_limit_bytes=VMEM_LIMIT),
        name="combine",
    )(dest3, dest3, x1, gates, gf, ys)


def _pick_tile(n, want):
    t = min(want, n)
    while n % t:
        t //= 2
    return t


def _mixer_weights(norm1_g, w_in, w_alpha_up, b_alpha, gla_norm_g, w_gla_branch, pool_w, pool_scale,
                   w_pool_branch, w_out, norm2_g, w_rg, b_rg, w_re, b_re):
    c0 = 2 * KEY_W + 2 * VAL_W
    c1 = c0 + GATE_RANK
    c2 = c1 + POOL_W
    w_qkvr = w_in[:, :c0].astype(BF16)
    w_a = jnp.pad(w_in[:, c0:c1], ((0, 0), (0, LANE - GATE_RANK))).astype(BF16)
    w_u = w_in[:, c1:c2].astype(BF16)
    w_g = w_in[:, c2:].astype(BF16)
    w_alpha = jnp.pad(w_alpha_up, ((0, LANE - GATE_RANK), (0, 0))).astype(BF16)
    w_re_t = jnp.transpose(w_re, (0, 2, 1)).reshape(N_EXPERTS, D_MODEL)
    wrt = jnp.zeros((ROUTER_ROWS, D_MODEL), F32)
    wrt = wrt.at[0:N_GROUPS].set(w_rg.T).at[8:8 + N_EXPERTS].set(w_re_t).astype(BF16)
    brt = jnp.zeros((ROUTER_ROWS,), F32).at[0:N_GROUPS].set(b_rg).at[8:8 + N_EXPERTS].set(b_re.reshape(-1))
    brt = jnp.broadcast_to(brt[:, None], (ROUTER_ROWS, LANE))
    row = lambda a: a.reshape(1, -1).astype(F32)
    return (row(norm1_g), w_qkvr, w_a, w_u, w_g, w_alpha, row(b_alpha), row(gla_norm_g),
            w_gla_branch.astype(BF16), pool_w.astype(BF16), row(pool_scale), w_pool_branch.astype(BF16),
            w_out.astype(BF16), row(norm2_g), wrt, brt)


def kernel(x, norm1_g, w_in, w_alpha_up, b_alpha, gla_norm_g, w_gla_branch, pool_w, pool_scale,
           w_pool_branch, w_out, norm2_g, w_router_group, b_router_group, w_router_expert,
           b_router_expert, w_exp_gate, w_exp_up, w_exp_down, norm_f_g):
    B, S, D = x.shape
    T = B * S
    depth = w_in.shape[0]
    ts = _pick_tile(S, MIX_TS)
    tr = _pick_tile(T, ROW_TILE)
    n_assign = 2 * T
    n_blocks = -(-(n_assign + N_EXPERTS * (EXP_BM - 1)) // EXP_BM)
    n_rows = n_blocks * EXP_BM

    assert depth == 1, "kernel supports the problem's DEPTH=1"
    for l in range(depth):
        wts = _mixer_weights(norm1_g[l], w_in[l], w_alpha_up[l], b_alpha[l], gla_norm_g[l], w_gla_branch[l],
                             pool_w[l], pool_scale[l], w_pool_branch[l], w_out[l], norm2_g[l],
                             w_router_group[l], b_router_group[l], w_router_expert[l], b_router_expert[l])
        x1, h2, eid, gates, rank, cnt = _mixer_call(x, wts, ts)

        counts = cnt[:, 0].astype(jnp.int32)
        padded = ((counts + EXP_BM - 1) // EXP_BM) * EXP_BM
        pends = jnp.cumsum(padded)
        pstarts = pends - padded
        expert_col = jnp.arange(N_EXPERTS, dtype=jnp.int32)[:, None, None]
        dest = jnp.sum(jnp.where(eid[None] == expert_col, pstarts[:, None, None], 0), axis=0) + rank
        dest3 = dest.reshape(2, T // tr, tr).transpose(1, 0, 2).reshape(T // tr, 1, 2 * tr)
        blk_start = jnp.arange(n_blocks, dtype=jnp.int32) * EXP_BM
        block_e = jnp.minimum(jnp.sum(pends[None, :] <= blk_start[:, None], axis=1), N_EXPERTS - 1).astype(jnp.int32)
        n_used = (pends[-1:] // EXP_BM).astype(jnp.int32)

        xs = _dispatch_call(h2.reshape(T, D // 2), dest3, pstarts.astype(jnp.int32), counts, n_rows)
        ys = _expert_call(xs, block_e, n_used, w_exp_gate[l].astype(BF16), w_exp_up[l].astype(BF16),
                          w_exp_down[l].astype(BF16))
        out = _combine_call(x1.reshape(T, D), dest3, gates, norm_f_g.reshape(1, D).astype(F32), ys)
        x = out.reshape(B, S, D)
    return x
```

```python
import functools

import jax
import jax.numpy as jnp
from jax import lax
from jax.experimental import pallas as pl
from jax.experimental.pallas import tpu as pltpu

F32 = jnp.float32
BF16 = jnp.bfloat16

D_MODEL = 1024
GLA_HEADS = 4
GLA_DK = 64
GLA_DV = 128
KEY_W = GLA_HEADS * GLA_DK
VAL_W = GLA_HEADS * GLA_DV
GATE_RANK = 16
GATE_NORMALIZER = 16.0
GLA_CHUNK = 64
POOL_WINDOWS = (2, 4, 8, 16)
POOL_W = 512
POOL_GW = 128
N_GROUPS = 4
EXPERTS_PER_GROUP = 8
N_EXPERTS = 32
D_EXPERT = 256
NORM_EPS = 1e-6

LANE = 128
SUBLANE = 8
POOL_HALO = 16
ROUTER_ROWS = 48

MIX_TS = 512
ROW_TILE = 512
EXP_BM = 512
VMEM_LIMIT = 56 * 1024 * 1024


def _dot(a, b):
    return jnp.dot(a, b, preferred_element_type=F32)


def _dot_nt(a, b):
    return lax.dot_general(a, b, (((1,), (1,)), ((), ())), preferred_element_type=F32)


def _dot_tn(a, b):
    return lax.dot_general(a, b, (((0,), (0,)), ((), ())), preferred_element_type=F32)


def _bf16_bits(x):
    u = lax.bitcast_convert_type(x, jnp.int32)
    return (u + 0x7FFF + (lax.shift_right_logical(u, 16) & 1)) & -65536


def _pack_rows(x):
    w = x.shape[1] // 2
    return lax.shift_right_logical(_bf16_bits(x[:, :w]), 16) | _bf16_bits(x[:, w:])


def _unpack_rows(words):
    lo = lax.bitcast_convert_type(lax.shift_left(words, 16), F32)
    hi = lax.bitcast_convert_type(words & -65536, F32)
    return lo, hi


def _chunk_cumsum(x, chunk):
    n, w = x.shape
    pos = lax.broadcasted_iota(jnp.int32, (n, w), 0) % chunk
    step = 1
    while step < chunk:
        if step < 8:
            shifted = pltpu.roll(x, step, axis=0)
        else:
            shifted = jnp.concatenate([jnp.zeros((step, w), x.dtype), x[:n - step]], axis=0)
        x = x + jnp.where(pos >= step, shifted, 0.0)
        step *= 2
    return x


def _mixer_kernel(x_ref, g1_ref, wqkvr_ref, wa_ref, wu_ref, wg_ref, walpha_ref, balpha_ref,
                  glag_ref, wglab_ref, poolw_ref, pscale_ref, wpoolb_ref, wout_ref, g2_ref,
                  wrt_ref, brt_ref,
                  x1_ref, h2_ref, eid_ref, gate_ref, rank_ref, cnt_ref,
                  state_s, carry_s, cnt_s, qdm_s, kem_s, kd_s, v_s, dec_s, o_s):
    ts = x_ref.shape[0]
    b_idx = pl.program_id(0)
    s_idx = pl.program_id(1)

    @pl.when(s_idx == 0)
    def _():
        state_s[...] = jnp.zeros_like(state_s)
        carry_s[...] = jnp.zeros_like(carry_s)

    @pl.when((b_idx == 0) & (s_idx == 0))
    def _():
        cnt_s[...] = jnp.zeros_like(cnt_s)

    xf = x_ref[...]
    ms = jnp.mean(xf * xf, axis=-1, keepdims=True)
    h = (xf * lax.rsqrt(ms + NORM_EPS) * g1_ref[...]).astype(BF16)

    qkvr = _dot(h, wqkvr_ref[...])
    q = qkvr[:, 0:KEY_W]
    k = qkvr[:, KEY_W:2 * KEY_W]
    v = qkvr[:, 2 * KEY_W:2 * KEY_W + VAL_W]
    r = qkvr[:, 2 * KEY_W + VAL_W:]

    a_low = _dot(h, wa_ref[...])
    z = _dot(a_low.astype(BF16), walpha_ref[...]) + balpha_ref[...]
    log_a = (jnp.minimum(z, 0.0) - jnp.log1p(jnp.exp(-jnp.abs(z)))) * (1.0 / GATE_NORMALIZER)

    nc = ts // GLA_CHUNK
    b = _chunk_cumsum(log_a, GLA_CHUNK)
    b3 = b.reshape(nc, GLA_CHUNK, KEY_W)
    b_last = b3[:, GLA_CHUNK - 1:GLA_CHUNK, :]
    lane = lax.broadcasted_iota(jnp.int32, (ts, LANE), 1)
    qd = q * jnp.exp(b) * (GLA_DK ** -0.5)
    ke = (k.reshape(nc, GLA_CHUNK, KEY_W) * jnp.exp(b_last - b3)).reshape(ts, KEY_W)
    for hd in range(GLA_HEADS):
        pair = slice((hd // 2) * LANE, (hd // 2 + 1) * LANE)
        mine = (lane < GLA_DK) if hd % 2 == 0 else (lane >= GLA_DK)
        qdm_s[hd] = jnp.where(mine, qd[:, pair], 0.0).astype(BF16)
        kem_s[hd] = jnp.where(mine, ke[:, pair], 0.0).astype(BF16)
    kd_s[...] = (k * jnp.exp(-b)).astype(BF16)
    dec_s[...] = jnp.broadcast_to(jnp.exp(b_last), (nc, SUBLANE, KEY_W))
    v_s[...] = v.astype(BF16)

    tri = (lax.broadcasted_iota(jnp.int32, (GLA_CHUNK, GLA_CHUNK), 0)
           >= lax.broadcasted_iota(jnp.int32, (GLA_CHUNK, GLA_CHUNK), 1))

    def chunk_body(c, carry):
        rows = pl.ds(pl.multiple_of(c * GLA_CHUNK, GLA_CHUNK), GLA_CHUNK)
        for hd in range(GLA_HEADS):
            pair = slice((hd // 2) * LANE, (hd // 2 + 1) * LANE)
            cols = slice(hd * GLA_DV, (hd + 1) * GLA_DV)
            qm = qdm_s[hd, rows, :]
            vh = v_s[rows, cols]
            st = state_s[hd]
            scores = jnp.where(tri, _dot_nt(qm, kd_s[rows, pair]), 0.0)
            o_s[rows, cols] = _dot(scores.astype(BF16), vh) + _dot_nt(qm, st.astype(BF16))
            state_s[hd] = dec_s[c, 0:1, pair] * st + _dot_tn(vh, kem_s[hd, rows, :])
        return carry

    lax.fori_loop(0, nc, chunk_body, 0)

    o_all = o_s[...]
    parts = []
    for hd in range(GLA_HEADS):
        oh = o_all[:, hd * GLA_DV:(hd + 1) * GLA_DV]
        parts.append(oh * lax.rsqrt(jnp.mean(oh * oh, axis=-1, keepdims=True) + NORM_EPS))
    o_n = jnp.concatenate(parts, axis=-1) * glag_ref[...]
    o_g = (o_n * (r * jax.nn.sigmoid(r))).astype(BF16)
    y_gla = _dot(o_g, wglab_ref[...])

    u = _dot(h, wu_ref[...])
    ext = jnp.concatenate([carry_s[...], u], axis=0)
    carry_s[...] = u[ts - POOL_HALO:, :]
    pos = (s_idx * ts + lax.broadcasted_iota(jnp.int32, (ts, 1), 0)).astype(F32)
    mixed = []
    for gi, w in enumerate(POOL_WINDOWS):
        a = ext[:, gi * POOL_GW:(gi + 1) * POOL_GW]
        step = 1
        while step < w:
            a = a + pltpu.roll(a, step, axis=0)
            step *= 2
        pooled = a[POOL_HALO:, :] / jnp.minimum(pos + 1.0, float(w))
        diff = pooled - u[:, gi * POOL_GW:(gi + 1) * POOL_GW]
        mixed.append(_dot(diff.astype(BF16), poolw_ref[gi]))
    pm = (jnp.concatenate(mixed, axis=-1) * pscale_ref[...]).astype(BF16)
    y_pool = _dot(pm, wpoolb_ref[...])

    gates = _dot(h, wg_ref[...])
    merged = (jax.nn.sigmoid(gates[:, :D_MODEL]) * y_gla
              + jax.nn.sigmoid(gates[:, D_MODEL:]) * y_pool).astype(BF16)
    x1 = xf + _dot(merged, wout_ref[...])
    x1_ref[...] = x1

    ms2 = jnp.mean(x1 * x1, axis=-1, keepdims=True)
    h2 = x1 * lax.rsqrt(ms2 + NORM_EPS) * g2_ref[...]
    h2_ref[...] = _pack_rows(h2)
    lt = _dot_nt(wrt_ref[...], h2.astype(BF16)) + brt_ref[:, 0:1]

    l0, l1, l2, l3 = lt[0:1], lt[1:2], lt[2:3], lt[3:4]
    gm = jnp.maximum(jnp.maximum(l0, l1), jnp.maximum(l2, l3))
    gidx = jnp.where(l0 == gm, 0, jnp.where(l1 == gm, 1, jnp.where(l2 == gm, 2, 3)))
    gp = 1.0 / (jnp.exp(l0 - gm) + jnp.exp(l1 - gm) + jnp.exp(l2 - gm) + jnp.exp(l3 - gm))
    el = jnp.where(gidx == 0, lt[8:16],
                   jnp.where(gidx == 1, lt[16:24], jnp.where(gidx == 2, lt[24:32], lt[32:40])))
    row8 = lax.broadcasted_iota(jnp.int32, (EXPERTS_PER_GROUP, ts), 0)
    m1 = jnp.max(el, axis=0, keepdims=True)
    i1 = jnp.min(jnp.where(el == m1, row8, EXPERTS_PER_GROUP), axis=0, keepdims=True)
    el2 = jnp.where(row8 == i1, -jnp.inf, el)
    m2 = jnp.max(el2, axis=0, keepdims=True)
    i2 = jnp.min(jnp.where(el2 == m2, row8, EXPERTS_PER_GROUP), axis=0, keepdims=True)
    dd = jnp.exp(m2 - m1)
    p1 = 1.0 / (1.0 + dd)
    p2 = dd / (1.0 + dd)
    e1 = gidx * EXPERTS_PER_GROUP + i1
    e2 = gidx * EXPERTS_PER_GROUP + i2
    eid_ref[0:1, :] = e1
    eid_ref[1:2, :] = e2
    gate_ref[...] = jnp.concatenate([gp * p1, gp * p2, jnp.zeros((6, ts), F32)], axis=0)

    row32 = lax.broadcasted_iota(jnp.int32, (N_EXPERTS, ts), 0)
    oh1 = row32 == e1
    oh2 = row32 == e2
    member = jnp.where(oh1, 1.0, 0.0) + jnp.where(oh2, 1.0, 0.0)
    before = (lax.broadcasted_iota(jnp.int32, (ts, ts), 0)
              < lax.broadcasted_iota(jnp.int32, (ts, ts), 1))
    prefix = _dot(member.astype(BF16), jnp.where(before, 1.0, 0.0).astype(BF16))
    base = cnt_s[...]
    prefix = prefix + jnp.concatenate([base] * (ts // LANE), axis=1)
    rank_ref[0:1, :] = jnp.sum(jnp.where(oh1, prefix, 0.0), axis=0, keepdims=True).astype(jnp.int32)
    rank_ref[1:2, :] = jnp.sum(jnp.where(oh2, prefix, 0.0), axis=0, keepdims=True).astype(jnp.int32)
    new_cnt = base + jnp.sum(member, axis=1, keepdims=True)
    cnt_s[...] = new_cnt
    cnt_ref[...] = new_cnt


def _const_spec(shape):
    nd = len(shape)
    return pl.BlockSpec(shape, lambda b, s, _nd=nd: (0,) * _nd, pipeline_mode=pl.Buffered(1))


def _mixer_call(x, wts, ts):
    B, S, D = x.shape
    T = B * S
    ns = S // ts
    tok_spec = pl.BlockSpec((None, ts, D), lambda b, s: (b, s, 0))
    flat_tok = lambda rows: pl.BlockSpec((rows, ts), lambda b, s: (0, b * ns + s))
    in_specs = [tok_spec] + [_const_spec(w.shape) for w in wts]
    out_shape = (
        jax.ShapeDtypeStruct((B, S, D), F32),
        jax.ShapeDtypeStruct((B, S, D // 2), jnp.int32),
        jax.ShapeDtypeStruct((2, T), jnp.int32),
        jax.ShapeDtypeStruct((8, T), F32),
        jax.ShapeDtypeStruct((2, T), jnp.int32),
        jax.ShapeDtypeStruct((N_EXPERTS, LANE), F32),
    )
    packed_spec = pl.BlockSpec((None, ts, D // 2), lambda b, s: (b, s, 0))
    out_specs = (tok_spec, packed_spec, flat_tok(2), flat_tok(8), flat_tok(2),
                 pl.BlockSpec((N_EXPERTS, LANE), lambda b, s: (0, 0)))
    scratch = [
        pltpu.VMEM((GLA_HEADS, GLA_DV, LANE), F32),
        pltpu.VMEM((POOL_HALO, POOL_W), F32),
        pltpu.VMEM((N_EXPERTS, LANE), F32),
        pltpu.VMEM((GLA_HEADS, ts, LANE), BF16),
        pltpu.VMEM((GLA_HEADS, ts, LANE), BF16),
        pltpu.VMEM((ts, KEY_W), BF16),
        pltpu.VMEM((ts, VAL_W), BF16),
        pltpu.VMEM((ts // GLA_CHUNK, SUBLANE, KEY_W), F32),
        pltpu.VMEM((ts, VAL_W), F32),
    ]
    return pl.pallas_call(
        _mixer_kernel,
        grid=(B, ns),
        in_specs=in_specs,
        out_specs=out_specs,
        out_shape=out_shape,
        scratch_shapes=scratch,
        compiler_params=pltpu.CompilerParams(
            dimension_semantics=("arbitrary", "arbitrary"), vmem_limit_bytes=VMEM_LIMIT),
        name="mixer",
    )(x, *wts)


_PAD_BITS = tuple(p for p in (1 << i for i in reversed(range((EXP_BM - 1).bit_length()))) if p >= SUBLANE)


def _dispatch_kernel(pst_ref, cnt_ref, dest_ref, h2_ref, xs_ref, zero_s, sem, zsem):
    td = h2_ref.shape[0] * SUBLANE
    i = pl.program_id(0)

    def zero_copy(off, p):
        return pltpu.make_async_copy(zero_s.at[pl.ds(0, p)], xs_ref.at[pl.ds(off, p)], zsem)

    def for_each_pad_piece(e, fn):
        cnt = cnt_ref[e]
        start = pst_ref[e] + cnt
        end = start + (EXP_BM - cnt % EXP_BM) % EXP_BM
        aligned = jnp.minimum((start + SUBLANE - 1) // SUBLANE * SUBLANE, end)
        for j in range(SUBLANE - 1):
            @pl.when(start + j < aligned)
            def _(j=j):
                fn(zero_copy(start + j, 1))

        npad = end - aligned
        off = aligned
        for p in _PAD_BITS:
            hit = (npad & p) != 0

            @pl.when(hit)
            def _(off=off, p=p):
                fn(zero_copy(pl.multiple_of(off, SUBLANE), p))

            off = off + jnp.where(hit, p, 0)

    @pl.when(i == 0)
    def _():
        zero_s[...] = jnp.zeros_like(zero_s)

        def start_e(e, c):
            for_each_pad_piece(e, lambda cp: cp.start())
            return c

        def wait_e(e, c):
            for_each_pad_piece(e, lambda cp: cp.wait())
            return c

        lax.fori_loop(0, N_EXPERTS, start_e, 0)
        lax.fori_loop(0, N_EXPERTS, wait_e, 0)

        zrows = zero_s.shape[0]
        last_cnt = cnt_ref[N_EXPERTS - 1]
        used = pst_ref[N_EXPERTS - 1] + (last_cnt + EXP_BM - 1) // EXP_BM * EXP_BM
        first_piece = used // zrows
        n_pieces = xs_ref.shape[0] // zrows

        def tail_copy(t):
            return zero_copy(pl.multiple_of(t * zrows, zrows), zrows)

        lax.fori_loop(first_piece, n_pieces, lambda t, c: (tail_copy(t).start(), c)[1], 0)
        lax.fori_loop(first_piece, n_pieces, lambda t, c: (tail_copy(t).wait(), c)[1], 0)

    def issue(g, c):
        for u in range(SUBLANE):
            for kk in range(2):
                d = dest_ref[kk * td + g * SUBLANE + u]
                pltpu.make_async_copy(h2_ref.at[g, pl.ds(u, 1)], xs_ref.at[pl.ds(d, 1)], sem).start(priority=kk)
        return c

    lax.fori_loop(0, td // SUBLANE, issue, 0)
    for kk in range(2):
        pltpu.make_async_copy(xs_ref.at[pl.ds(0, td)], xs_ref.at[pl.ds(td, td)], sem).wait()


def _dispatch_call(h2, dest3, pstarts, counts, n_rows):
    T, D = h2.shape
    nt, _, td2 = dest3.shape
    td = td2 // 2
    h2 = h2.reshape(T // SUBLANE, SUBLANE, D)
    grid_spec = pltpu.PrefetchScalarGridSpec(
        num_scalar_prefetch=2,
        grid=(nt,),
        in_specs=[
            pl.BlockSpec((None, None, 2 * td), lambda i, *_: (i, 0, 0), memory_space=pltpu.SMEM),
            pl.BlockSpec((td // SUBLANE, SUBLANE, D), lambda i, *_: (i, 0, 0)),
        ],
        out_specs=pl.BlockSpec(memory_space=pl.ANY),
        scratch_shapes=[
            pltpu.VMEM((_PAD_BITS[0], D), h2.dtype),
            pltpu.SemaphoreType.DMA(()),
            pltpu.SemaphoreType.DMA(()),
        ],
    )
    return pl.pallas_call(
        _dispatch_kernel,
        grid_spec=grid_spec,
        out_shape=jax.ShapeDtypeStruct((n_rows, D), h2.dtype),
        compiler_params=pltpu.CompilerParams(
            dimension_semantics=("arbitrary",), vmem_limit_bytes=VMEM_LIMIT, has_side_effects=True),
        name="dispatch",
    )(pstarts, counts, dest3, h2)


def _expert_kernel(be_ref, nu_ref, x_ref, wg_ref, wu_ref, wd_ref, y_ref):
    b = pl.program_id(0)

    @pl.when(b < nu_ref[0])
    def _():
        lo, hi = _unpack_rows(x_ref[...])
        xb = jnp.concatenate([lo.astype(BF16), hi.astype(BF16)], axis=1)
        g = _dot(xb, wg_ref[...])
        u = _dot(xb, wu_ref[...])
        hmid = (g * jax.nn.sigmoid(g) * u).astype(BF16)
        y_ref[...] = _pack_rows(_dot(hmid, wd_ref[...]))

    @pl.when(b >= nu_ref[0])
    def _():
        y_ref[...] = jnp.zeros_like(y_ref)


def _expert_call(xs, block_e, n_used, wg, wu, wd):
    R, W = xs.shape
    D = 2 * W
    nb = R // EXP_BM
    last = lambda b, nu: jnp.minimum(b, nu[0] - 1)
    grid_spec = pltpu.PrefetchScalarGridSpec(
        num_scalar_prefetch=2,
        grid=(nb,),
        in_specs=[
            pl.BlockSpec((EXP_BM, W), lambda b, be, nu: (last(b, nu), 0)),
            pl.BlockSpec((None, D, D_EXPERT), lambda b, be, nu: (be[last(b, nu)], 0, 0)),
            pl.BlockSpec((None, D, D_EXPERT), lambda b, be, nu: (be[last(b, nu)], 0, 0)),
            pl.BlockSpec((None, D_EXPERT, D), lambda b, be, nu: (be[last(b, nu)], 0, 0)),
        ],
        out_specs=pl.BlockSpec((EXP_BM, W), lambda b, be, nu: (b, 0)),
    )
    return pl.pallas_call(
        _expert_kernel,
        grid_spec=grid_spec,
        out_shape=jax.ShapeDtypeStruct((R, W), xs.dtype),
        compiler_params=pltpu.CompilerParams(
            dimension_semantics=("arbitrary",), vmem_limit_bytes=VMEM_LIMIT),
        name="experts",
    )(block_e, n_used, xs, wg, wu, wd)


def _combine_kernel(dcur_ref, dnext_ref, x1_ref, gate_ref, gf_ref, y_ref, out_ref, ybuf, sem):
    tf = x1_ref.shape[0]
    i = pl.program_id(0)
    n = pl.num_programs(0)

    def issue_all(dref, slot):
        def group(g, c):
            for u in range(SUBLANE):
                for kk in range(2):
                    d = dref[kk * tf + g * SUBLANE + u]
                    pltpu.make_async_copy(y_ref.at[pl.ds(d, 1)], ybuf.at[slot, kk, g, pl.ds(u, 1)],
                                          sem.at[slot]).start(priority=kk)
            return c

        lax.fori_loop(0, tf // SUBLANE, group, 0)

    @pl.when(i == 0)
    def _():
        issue_all(dcur_ref, 0)

    for par in range(2):
        @pl.when((i % 2 == par) & (i + 1 < n))
        def _(par=par):
            issue_all(dnext_ref, 1 - par)

    slot = i % 2
    for kk in range(2):
        pltpu.make_async_copy(y_ref.at[pl.ds(0, tf)], y_ref.at[pl.ds(tf, tf)], sem.at[slot]).wait()

    g = jnp.concatenate([gate_ref[...], jnp.zeros((LANE - 8, tf), F32)], axis=0)
    gt = g.T
    w = ybuf.shape[-1]
    y0_lo, y0_hi = _unpack_rows(ybuf[slot, 0].reshape(tf, w))
    y1_lo, y1_hi = _unpack_rows(ybuf[slot, 1].reshape(tf, w))
    g0 = gt[:, 0:1]
    g1 = gt[:, 1:2]
    xo_lo = x1_ref[:, :w] + (g0 * y0_lo + g1 * y1_lo)
    xo_hi = x1_ref[:, w:] + (g0 * y0_hi + g1 * y1_hi)
    ms = (jnp.sum(xo_lo * xo_lo, axis=-1, keepdims=True)
          + jnp.sum(xo_hi * xo_hi, axis=-1, keepdims=True)) * (1.0 / (2 * w))
    scale = lax.rsqrt(ms + NORM_EPS)
    out_ref[:, :w] = xo_lo * scale * gf_ref[:, :w]
    out_ref[:, w:] = xo_hi * scale * gf_ref[:, w:]


def _combine_call(x1, dest3, gates, gf, ys):
    T, D = x1.shape
    nt, _, tf2 = dest3.shape
    tf = tf2 // 2
    return pl.pallas_call(
        _combine_kernel,
        grid=(nt,),
        in_specs=[
            pl.BlockSpec((None, None, 2 * tf), lambda i: (i, 0, 0), memory_space=pltpu.SMEM),
            pl.BlockSpec((None, None, 2 * tf), lambda i: (jnp.minimum(i + 1, nt - 1), 0, 0),
                         memory_space=pltpu.SMEM),
            pl.BlockSpec((tf, D), lambda i: (i, 0)),
            pl.BlockSpec((8, tf), lambda i: (0, i)),
            pl.BlockSpec((1, D), lambda i: (0, 0)),
            pl.BlockSpec(memory_space=pl.ANY),
        ],
        out_specs=pl.BlockSpec((tf, D), lambda i: (i, 0)),
        out_shape=jax.ShapeDtypeStruct((T, D), F32),
        scratch_shapes=[pltpu.VMEM((2, 2, tf // SUBLANE, SUBLANE, ys.shape[1]), ys.dtype),
                        pltpu.SemaphoreType.DMA((2,))],
        compiler_params=pltpu.CompilerParams(
            dimension_semantics=("arbitrary",), vmem_limit_bytes=VMEM_LIMIT),
        name="combine",
    )(dest3, dest3, x1, gates, gf, ys)


def _pick_tile(n, want):
    t = min(want, n)
    while n % t:
        t //= 2
    return t


def _mixer_weights(norm1_g, w_in, w_alpha_up, b_alpha, gla_norm_g, w_gla_branch, pool_w, pool_scale,
                   w_pool_branch, w_out, norm2_g, w_rg, b_rg, w_re, b_re):
    c0 = 2 * KEY_W + 2 * VAL_W
    c1 = c0 + GATE_RANK
    c2 = c1 + POOL_W
    w_qkvr = w_in[:, :c0].astype(BF16)
    w_a = jnp.pad(w_in[:, c0:c1], ((0, 0), (0, LANE - GATE_RANK))).astype(BF16)
    w_u = w_in[:, c1:c2].astype(BF16)
    w_g = w_in[:, c2:].astype(BF16)
    w_alpha = jnp.pad(w_alpha_up, ((0, LANE - GATE_RANK), (0, 0))).astype(BF16)
    w_re_t = jnp.transpose(w_re, (0, 2, 1)).reshape(N_EXPERTS, D_MODEL)
    wrt = jnp.zeros((ROUTER_ROWS, D_MODEL), F32)
    wrt = wrt.at[0:N_GROUPS].set(w_rg.T).at[8:8 + N_EXPERTS].set(w_re_t).astype(BF16)
    brt = jnp.zeros((ROUTER_ROWS,), F32).at[0:N_GROUPS].set(b_rg).at[8:8 + N_EXPERTS].set(b_re.reshape(-1))
    brt = jnp.broadcast_to(brt[:, None], (ROUTER_ROWS, LANE))
    row = lambda a: a.reshape(1, -1).astype(F32)
    return (row(norm1_g), w_qkvr, w_a, w_u, w_g, w_alpha, row(b_alpha), row(gla_norm_g),
            w_gla_branch.astype(BF16), pool_w.astype(BF16), row(pool_scale), w_pool_branch.astype(BF16),
            w_out.astype(BF16), row(norm2_g), wrt, brt)


def kernel(x, norm1_g, w_in, w_alpha_up, b_alpha, gla_norm_g, w_gla_branch, pool_w, pool_scale,
           w_pool_branch, w_out, norm2_g, w_router_group, b_router_group, w_router_expert,
           b_router_expert, w_exp_gate, w_exp_up, w_exp_down, norm_f_g):
    B, S, D = x.shape
    T = B * S
    depth = w_in.shape[0]
    ts = _pick_tile(S, MIX_TS)
    tr = _pick_tile(T, ROW_TILE)
    n_assign = 2 * T
    n_blocks = -(-(n_assign + N_EXPERTS * (EXP_BM - 1)) // EXP_BM)
    n_rows = n_blocks * EXP_BM

    assert depth == 1, "kernel supports the problem's DEPTH=1"
    for l in range(depth):
        wts = _mixer_weights(norm1_g[l], w_in[l], w_alpha_up[l], b_alpha[l], gla_norm_g[l], w_gla_branch[l],
                             pool_w[l], pool_scale[l], w_pool_branch[l], w_out[l], norm2_g[l],
                             w_router_group[l], b_router_group[l], w_router_expert[l], b_router_expert[l])
        x1, h2, eid, gates, rank, cnt = _mixer_call(x, wts, ts)

        counts = cnt[:, 0].astype(jnp.int32)
        padded = ((counts + EXP_BM - 1) // EXP_BM) * EXP_BM
        pends = jnp.cumsum(padded)
        pstarts = pends - padded
        expert_col = jnp.arange(N_EXPERTS, dtype=jnp.int32)[:, None, None]
        dest = jnp.sum(jnp.where(eid[None] == expert_col, pstarts[:, None, None], 0), axis=0) + rank
        dest3 = dest.reshape(2, T // tr, tr).transpose(1, 0, 2).reshape(T // tr, 1, 2 * tr)
        blk_start = jnp.arange(n_blocks, dtype=jnp.int32) * EXP_BM
        block_e = jnp.minimum(jnp.sum(pends[None, :] <= blk_start[:, None], axis=1), N_EXPERTS - 1).astype(jnp.int32)
        n_used = (pends[-1:] // EXP_BM).astype(jnp.int32)

        xs = _dispatch_call(h2.reshape(T, D // 2), dest3, pstarts.astype(jnp.int32), counts, n_rows)
        ys = _expert_call(xs, block_e, n_used, w_exp_gate[l].astype(BF16), w_exp_up[l].astype(BF16),
                          w_exp_down[l].astype(BF16))
        out = _combine_call(x1.reshape(T, D), dest3, gates, norm_f_g.reshape(1, D).astype(F32), ys)
        x = out.reshape(B, S, D)
    return x
```

```python
import jax
import jax.numpy as jnp
from jax import lax
from jax.experimental import pallas as pl
from jax.experimental.pallas import tpu as pltpu

F32 = jnp.float32
BF16 = jnp.bfloat16

D_MODEL = 1024
GLA_HEADS = 4
GLA_DK = 64
GLA_DV = 128
KEY_W = GLA_HEADS * GLA_DK
VAL_W = GLA_HEADS * GLA_DV
GATE_RANK = 16
GATE_NORMALIZER = 16.0
GLA_CHUNK = 64
POOL_WINDOWS = (2, 4, 8, 16)
POOL_W = 512
POOL_GW = 128
N_GROUPS = 4
EXPERTS_PER_GROUP = 8
N_EXPERTS = 32
D_EXPERT = 256
NORM_EPS = 1e-6

LANE = 128
SUBLANE = 8
POOL_HALO = 16
ROUTER_ROWS = 48

MIX_TS = 512
ROW_TILE = 512
GLA_GROUP = 2
EXP_BM = 512
VMEM_LIMIT = 56 * 1024 * 1024


def _dot(a, b):
    return jnp.dot(a, b, preferred_element_type=F32)


def _dot_nt(a, b):
    return lax.dot_general(a, b, (((1,), (1,)), ((), ())), preferred_element_type=F32)


def _dot_tn(a, b):
    return lax.dot_general(a, b, (((0,), (0,)), ((), ())), preferred_element_type=F32)


def _bf16_bits(x):
    u = lax.bitcast_convert_type(x, jnp.int32)
    return (u + 0x7FFF + (lax.shift_right_logical(u, 16) & 1)) & -65536


def _pack_rows(x):
    w = x.shape[1] // 2
    return lax.shift_right_logical(_bf16_bits(x[:, :w]), 16) | _bf16_bits(x[:, w:])


def _unpack_rows(words):
    lo = lax.bitcast_convert_type(lax.shift_left(words, 16), F32)
    hi = lax.bitcast_convert_type(words & -65536, F32)
    return lo, hi


def _chunk_cumsum(x, chunk):
    n, w = x.shape
    pos = lax.broadcasted_iota(jnp.int32, (n, w), 0) % chunk
    step = 1
    while step < chunk:
        if step < 8:
            shifted = pltpu.roll(x, step, axis=0)
        else:
            shifted = jnp.concatenate([jnp.zeros((step, w), x.dtype), x[:n - step]], axis=0)
        x = x + jnp.where(pos >= step, shifted, 0.0)
        step *= 2
    return x


def _mixer_kernel(x_ref, g1_ref, wqkvr_ref, wa_ref, wu_ref, wg_ref, walpha_ref, balpha_ref,
                  glag_ref, wglab_ref, poolw_ref, pscale_ref, wpoolb_ref, wout_ref, g2_ref,
                  wrt_ref, brt_ref,
                  x1_ref, h2_ref, eid_ref, gate_ref, rank_ref, cnt_ref,
                  state_s, carry_s, cnt_s, qdm_s, kem_s, kdt_s, v_s, dec_s, o_s):
    ts = x_ref.shape[0]
    b_idx = pl.program_id(0)
    s_idx = pl.program_id(1)

    @pl.when(s_idx == 0)
    def _():
        state_s[...] = jnp.zeros_like(state_s)
        carry_s[...] = jnp.zeros_like(carry_s)

    @pl.when((b_idx == 0) & (s_idx == 0))
    def _():
        cnt_s[...] = jnp.zeros_like(cnt_s)

    xf = x_ref[...]
    ms = jnp.mean(xf * xf, axis=-1, keepdims=True)
    h = (xf * lax.rsqrt(ms + NORM_EPS) * g1_ref[...]).astype(BF16)

    qkvr = _dot(h, wqkvr_ref[...])
    q = qkvr[:, 0:KEY_W]
    k = qkvr[:, KEY_W:2 * KEY_W]
    v = qkvr[:, 2 * KEY_W:2 * KEY_W + VAL_W]
    r = qkvr[:, 2 * KEY_W + VAL_W:]

    a_low = _dot(h, wa_ref[...])
    z = _dot(a_low.astype(BF16), walpha_ref[...]) + balpha_ref[...]
    log_a = (jnp.minimum(z, 0.0) - jnp.log1p(jnp.exp(-jnp.abs(z)))) * (1.0 / GATE_NORMALIZER)

    nc = ts // GLA_CHUNK
    b = _chunk_cumsum(log_a, GLA_CHUNK)
    b3 = b.reshape(nc, GLA_CHUNK, KEY_W)
    b_last = b3[:, GLA_CHUNK - 1:GLA_CHUNK, :]
    lane = lax.broadcasted_iota(jnp.int32, (ts, LANE), 1)
    qd = q * jnp.exp(b) * (GLA_DK ** -0.5)
    ke = (k.reshape(nc, GLA_CHUNK, KEY_W) * jnp.exp(b_last - b3)).reshape(ts, KEY_W)
    for hd in range(GLA_HEADS):
        pair = slice((hd // 2) * LANE, (hd // 2 + 1) * LANE)
        mine = (lane < GLA_DK) if hd % 2 == 0 else (lane >= GLA_DK)
        qdm_s[hd] = jnp.where(mine, qd[:, pair], 0.0).astype(BF16)
        kem_s[hd] = jnp.where(mine, ke[:, pair], 0.0).astype(BF16)
    v_s[...] = v.astype(BF16)
    grp = GLA_GROUP * GLA_CHUNK
    kd = k * jnp.exp(-b)
    for p in range(GLA_HEADS // 2):
        kd_t = kd[:, p * LANE:(p + 1) * LANE].T.astype(BF16)
        for j in range(ts // grp):
            kdt_s[p, j] = kd_t[:, j * grp:(j + 1) * grp]
    dec_rows = jnp.concatenate([jnp.exp(b_last[c]) for c in range(nc)]
                               + [jnp.zeros((LANE - nc, KEY_W), F32)], axis=0)
    dec_t = dec_rows.T
    for c in range(nc):
        for p in range(GLA_HEADS // 2):
            dec_s[c, p] = jnp.broadcast_to(dec_t[p * LANE:(p + 1) * LANE, c:c + 1], (LANE, GLA_DV))

    rg = lax.broadcasted_iota(jnp.int32, (grp, grp), 0)
    cg = lax.broadcasted_iota(jnp.int32, (grp, grp), 1)
    tri_g = (rg >= cg) & (rg // GLA_CHUNK == cg // GLA_CHUNK)

    def chunk_group_body(j, carry):
        rows_g = pl.ds(pl.multiple_of(j * grp, grp), grp)
        for hd in range(GLA_HEADS):
            p = hd // 2
            cols = slice(hd * GLA_DV, (hd + 1) * GLA_DV)
            qm = qdm_s[hd, rows_g, :]
            vh = v_s[rows_g, cols]
            km = kem_s[hd, rows_g, :]
            scores = jnp.where(tri_g, _dot(qm, kdt_s[p, j]), 0.0)
            o_intra = _dot(scores.astype(BF16), vh)
            st = state_s[hd]
            outs = []
            for cc in range(GLA_GROUP):
                c = GLA_GROUP * j + cc
                part = slice(cc * GLA_CHUNK, (cc + 1) * GLA_CHUNK)
                outs.append(o_intra[part] + _dot(qm[part], st.astype(BF16)))
                st = dec_s[c, p] * st + _dot_tn(km[part], vh[part])
            state_s[hd] = st
            o_s[rows_g, cols] = jnp.concatenate(outs, axis=0)
        return carry

    lax.fori_loop(0, nc // GLA_GROUP, chunk_group_body, 0)

    o_all = o_s[...]
    parts = []
    for hd in range(GLA_HEADS):
        oh = o_all[:, hd * GLA_DV:(hd + 1) * GLA_DV]
        parts.append(oh * lax.rsqrt(jnp.mean(oh * oh, axis=-1, keepdims=True) + NORM_EPS))
    o_n = jnp.concatenate(parts, axis=-1) * glag_ref[...]
    o_g = (o_n * (r * jax.nn.sigmoid(r))).astype(BF16)
    y_gla = _dot(o_g, wglab_ref[...])

    u = _dot(h, wu_ref[...])
    ext = jnp.concatenate([carry_s[...], u], axis=0)
    carry_s[...] = u[ts - POOL_HALO:, :]
    pos = (s_idx * ts + lax.broadcasted_iota(jnp.int32, (ts, 1), 0)).astype(F32)
    mixed = []
    for gi, w in enumerate(POOL_WINDOWS):
        a = ext[:, gi * POOL_GW:(gi + 1) * POOL_GW]
        step = 1
        while step < w:
            a = a + pltpu.roll(a, step, axis=0)
            step *= 2
        pooled = a[POOL_HALO:, :] / jnp.minimum(pos + 1.0, float(w))
        diff = pooled - u[:, gi * POOL_GW:(gi + 1) * POOL_GW]
        mixed.append(_dot(diff.astype(BF16), poolw_ref[gi]))
    pm = (jnp.concatenate(mixed, axis=-1) * pscale_ref[...]).astype(BF16)
    y_pool = _dot(pm, wpoolb_ref[...])

    gates = _dot(h, wg_ref[...])
    merged = (jax.nn.sigmoid(gates[:, :D_MODEL]) * y_gla
              + jax.nn.sigmoid(gates[:, D_MODEL:]) * y_pool).astype(BF16)
    x1 = xf + _dot(merged, wout_ref[...])
    x1_ref[...] = x1

    ms2 = jnp.mean(x1 * x1, axis=-1, keepdims=True)
    h2 = x1 * lax.rsqrt(ms2 + NORM_EPS) * g2_ref[...]
    h2_ref[...] = _pack_rows(h2)
    lt = _dot_nt(wrt_ref[...], h2.astype(BF16)) + brt_ref[:, 0:1]

    l0, l1, l2, l3 = lt[0:1], lt[1:2], lt[2:3], lt[3:4]
    gm = jnp.maximum(jnp.maximum(l0, l1), jnp.maximum(l2, l3))
    gidx = jnp.where(l0 == gm, 0, jnp.where(l1 == gm, 1, jnp.where(l2 == gm, 2, 3)))
    gp = 1.0 / (jnp.exp(l0 - gm) + jnp.exp(l1 - gm) + jnp.exp(l2 - gm) + jnp.exp(l3 - gm))
    el = jnp.where(gidx == 0, lt[8:16],
                   jnp.where(gidx == 1, lt[16:24], jnp.where(gidx == 2, lt[24:32], lt[32:40])))
    row8 = lax.broadcasted_iota(jnp.int32, (EXPERTS_PER_GROUP, ts), 0)
    m1 = jnp.max(el, axis=0, keepdims=True)
    i1 = jnp.min(jnp.where(el == m1, row8, EXPERTS_PER_GROUP), axis=0, keepdims=True)
    el2 = jnp.where(row8 == i1, -jnp.inf, el)
    m2 = jnp.max(el2, axis=0, keepdims=True)
    i2 = jnp.min(jnp.where(el2 == m2, row8, EXPERTS_PER_GROUP), axis=0, keepdims=True)
    dd = jnp.exp(m2 - m1)
    p1 = 1.0 / (1.0 + dd)
    p2 = dd / (1.0 + dd)
    e1 = gidx * EXPERTS_PER_GROUP + i1
    e2 = gidx * EXPERTS_PER_GROUP + i2
    eid_ref[0:1, :] = e1
    eid_ref[1:2, :] = e2
    gate_ref[...] = jnp.concatenate([gp * p1, gp * p2, jnp.zeros((6, ts), F32)], axis=0)

    row32 = lax.broadcasted_iota(jnp.int32, (N_EXPERTS, ts), 0)
    oh1 = row32 == e1
    oh2 = row32 == e2
    member = jnp.where(oh1, 1.0, 0.0) + jnp.where(oh2, 1.0, 0.0)
    before = (lax.broadcasted_iota(jnp.int32, (ts, ts), 0)
              < lax.broadcasted_iota(jnp.int32, (ts, ts), 1))
    prefix = _dot(member.astype(BF16), jnp.where(before, 1.0, 0.0).astype(BF16))
    base = cnt_s[...]
    prefix = prefix + jnp.concatenate([base] * (ts // LANE), axis=1)
    rank_ref[0:1, :] = jnp.sum(jnp.where(oh1, prefix, 0.0), axis=0, keepdims=True).astype(jnp.int32)
    rank_ref[1:2, :] = jnp.sum(jnp.where(oh2, prefix, 0.0), axis=0, keepdims=True).astype(jnp.int32)
    new_cnt = base + jnp.sum(member, axis=1, keepdims=True)
    cnt_s[...] = new_cnt
    cnt_ref[...] = new_cnt


def _const_spec(shape):
    nd = len(shape)
    return pl.BlockSpec(shape, lambda b, s, _nd=nd: (0,) * _nd, pipeline_mode=pl.Buffered(1))


def _mixer_call(x, wts, ts):
    B, S, D = x.shape
    T = B * S
    ns = S // ts
    tok_spec = pl.BlockSpec((None, ts, D), lambda b, s: (b, s, 0))
    flat_tok = lambda rows: pl.BlockSpec((rows, ts), lambda b, s: (0, b * ns + s))
    in_specs = [tok_spec] + [_const_spec(w.shape) for w in wts]
    out_shape = (
        jax.ShapeDtypeStruct((B, S, D), F32),
        jax.ShapeDtypeStruct((B, S, D // 2), jnp.int32),
        jax.ShapeDtypeStruct((2, T), jnp.int32),
        jax.ShapeDtypeStruct((8, T), F32),
        jax.ShapeDtypeStruct((2, T), jnp.int32),
        jax.ShapeDtypeStruct((N_EXPERTS, LANE), F32),
    )
    packed_spec = pl.BlockSpec((None, ts, D // 2), lambda b, s: (b, s, 0))
    out_specs = (tok_spec, packed_spec, flat_tok(2), flat_tok(8), flat_tok(2),
                 pl.BlockSpec((N_EXPERTS, LANE), lambda b, s: (0, 0)))
    grp = GLA_GROUP * GLA_CHUNK
    scratch = [
        pltpu.VMEM((GLA_HEADS, LANE, GLA_DV), F32),
        pltpu.VMEM((POOL_HALO, POOL_W), F32),
        pltpu.VMEM((N_EXPERTS, LANE), F32),
        pltpu.VMEM((GLA_HEADS, ts, LANE), BF16),
        pltpu.VMEM((GLA_HEADS, ts, LANE), BF16),
        pltpu.VMEM((GLA_HEADS // 2, ts // grp, LANE, grp), BF16),
        pltpu.VMEM((ts, VAL_W), BF16),
        pltpu.VMEM((ts // GLA_CHUNK, GLA_HEADS // 2, LANE, GLA_DV), F32),
        pltpu.VMEM((ts, VAL_W), F32),
    ]
    return pl.pallas_call(
        _mixer_kernel,
        grid=(B, ns),
        in_specs=in_specs,
        out_specs=out_specs,
        out_shape=out_shape,
        scratch_shapes=scratch,
        compiler_params=pltpu.CompilerParams(
            dimension_semantics=("arbitrary", "arbitrary"), vmem_limit_bytes=VMEM_LIMIT),
        name="mixer",
    )(x, *wts)


_PAD_BITS = tuple(p for p in (1 << i for i in reversed(range((EXP_BM - 1).bit_length()))) if p >= SUBLANE)


def _dispatch_kernel(pst_ref, cnt_ref, dest_ref, h2_ref, xs_ref, zero_s, sem, zsem):
    td = h2_ref.shape[0] * SUBLANE
    i = pl.program_id(0)

    def zero_copy(off, p):
        return pltpu.make_async_copy(zero_s.at[pl.ds(0, p)], xs_ref.at[pl.ds(off, p)], zsem)

    def for_each_pad_piece(e, fn):
        cnt = cnt_ref[e]
        start = pst_ref[e] + cnt
        end = start + (EXP_BM - cnt % EXP_BM) % EXP_BM
        aligned = jnp.minimum((start + SUBLANE - 1) // SUBLANE * SUBLANE, end)
        for j in range(SUBLANE - 1):
            @pl.when(start + j < aligned)
            def _(j=j):
                fn(zero_copy(start + j, 1))

        npad = end - aligned
        off = aligned
        for p in _PAD_BITS:
            hit = (npad & p) != 0

            @pl.when(hit)
            def _(off=off, p=p):
                fn(zero_copy(pl.multiple_of(off, SUBLANE), p))

            off = off + jnp.where(hit, p, 0)

    @pl.when(i == 0)
    def _():
        zero_s[...] = jnp.zeros_like(zero_s)

        def start_e(e, c):
            for_each_pad_piece(e, lambda cp: cp.start())
            return c

        def wait_e(e, c):
            for_each_pad_piece(e, lambda cp: cp.wait())
            return c

        lax.fori_loop(0, N_EXPERTS, start_e, 0)
        lax.fori_loop(0, N_EXPERTS, wait_e, 0)

        zrows = zero_s.shape[0]
        last_cnt = cnt_ref[N_EXPERTS - 1]
        used = pst_ref[N_EXPERTS - 1] + (last_cnt + EXP_BM - 1) // EXP_BM * EXP_BM
        first_piece = used // zrows
        n_pieces = xs_ref.shape[0] // zrows

        def tail_copy(t):
            return zero_copy(pl.multiple_of(t * zrows, zrows), zrows)

        lax.fori_loop(first_piece, n_pieces, lambda t, c: (tail_copy(t).start(), c)[1], 0)
        lax.fori_loop(first_piece, n_pieces, lambda t, c: (tail_copy(t).wait(), c)[1], 0)

    def issue(g, c):
        for u in range(SUBLANE):
            for kk in range(2):
                d = dest_ref[kk * td + g * SUBLANE + u]
                pltpu.make_async_copy(h2_ref.at[g, pl.ds(u, 1)], xs_ref.at[pl.ds(d, 1)], sem).start(priority=kk)
        return c

    lax.fori_loop(0, td // SUBLANE, issue, 0)
    for kk in range(2):
        pltpu.make_async_copy(xs_ref.at[pl.ds(0, td)], xs_ref.at[pl.ds(td, td)], sem).wait()


def _dispatch_call(h2, dest3, pstarts, counts, n_rows):
    T, D = h2.shape
    nt, _, td2 = dest3.shape
    td = td2 // 2
    h2 = h2.reshape(T // SUBLANE, SUBLANE, D)
    grid_spec = pltpu.PrefetchScalarGridSpec(
        num_scalar_prefetch=2,
        grid=(nt,),
        in_specs=[
            pl.BlockSpec((None, None, 2 * td), lambda i, *_: (i, 0, 0), memory_space=pltpu.SMEM),
            pl.BlockSpec((td // SUBLANE, SUBLANE, D), lambda i, *_: (i, 0, 0)),
        ],
        out_specs=pl.BlockSpec(memory_space=pl.ANY),
        scratch_shapes=[
            pltpu.VMEM((_PAD_BITS[0], D), h2.dtype),
            pltpu.SemaphoreType.DMA(()),
            pltpu.SemaphoreType.DMA(()),
        ],
    )
    return pl.pallas_call(
        _dispatch_kernel,
        grid_spec=grid_spec,
        out_shape=jax.ShapeDtypeStruct((n_rows, D), h2.dtype),
        compiler_params=pltpu.CompilerParams(
            dimension_semantics=("arbitrary",), vmem_limit_bytes=VMEM_LIMIT, has_side_effects=True),
        name="dispatch",
    )(pstarts, counts, dest3, h2)


def _expert_kernel(be_ref, nu_ref, x_ref, wg_ref, wu_ref, wd_ref, y_ref):
    b = pl.program_id(0)

    @pl.when(b < nu_ref[0])
    def _():
        lo, hi = _unpack_rows(x_ref[...])
        xb = jnp.concatenate([lo.astype(BF16), hi.astype(BF16)], axis=1)
        g = _dot(xb, wg_ref[...])
        u = _dot(xb, wu_ref[...])
        hmid = (g * jax.nn.sigmoid(g) * u).astype(BF16)
        y_ref[...] = _pack_rows(_dot(hmid, wd_ref[...]))

    @pl.when(b >= nu_ref[0])
    def _():
        y_ref[...] = jnp.zeros_like(y_ref)


def _expert_call(xs, block_e, n_used, wg, wu, wd):
    R, W = xs.shape
    D = 2 * W
    nb = R // EXP_BM
    last = lambda b, nu: jnp.minimum(b, nu[0] - 1)
    grid_spec = pltpu.PrefetchScalarGridSpec(
        num_scalar_prefetch=2,
        grid=(nb,),
        in_specs=[
            pl.BlockSpec((EXP_BM, W), lambda b, be, nu: (last(b, nu), 0)),
            pl.BlockSpec((None, D, D_EXPERT), lambda b, be, nu: (be[last(b, nu)], 0, 0)),
            pl.BlockSpec((None, D, D_EXPERT), lambda b, be, nu: (be[last(b, nu)], 0, 0)),
            pl.BlockSpec((None, D_EXPERT, D), lambda b, be, nu: (be[last(b, nu)], 0, 0)),
        ],
        out_specs=pl.BlockSpec((EXP_BM, W), lambda b, be, nu: (b, 0)),
    )
    return pl.pallas_call(
        _expert_kernel,
        grid_spec=grid_spec,
        out_shape=jax.ShapeDtypeStruct((R, W), xs.dtype),
        compiler_params=pltpu.CompilerParams(
            dimension_semantics=("arbitrary",), vmem_limit_bytes=VMEM_LIMIT),
        name="experts",
    )(block_e, n_used, xs, wg, wu, wd)


def _combine_kernel(dcur_ref, dnext_ref, x1_ref, gate_ref, gf_ref, y_ref, out_ref, ybuf, sem):
    tf = x1_ref.shape[0]
    i = pl.program_id(0)
    n = pl.num_programs(0)

    def issue_all(dref, slot):
        def group(g, c):
            for u in range(SUBLANE):
                for kk in range(2):
                    d = dref[kk * tf + g * SUBLANE + u]
                    pltpu.make_async_copy(y_ref.at[pl.ds(d, 1)], ybuf.at[slot, kk, g, pl.ds(u, 1)],
                                          sem.at[slot]).start(priority=kk)
            return c

        lax.fori_loop(0, tf // SUBLANE, group, 0)

    @pl.when(i == 0)
    def _():
        issue_all(dcur_ref, 0)

    for par in range(2):
        @pl.when((i % 2 == par) & (i + 1 < n))
        def _(par=par):
            issue_all(dnext_ref, 1 - par)

    slot = i % 2
    for kk in range(2):
        pltpu.make_async_copy(y_ref.at[pl.ds(0, tf)], y_ref.at[pl.ds(tf, tf)], sem.at[slot]).wait()

    g = jnp.concatenate([gate_ref[...], jnp.zeros((LANE - 8, tf), F32)], axis=0)
    gt = g.T
    w = ybuf.shape[-1]
    y0_lo, y0_hi = _unpack_rows(ybuf[slot, 0].reshape(tf, w))
    y1_lo, y1_hi = _unpack_rows(ybuf[slot, 1].reshape(tf, w))
    g0 = gt[:, 0:1]
    g1 = gt[:, 1:2]
    xo_lo = x1_ref[:, :w] + (g0 * y0_lo + g1 * y1_lo)
    xo_hi = x1_ref[:, w:] + (g0 * y0_hi + g1 * y1_hi)
    ms = (jnp.sum(xo_lo * xo_lo, axis=-1, keepdims=True)
          + jnp.sum(xo_hi * xo_hi, axis=-1, keepdims=True)) * (1.0 / (2 * w))
    scale = lax.rsqrt(ms + NORM_EPS)
    out_ref[:, :w] = xo_lo * scale * gf_ref[:, :w]
    out_ref[:, w:] = xo_hi * scale * gf_ref[:, w:]


def _combine_call(x1, dest3, gates, gf, ys):
    T, D = x1.shape
    nt, _, tf2 = dest3.shape
    tf = tf2 // 2
    return pl.pallas_call(
        _combine_kernel,
        grid=(nt,),
        in_specs=[
            pl.BlockSpec((None, None, 2 * tf), lambda i: (i, 0, 0), memory_space=pltpu.SMEM),
            pl.BlockSpec((None, None, 2 * tf), lambda i: (jnp.minimum(i + 1, nt - 1), 0, 0),
                         memory_space=pltpu.SMEM),
            pl.BlockSpec((tf, D), lambda i: (i, 0)),
            pl.BlockSpec((8, tf), lambda i: (0, i)),
            pl.BlockSpec((1, D), lambda i: (0, 0)),
            pl.BlockSpec(memory_space=pl.ANY),
        ],
        out_specs=pl.BlockSpec((tf, D), lambda i: (i, 0)),
        out_shape=jax.ShapeDtypeStruct((T, D), F32),
        scratch_shapes=[pltpu.VMEM((2, 2, tf // SUBLANE, SUBLANE, ys.shape[1]), ys.dtype),
                        pltpu.SemaphoreType.DMA((2,))],
        compiler_params=pltpu.CompilerParams(
            dimension_semantics=("arbitrary",), vmem_limit_bytes=VMEM_LIMIT),
        name="combine",
    )(dest3, dest3, x1, gates, gf, ys)


def _pick_tile(n, want):
    t = min(want, n)
    while n % t:
        t //= 2
    return t


def _mixer_weights(norm1_g, w_in, w_alpha_up, b_alpha, gla_norm_g, w_gla_branch, pool_w, pool_scale,
                   w_pool_branch, w_out, norm2_g, w_rg, b_rg, w_re, b_re):
    c0 = 2 * KEY_W + 2 * VAL_W
    c1 = c0 + GATE_RANK
    c2 = c1 + POOL_W
    w_qkvr = w_in[:, :c0].astype(BF16)
    w_a = jnp.pad(w_in[:, c0:c1], ((0, 0), (0, LANE - GATE_RANK))).astype(BF16)
    w_u = w_in[:, c1:c2].astype(BF16)
    w_g = w_in[:, c2:].astype(BF16)
    w_alpha = jnp.pad(w_alpha_up, ((0, LANE - GATE_RANK), (0, 0))).astype(BF16)
    w_re_t = jnp.transpose(w_re, (0, 2, 1)).reshape(N_EXPERTS, D_MODEL)
    wrt = jnp.zeros((ROUTER_ROWS, D_MODEL), F32)
    wrt = wrt.at[0:N_GROUPS].set(w_rg.T).at[8:8 + N_EXPERTS].set(w_re_t).astype(BF16)
    brt = jnp.zeros((ROUTER_ROWS,), F32).at[0:N_GROUPS].set(b_rg).at[8:8 + N_EXPERTS].set(b_re.reshape(-1))
    brt = jnp.broadcast_to(brt[:, None], (ROUTER_ROWS, LANE))
    row = lambda a: a.reshape(1, -1).astype(F32)
    return (row(norm1_g), w_qkvr, w_a, w_u, w_g, w_alpha, row(b_alpha), row(gla_norm_g),
            w_gla_branch.astype(BF16), pool_w.astype(BF16), row(pool_scale), w_pool_branch.astype(BF16),
            w_out.astype(BF16), row(norm2_g), wrt, brt)


def kernel(x, norm1_g, w_in, w_alpha_up, b_alpha, gla_norm_g, w_gla_branch, pool_w, pool_scale,
           w_pool_branch, w_out, norm2_g, w_router_group, b_router_group, w_router_expert,
           b_router_expert, w_exp_gate, w_exp_up, w_exp_down, norm_f_g):
    B, S, D = x.shape
    T = B * S
    depth = w_in.shape[0]
    ts = _pick_tile(S, MIX_TS)
    tr = _pick_tile(T, ROW_TILE)
    n_assign = 2 * T
    n_blocks = -(-(n_assign + N_EXPERTS * (EXP_BM - 1)) // EXP_BM)
    n_rows = n_blocks * EXP_BM

    assert depth == 1, "kernel supports the problem's DEPTH=1"
    for l in range(depth):
        wts = _mixer_weights(norm1_g[l], w_in[l], w_alpha_up[l], b_alpha[l], gla_norm_g[l], w_gla_branch[l],
                             pool_w[l], pool_scale[l], w_pool_branch[l], w_out[l], norm2_g[l],
                             w_router_group[l], b_router_group[l], w_router_expert[l], b_router_expert[l])
        x1, h2, eid, gates, rank, cnt = _mixer_call(x, wts, ts)

        counts = cnt[:, 0].astype(jnp.int32)
        padded = ((counts + EXP_BM - 1) // EXP_BM) * EXP_BM
        pends = jnp.cumsum(padded)
        pstarts = pends - padded
        expert_col = jnp.arange(N_EXPERTS, dtype=jnp.int32)[:, None, None]
        dest = jnp.sum(jnp.where(eid[None] == expert_col, pstarts[:, None, None], 0), axis=0) + rank
        dest3 = dest.reshape(2, T // tr, tr).transpose(1, 0, 2).reshape(T // tr, 1, 2 * tr)
        blk_start = jnp.arange(n_blocks, dtype=jnp.int32) * EXP_BM
        block_e = jnp.minimum(jnp.sum(pends[None, :] <= blk_start[:, None], axis=1), N_EXPERTS - 1).astype(jnp.int32)
        n_used = (pends[-1:] // EXP_BM).astype(jnp.int32)

        xs = _dispatch_call(h2.reshape(T, D // 2), dest3, pstarts.astype(jnp.int32), counts, n_rows)
        ys = _expert_call(xs, block_e, n_used, w_exp_gate[l].astype(BF16), w_exp_up[l].astype(BF16),
                          w_exp_down[l].astype(BF16))
        out = _combine_call(x1.reshape(T, D), dest3, gates, norm_f_g.reshape(1, D).astype(F32), ys)
        x = out.reshape(B, S, D)
    return x
```

```python
import jax
import jax.numpy as jnp
from jax import lax
from jax.experimental import pallas as pl
from jax.experimental.pallas import tpu as pltpu

F32 = jnp.float32
BF16 = jnp.bfloat16

D_MODEL = 1024
GLA_HEADS = 4
GLA_DK = 64
GLA_DV = 128
KEY_W = GLA_HEADS * GLA_DK
VAL_W = GLA_HEADS * GLA_DV
GATE_RANK = 16
GATE_NORMALIZER = 16.0
GLA_CHUNK = 64
POOL_WINDOWS = (2, 4, 8, 16)
POOL_W = 512
POOL_GW = 128
N_GROUPS = 4
EXPERTS_PER_GROUP = 8
N_EXPERTS = 32
D_EXPERT = 256
NORM_EPS = 1e-6

LANE = 128
SUBLANE = 8
POOL_HALO = 16
ROUTER_ROWS = 48

MIX_TS = 512
ROW_TILE = 512
GLA_GROUP = 2
EXP_BM = 512
VMEM_LIMIT = 56 * 1024 * 1024


def _dot(a, b):
    return jnp.dot(a, b, preferred_element_type=F32)


def _dot_nt(a, b):
    return lax.dot_general(a, b, (((1,), (1,)), ((), ())), preferred_element_type=F32)


def _dot_tn(a, b):
    return lax.dot_general(a, b, (((0,), (0,)), ((), ())), preferred_element_type=F32)


def _bf16_bits(x):
    return lax.bitcast_convert_type(x.astype(BF16).astype(F32), jnp.int32)


def _pack_rows(x):
    w = x.shape[1] // 2
    return lax.shift_right_logical(_bf16_bits(x[:, :w]), 16) | _bf16_bits(x[:, w:])


def _unpack_rows(words):
    lo = lax.bitcast_convert_type(lax.shift_left(words, 16), F32)
    hi = lax.bitcast_convert_type(words & -65536, F32)
    return lo, hi


def _chunk_cumsum(x, chunk):
    n, w = x.shape
    pos = lax.broadcasted_iota(jnp.int32, (n, w), 0) % chunk
    step = 1
    while step < chunk:
        if step < 8:
            shifted = pltpu.roll(x, step, axis=0)
        else:
            shifted = jnp.concatenate([jnp.zeros((step, w), x.dtype), x[:n - step]], axis=0)
        x = x + jnp.where(pos >= step, shifted, 0.0)
        step *= 2
    return x


def _mixer_kernel(x_ref, g1_ref, wqkvr_ref, wa_ref, wu_ref, wg_ref, walpha_ref, balpha_ref,
                  glag_ref, wglab_ref, poolw_ref, pscale_ref, wpoolb_ref, wout_ref, g2_ref,
                  wrt_ref, brt_ref,
                  x1_ref, h2_ref, eid_ref, gate_ref, rank_ref, cnt_ref,
                  state_s, carry_s, cnt_s, qdm_s, kem_s, kdt_s, v_s, dec_s, o_s, ypool_s, gates_s):
    ts = x_ref.shape[0]
    b_idx = pl.program_id(0)
    s_idx = pl.program_id(1)

    @pl.when(s_idx == 0)
    def _():
        state_s[...] = jnp.zeros_like(state_s)
        carry_s[...] = jnp.zeros_like(carry_s)

    @pl.when((b_idx == 0) & (s_idx == 0))
    def _():
        cnt_s[...] = jnp.zeros_like(cnt_s)

    xf = x_ref[...]
    ms = jnp.mean(xf * xf, axis=-1, keepdims=True)
    h = (xf * lax.rsqrt(ms + NORM_EPS) * g1_ref[...]).astype(BF16)

    qkvr = _dot(h, wqkvr_ref[...])
    q = qkvr[:, 0:KEY_W]
    k = qkvr[:, KEY_W:2 * KEY_W]
    v = qkvr[:, 2 * KEY_W:2 * KEY_W + VAL_W]
    r = qkvr[:, 2 * KEY_W + VAL_W:]

    a_low = _dot(h, wa_ref[...])
    z = _dot(a_low.astype(BF16), walpha_ref[...]) + balpha_ref[...]
    log_a = (jnp.minimum(z, 0.0) - jnp.log1p(jnp.exp(-jnp.abs(z)))) * (1.0 / GATE_NORMALIZER)

    u = _dot(h, wu_ref[...])
    ext = jnp.concatenate([carry_s[...], u], axis=0)
    carry_s[...] = u[ts - POOL_HALO:, :]
    pos = (s_idx * ts + lax.broadcasted_iota(jnp.int32, (ts, 1), 0)).astype(F32)
    mixed = []
    for gi, w in enumerate(POOL_WINDOWS):
        a = ext[:, gi * POOL_GW:(gi + 1) * POOL_GW]
        step = 1
        while step < w:
            a = a + pltpu.roll(a, step, axis=0)
            step *= 2
        pooled = a[POOL_HALO:, :] / jnp.minimum(pos + 1.0, float(w))
        diff = pooled - u[:, gi * POOL_GW:(gi + 1) * POOL_GW]
        mixed.append(_dot(diff.astype(BF16), poolw_ref[gi]))
    pm = (jnp.concatenate(mixed, axis=-1) * pscale_ref[...]).astype(BF16)
    ypool_s[...] = _dot(pm, wpoolb_ref[...])
    gates_s[...] = _dot(h, wg_ref[...])

    nc = ts // GLA_CHUNK
    b = _chunk_cumsum(log_a, GLA_CHUNK)
    b3 = b.reshape(nc, GLA_CHUNK, KEY_W)
    b_last = b3[:, GLA_CHUNK - 1:GLA_CHUNK, :]
    lane = lax.broadcasted_iota(jnp.int32, (ts, LANE), 1)
    qd = q * jnp.exp(b) * (GLA_DK ** -0.5)
    ke = (k.reshape(nc, GLA_CHUNK, KEY_W) * jnp.exp(b_last - b3)).reshape(ts, KEY_W)
    for hd in range(GLA_HEADS):
        pair = slice((hd // 2) * LANE, (hd // 2 + 1) * LANE)
        mine = (lane < GLA_DK) if hd % 2 == 0 else (lane >= GLA_DK)
        qdm_s[hd] = jnp.where(mine, qd[:, pair], 0.0).astype(BF16)
        kem_s[hd] = jnp.where(mine, ke[:, pair], 0.0).astype(BF16)
    v_s[...] = v.astype(BF16)
    grp = GLA_GROUP * GLA_CHUNK
    kd = k * jnp.exp(-b)
    for p in range(GLA_HEADS // 2):
        kd_t = kd[:, p * LANE:(p + 1) * LANE].T.astype(BF16)
        for j in range(ts // grp):
            kdt_s[p, j] = kd_t[:, j * grp:(j + 1) * grp]
    dec_rows = jnp.concatenate([jnp.exp(b_last[c]) for c in range(nc)]
                               + [jnp.zeros((LANE - nc, KEY_W), F32)], axis=0)
    dec_t = dec_rows.T
    for c in range(nc):
        for p in range(GLA_HEADS // 2):
            dec_s[c, p] = jnp.broadcast_to(dec_t[p * LANE:(p + 1) * LANE, c:c + 1], (LANE, GLA_DV))

    rg = lax.broadcasted_iota(jnp.int32, (grp, grp), 0)
    cg = lax.broadcasted_iota(jnp.int32, (grp, grp), 1)
    tri_g = (rg >= cg) & (rg // GLA_CHUNK == cg // GLA_CHUNK)

    def chunk_group_body(j, carry):
        rows_g = pl.ds(pl.multiple_of(j * grp, grp), grp)
        for hd in range(GLA_HEADS):
            p = hd // 2
            cols = slice(hd * GLA_DV, (hd + 1) * GLA_DV)
            qm = qdm_s[hd, rows_g, :]
            vh = v_s[rows_g, cols]
            km = kem_s[hd, rows_g, :]
            scores = jnp.where(tri_g, _dot(qm, kdt_s[p, j]), 0.0)
            o_intra = _dot(scores.astype(BF16), vh)
            st = state_s[hd]
            outs = []
            for cc in range(GLA_GROUP):
                c = GLA_GROUP * j + cc
                part = slice(cc * GLA_CHUNK, (cc + 1) * GLA_CHUNK)
                outs.append(o_intra[part] + _dot(qm[part], st.astype(BF16)))
                st = dec_s[c, p] * st + _dot_tn(km[part], vh[part])
            state_s[hd] = st
            o_s[rows_g, cols] = jnp.concatenate(outs, axis=0)
        return carry

    lax.fori_loop(0, nc // GLA_GROUP, chunk_group_body, 0)

    o_all = o_s[...]
    parts = []
    for hd in range(GLA_HEADS):
        oh = o_all[:, hd * GLA_DV:(hd + 1) * GLA_DV]
        parts.append(oh * lax.rsqrt(jnp.mean(oh * oh, axis=-1, keepdims=True) + NORM_EPS))
    o_n = jnp.concatenate(parts, axis=-1) * glag_ref[...]
    o_g = (o_n * (r * jax.nn.sigmoid(r))).astype(BF16)
    y_gla = _dot(o_g, wglab_ref[...])

    merged = (jax.nn.sigmoid(gates_s[:, :D_MODEL]) * y_gla
              + jax.nn.sigmoid(gates_s[:, D_MODEL:]) * ypool_s[...]).astype(BF16)
    x1 = xf + _dot(merged, wout_ref[...])
    x1_ref[...] = x1

    ms2 = jnp.mean(x1 * x1, axis=-1, keepdims=True)
    h2 = x1 * lax.rsqrt(ms2 + NORM_EPS) * g2_ref[...]
    h2_ref[...] = _pack_rows(h2)
    lt = _dot_nt(wrt_ref[...], h2.astype(BF16)) + brt_ref[:, 0:1]

    l0, l1, l2, l3 = lt[0:1], lt[1:2], lt[2:3], lt[3:4]
    gm = jnp.maximum(jnp.maximum(l0, l1), jnp.maximum(l2, l3))
    gidx = jnp.where(l0 == gm, 0, jnp.where(l1 == gm, 1, jnp.where(l2 == gm, 2, 3)))
    gp = 1.0 / (jnp.exp(l0 - gm) + jnp.exp(l1 - gm) + jnp.exp(l2 - gm) + jnp.exp(l3 - gm))
    el = jnp.where(gidx == 0, lt[8:16],
                   jnp.where(gidx == 1, lt[16:24], jnp.where(gidx == 2, lt[24:32], lt[32:40])))
    row8 = lax.broadcasted_iota(jnp.int32, (EXPERTS_PER_GROUP, ts), 0)
    m1 = jnp.max(el, axis=0, keepdims=True)
    i1 = jnp.min(jnp.where(el == m1, row8, EXPERTS_PER_GROUP), axis=0, keepdims=True)
    el2 = jnp.where(row8 == i1, -jnp.inf, el)
    m2 = jnp.max(el2, axis=0, keepdims=True)
    i2 = jnp.min(jnp.where(el2 == m2, row8, EXPERTS_PER_GROUP), axis=0, keepdims=True)
    dd = jnp.exp(m2 - m1)
    p1 = 1.0 / (1.0 + dd)
    p2 = dd / (1.0 + dd)
    e1 = gidx * EXPERTS_PER_GROUP + i1
    e2 = gidx * EXPERTS_PER_GROUP + i2
    eid_ref[0:1, :] = e1
    eid_ref[1:2, :] = e2
    gate_ref[...] = jnp.concatenate([gp * p1, gp * p2, jnp.zeros((6, ts), F32)], axis=0)

    row32 = lax.broadcasted_iota(jnp.int32, (N_EXPERTS, ts), 0)
    oh1 = row32 == e1
    oh2 = row32 == e2
    member = jnp.where(oh1, 1.0, 0.0) + jnp.where(oh2, 1.0, 0.0)
    before = (lax.broadcasted_iota(jnp.int32, (ts, ts), 0)
              < lax.broadcasted_iota(jnp.int32, (ts, ts), 1))
    prefix = _dot(member.astype(BF16), jnp.where(before, 1.0, 0.0).astype(BF16))
    base = cnt_s[...]
    prefix = prefix + jnp.concatenate([base] * (ts // LANE), axis=1)
    rank_ref[0:1, :] = jnp.sum(jnp.where(oh1, prefix, 0.0), axis=0, keepdims=True).astype(jnp.int32)
    rank_ref[1:2, :] = jnp.sum(jnp.where(oh2, prefix, 0.0), axis=0, keepdims=True).astype(jnp.int32)
    new_cnt = base + jnp.sum(member, axis=1, keepdims=True)
    cnt_s[...] = new_cnt
    cnt_ref[...] = new_cnt


def _const_spec(shape):
    nd = len(shape)
    return pl.BlockSpec(shape, lambda b, s, _nd=nd: (0,) * _nd, pipeline_mode=pl.Buffered(1))


def _mixer_call(x, wts, ts):
    B, S, D = x.shape
    T = B * S
    ns = S // ts
    tok_spec = pl.BlockSpec((None, ts, D), lambda b, s: (b, s, 0))
    flat_tok = lambda rows: pl.BlockSpec((rows, ts), lambda b, s: (0, b * ns + s))
    in_specs = [tok_spec] + [_const_spec(w.shape) for w in wts]
    out_shape = (
        jax.ShapeDtypeStruct((B, S, D), F32),
        jax.ShapeDtypeStruct((B, S, D // 2), jnp.int32),
        jax.ShapeDtypeStruct((2, T), jnp.int32),
        jax.ShapeDtypeStruct((8, T), F32),
        jax.ShapeDtypeStruct((2, T), jnp.int32),
        jax.ShapeDtypeStruct((N_EXPERTS, LANE), F32),
    )
    packed_spec = pl.BlockSpec((None, ts, D // 2), lambda b, s: (b, s, 0))
    out_specs = (tok_spec, packed_spec, flat_tok(2), flat_tok(8), flat_tok(2),
                 pl.BlockSpec((N_EXPERTS, LANE), lambda b, s: (0, 0)))
    grp = GLA_GROUP * GLA_CHUNK
    scratch = [
        pltpu.VMEM((GLA_HEADS, LANE, GLA_DV), F32),
        pltpu.VMEM((POOL_HALO, POOL_W), F32),
        pltpu.VMEM((N_EXPERTS, LANE), F32),
        pltpu.VMEM((GLA_HEADS, ts, LANE), BF16),
        pltpu.VMEM((GLA_HEADS, ts, LANE), BF16),
        pltpu.VMEM((GLA_HEADS // 2, ts // grp, LANE, grp), BF16),
        pltpu.VMEM((ts, VAL_W), BF16),
        pltpu.VMEM((ts // GLA_CHUNK, GLA_HEADS // 2, LANE, GLA_DV), F32),
        pltpu.VMEM((ts, VAL_W), F32),
        pltpu.VMEM((ts, D), F32),
        pltpu.VMEM((ts, 2 * D), F32),
    ]
    return pl.pallas_call(
        _mixer_kernel,
        grid=(B, ns),
        in_specs=in_specs,
        out_specs=out_specs,
        out_shape=out_shape,
        scratch_shapes=scratch,
        compiler_params=pltpu.CompilerParams(
            dimension_semantics=("arbitrary", "arbitrary"), vmem_limit_bytes=VMEM_LIMIT),
        name="mixer",
    )(x, *wts)


_PAD_BITS = tuple(p for p in (1 << i for i in reversed(range((EXP_BM - 1).bit_length()))) if p >= SUBLANE)


def _dispatch_kernel(pst_ref, cnt_ref, dest_ref, h2_ref, xs_ref, zero_s, sem, zsem):
    td = h2_ref.shape[0] * SUBLANE
    i = pl.program_id(0)

    def zero_copy(off, p):
        return pltpu.make_async_copy(zero_s.at[pl.ds(0, p)], xs_ref.at[pl.ds(off, p)], zsem)

    def for_each_pad_piece(e, fn):
        cnt = cnt_ref[e]
        start = pst_ref[e] + cnt
        end = start + (EXP_BM - cnt % EXP_BM) % EXP_BM
        aligned = jnp.minimum((start + SUBLANE - 1) // SUBLANE * SUBLANE, end)
        for j in range(SUBLANE - 1):
            @pl.when(start + j < aligned)
            def _(j=j):
                fn(zero_copy(start + j, 1))

        npad = end - aligned
        off = aligned
        for p in _PAD_BITS:
            hit = (npad & p) != 0

            @pl.when(hit)
            def _(off=off, p=p):
                fn(zero_copy(pl.multiple_of(off, SUBLANE), p))

            off = off + jnp.where(hit, p, 0)

    @pl.when(i == 0)
    def _():
        zero_s[...] = jnp.zeros_like(zero_s)

        def start_e(e, c):
            for_each_pad_piece(e, lambda cp: cp.start())
            return c

        def wait_e(e, c):
            for_each_pad_piece(e, lambda cp: cp.wait())
            return c

        lax.fori_loop(0, N_EXPERTS, start_e, 0)
        lax.fori_loop(0, N_EXPERTS, wait_e, 0)

        zrows = zero_s.shape[0]
        last_cnt = cnt_ref[N_EXPERTS - 1]
        used = pst_ref[N_EXPERTS - 1] + (last_cnt + EXP_BM - 1) // EXP_BM * EXP_BM
        first_piece = used // zrows
        n_pieces = xs_ref.shape[0] // zrows

        def tail_copy(t):
            return zero_copy(pl.multiple_of(t * zrows, zrows), zrows)

        lax.fori_loop(first_piece, n_pieces, lambda t, c: (tail_copy(t).start(), c)[1], 0)
        lax.fori_loop(first_piece, n_pieces, lambda t, c: (tail_copy(t).wait(), c)[1], 0)

    def issue(g, c):
        for u in range(SUBLANE):
            for kk in range(2):
                d = dest_ref[kk * td + g * SUBLANE + u]
                pltpu.make_async_copy(h2_ref.at[g, pl.ds(u, 1)], xs_ref.at[pl.ds(d, 1)], sem).start(priority=kk)
        return c

    lax.fori_loop(0, td // SUBLANE, issue, 0)
    for kk in range(2):
        pltpu.make_async_copy(xs_ref.at[pl.ds(0, td)], xs_ref.at[pl.ds(td, td)], sem).wait()


def _dispatch_call(h2, dest3, pstarts, counts, n_rows):
    T, D = h2.shape
    nt, _, td2 = dest3.shape
    td = td2 // 2
    h2 = h2.reshape(T // SUBLANE, SUBLANE, D)
    grid_spec = pltpu.PrefetchScalarGridSpec(
        num_scalar_prefetch=2,
        grid=(nt,),
        in_specs=[
            pl.BlockSpec((None, None, 2 * td), lambda i, *_: (i, 0, 0), memory_space=pltpu.SMEM),
            pl.BlockSpec((td // SUBLANE, SUBLANE, D), lambda i, *_: (i, 0, 0)),
        ],
        out_specs=pl.BlockSpec(memory_space=pl.ANY),
        scratch_shapes=[
            pltpu.VMEM((_PAD_BITS[0], D), h2.dtype),
            pltpu.SemaphoreType.DMA(()),
            pltpu.SemaphoreType.DMA(()),
        ],
    )
    return pl.pallas_call(
        _dispatch_kernel,
        grid_spec=grid_spec,
        out_shape=jax.ShapeDtypeStruct((n_rows, D), h2.dtype),
        compiler_params=pltpu.CompilerParams(
            dimension_semantics=("arbitrary",), vmem_limit_bytes=VMEM_LIMIT, has_side_effects=True),
        name="dispatch",
    )(pstarts, counts, dest3, h2)


def _expert_kernel(be_ref, nu_ref, x_ref, wg_ref, wu_ref, wd_ref, y_ref):
    b = pl.program_id(0)

    @pl.when(b < nu_ref[0])
    def _():
        lo, hi = _unpack_rows(x_ref[...])
        xb = jnp.concatenate([lo.astype(BF16), hi.astype(BF16)], axis=1)
        g = _dot(xb, wg_ref[...])
        u = _dot(xb, wu_ref[...])
        hmid = (g * jax.nn.sigmoid(g) * u).astype(BF16)
        y_ref[...] = _pack_rows(_dot(hmid, wd_ref[...]))

    @pl.when(b >= nu_ref[0])
    def _():
        y_ref[...] = jnp.zeros_like(y_ref)


def _expert_call(xs, block_e, n_used, wg, wu, wd):
    R, W = xs.shape
    D = 2 * W
    nb = R // EXP_BM
    last = lambda b, nu: jnp.minimum(b, nu[0] - 1)
    grid_spec = pltpu.PrefetchScalarGridSpec(
        num_scalar_prefetch=2,
        grid=(nb,),
        in_specs=[
            pl.BlockSpec((EXP_BM, W), lambda b, be, nu: (last(b, nu), 0)),
            pl.BlockSpec((None, D, D_EXPERT), lambda b, be, nu: (be[last(b, nu)], 0, 0)),
            pl.BlockSpec((None, D, D_EXPERT), lambda b, be, nu: (be[last(b, nu)], 0, 0)),
            pl.BlockSpec((None, D_EXPERT, D), lambda b, be, nu: (be[last(b, nu)], 0, 0)),
        ],
        out_specs=pl.BlockSpec((EXP_BM, W), lambda b, be, nu: (b, 0)),
    )
    return pl.pallas_call(
        _expert_kernel,
        grid_spec=grid_spec,
        out_shape=jax.ShapeDtypeStruct((R, W), xs.dtype),
        compiler_params=pltpu.CompilerParams(
            dimension_semantics=("arbitrary",), vmem_limit_bytes=VMEM_LIMIT),
        name="experts",
    )(block_e, n_used, xs, wg, wu, wd)


def _combine_kernel(dcur_ref, dnext_ref, x1_ref, gate_ref, gf_ref, y_ref, out_ref, ybuf, sem):
    tf = x1_ref.shape[0]
    i = pl.program_id(0)
    n = pl.num_programs(0)

    def issue_all(dref, slot):
        def group(g, c):
            for u in range(SUBLANE):
                for kk in range(2):
                    d = dref[kk * tf + g * SUBLANE + u]
                    pltpu.make_async_copy(y_ref.at[pl.ds(d, 1)], ybuf.at[slot, kk, g, pl.ds(u, 1)],
                                          sem.at[slot]).start(priority=kk)
            return c

        lax.fori_loop(0, tf // SUBLANE, group, 0)

    @pl.when(i == 0)
    def _():
        issue_all(dcur_ref, 0)

    for par in range(2):
        @pl.when((i % 2 == par) & (i + 1 < n))
        def _(par=par):
            issue_all(dnext_ref, 1 - par)

    slot = i % 2
    for kk in range(2):
        pltpu.make_async_copy(y_ref.at[pl.ds(0, tf)], y_ref.at[pl.ds(tf, tf)], sem.at[slot]).wait()

    g = jnp.concatenate([gate_ref[...], jnp.zeros((LANE - 8, tf), F32)], axis=0)
    gt = g.T
    w = ybuf.shape[-1]
    y0_lo, y0_hi = _unpack_rows(ybuf[slot, 0].reshape(tf, w))
    y1_lo, y1_hi = _unpack_rows(ybuf[slot, 1].reshape(tf, w))
    g0 = gt[:, 0:1]
    g1 = gt[:, 1:2]
    xo_lo = x1_ref[:, :w] + (g0 * y0_lo + g1 * y1_lo)
    xo_hi = x1_ref[:, w:] + (g0 * y0_hi + g1 * y1_hi)
    ms = (jnp.sum(xo_lo * xo_lo, axis=-1, keepdims=True)
          + jnp.sum(xo_hi * xo_hi, axis=-1, keepdims=True)) * (1.0 / (2 * w))
    scale = lax.rsqrt(ms + NORM_EPS)
    out_ref[:, :w] = xo_lo * scale * gf_ref[:, :w]
    out_ref[:, w:] = xo_hi * scale * gf_ref[:, w:]


def _combine_call(x1, dest3, gates, gf, ys):
    T, D = x1.shape
    nt, _, tf2 = dest3.shape
    tf = tf2 // 2
    return pl.pallas_call(
        _combine_kernel,
        grid=(nt,),
        in_specs=[
            pl.BlockSpec((None, None, 2 * tf), lambda i: (i, 0, 0), memory_space=pltpu.SMEM),
            pl.BlockSpec((None, None, 2 * tf), lambda i: (jnp.minimum(i + 1, nt - 1), 0, 0),
                         memory_space=pltpu.SMEM),
            pl.BlockSpec((tf, D), lambda i: (i, 0)),
            pl.BlockSpec((8, tf), lambda i: (0, i)),
            pl.BlockSpec((1, D), lambda i: (0, 0)),
            pl.BlockSpec(memory_space=pl.ANY),
        ],
        out_specs=pl.BlockSpec((tf, D), lambda i: (i, 0)),
        out_shape=jax.ShapeDtypeStruct((T, D), F32),
        scratch_shapes=[pltpu.VMEM((2, 2, tf // SUBLANE, SUBLANE, ys.shape[1]), ys.dtype),
                        pltpu.SemaphoreType.DMA((2,))],
        compiler_params=pltpu.CompilerParams(
            dimension_semantics=("arbitrary",), vmem_limit_bytes=VMEM_LIMIT),
        name="combine",
    )(dest3, dest3, x1, gates, gf, ys)


def _pick_tile(n, want):
    t = min(want, n)
    while n % t:
        t //= 2
    return t


def _mixer_weights(norm1_g, w_in, w_alpha_up, b_alpha, gla_norm_g, w_gla_branch, pool_w, pool_scale,
                   w_pool_branch, w_out, norm2_g, w_rg, b_rg, w_re, b_re):
    c0 = 2 * KEY_W + 2 * VAL_W
    c1 = c0 + GATE_RANK
    c2 = c1 + POOL_W
    w_qkvr = w_in[:, :c0].astype(BF16)
    w_a = jnp.pad(w_in[:, c0:c1], ((0, 0), (0, LANE - GATE_RANK))).astype(BF16)
    w_u = w_in[:, c1:c2].astype(BF16)
    w_g = w_in[:, c2:].astype(BF16)
    w_alpha = jnp.pad(w_alpha_up, ((0, LANE - GATE_RANK), (0, 0))).astype(BF16)
    w_re_t = jnp.transpose(w_re, (0, 2, 1)).reshape(N_EXPERTS, D_MODEL)
    wrt = jnp.zeros((ROUTER_ROWS, D_MODEL), F32)
    wrt = wrt.at[0:N_GROUPS].set(w_rg.T).at[8:8 + N_EXPERTS].set(w_re_t).astype(BF16)
    brt = jnp.zeros((ROUTER_ROWS,), F32).at[0:N_GROUPS].set(b_rg).at[8:8 + N_EXPERTS].set(b_re.reshape(-1))
    brt = jnp.broadcast_to(brt[:, None], (ROUTER_ROWS, LANE))
    row = lambda a: a.reshape(1, -1).astype(F32)
    return (row(norm1_g), w_qkvr, w_a, w_u, w_g, w_alpha, row(b_alpha), row(gla_norm_g),
            w_gla_branch.astype(BF16), pool_w.astype(BF16), row(pool_scale), w_pool_branch.astype(BF16),
            w_out.astype(BF16), row(norm2_g), wrt, brt)


def kernel(x, norm1_g, w_in, w_alpha_up, b_alpha, gla_norm_g, w_gla_branch, pool_w, pool_scale,
           w_pool_branch, w_out, norm2_g, w_router_group, b_router_group, w_router_expert,
           b_router_expert, w_exp_gate, w_exp_up, w_exp_down, norm_f_g):
    B, S, D = x.shape
    T = B * S
    depth = w_in.shape[0]
    ts = _pick_tile(S, MIX_TS)
    tr = _pick_tile(T, ROW_TILE)
    n_assign = 2 * T
    n_blocks = -(-(n_assign + N_EXPERTS * (EXP_BM - 1)) // EXP_BM)
    n_rows = n_blocks * EXP_BM

    assert depth == 1, "kernel supports the problem's DEPTH=1"
    for l in range(depth):
        wts = _mixer_weights(norm1_g[l], w_in[l], w_alpha_up[l], b_alpha[l], gla_norm_g[l], w_gla_branch[l],
                             pool_w[l], pool_scale[l], w_pool_branch[l], w_out[l], norm2_g[l],
                             w_router_group[l], b_router_group[l], w_router_expert[l], b_router_expert[l])
        x1, h2, eid, gates, rank, cnt = _mixer_call(x, wts, ts)

        counts = cnt[:, 0].astype(jnp.int32)
        padded = ((counts + EXP_BM - 1) // EXP_BM) * EXP_BM
        pends = jnp.cumsum(padded)
        pstarts = pends - padded
        expert_col = jnp.arange(N_EXPERTS, dtype=jnp.int32)[:, None, None]
        dest = jnp.sum(jnp.where(eid[None] == expert_col, pstarts[:, None, None], 0), axis=0) + rank
        dest3 = dest.reshape(2, T // tr, tr).transpose(1, 0, 2).reshape(T // tr, 1, 2 * tr)
        blk_start = jnp.arange(n_blocks, dtype=jnp.int32) * EXP_BM
        block_e = jnp.minimum(jnp.sum(pends[None, :] <= blk_start[:, None], axis=1), N_EXPERTS - 1).astype(jnp.int32)
        n_used = (pends[-1:] // EXP_BM).astype(jnp.int32)

        xs = _dispatch_call(h2.reshape(T, D // 2), dest3, pstarts.astype(jnp.int32), counts, n_rows)
        ys = _expert_call(xs, block_e, n_used, w_exp_gate[l].astype(BF16), w_exp_up[l].astype(BF16),
                          w_exp_down[l].astype(BF16))
        out = _combine_call(x1.reshape(T, D), dest3, gates, norm_f_g.reshape(1, D).astype(F32), ys)
        x = out.reshape(B, S, D)
    return x
```

```python
import jax
import jax.numpy as jnp
from jax import lax
from jax.experimental import pallas as pl
from jax.experimental.pallas import tpu as pltpu

F32 = jnp.float32
BF16 = jnp.bfloat16

D_MODEL = 1024
GLA_HEADS = 4
GLA_DK = 64
GLA_DV = 128
KEY_W = GLA_HEADS * GLA_DK
VAL_W = GLA_HEADS * GLA_DV
GATE_RANK = 16
GATE_NORMALIZER = 16.0
GLA_CHUNK = 64
POOL_WINDOWS = (2, 4, 8, 16)
POOL_W = 512
POOL_GW = 128
N_GROUPS = 4
EXPERTS_PER_GROUP = 8
N_EXPERTS = 32
D_EXPERT = 256
NORM_EPS = 1e-6

LANE = 128
SUBLANE = 8
POOL_HALO = 16
ROUTER_ROWS = 48

MIX_TS = 512
ROW_TILE = 512
GLA_GROUP = 2
EXP_BM = 512
VMEM_LIMIT = 56 * 1024 * 1024


def _dot(a, b):
    return jnp.dot(a, b, preferred_element_type=F32)


def _dot_nt(a, b):
    return lax.dot_general(a, b, (((1,), (1,)), ((), ())), preferred_element_type=F32)


def _dot_tn(a, b):
    return lax.dot_general(a, b, (((0,), (0,)), ((), ())), preferred_element_type=F32)


def _bf16_bits(x):
    return lax.bitcast_convert_type(x.astype(BF16).astype(F32), jnp.int32)


def _pack_rows(x):
    w = x.shape[1] // 2
    return lax.shift_right_logical(_bf16_bits(x[:, :w]), 16) | _bf16_bits(x[:, w:])


def _unpack_rows(words):
    lo = lax.bitcast_convert_type(lax.shift_left(words, 16), F32)
    hi = lax.bitcast_convert_type(words & -65536, F32)
    return lo, hi


ROW_TILES = D_MODEL // 2 // LANE


def _store_rows(ref, words):
    n = words.shape[0]
    for t in range(ROW_TILES):
        ref[pl.ds(t, n, stride=ROW_TILES), :] = words[:, t * LANE:(t + 1) * LANE]


def _load_rows(ref):
    n = ref.shape[0] // ROW_TILES
    return jnp.concatenate([ref[pl.ds(t, n, stride=ROW_TILES), :] for t in range(ROW_TILES)], axis=1)


def _chunk_cumsum(x, chunk):
    n, w = x.shape
    pos = lax.broadcasted_iota(jnp.int32, (n, w), 0) % chunk
    step = 1
    while step < chunk:
        if step < 8:
            shifted = pltpu.roll(x, step, axis=0)
        else:
            shifted = jnp.concatenate([jnp.zeros((step, w), x.dtype), x[:n - step]], axis=0)
        x = x + jnp.where(pos >= step, shifted, 0.0)
        step *= 2
    return x


def _mixer_kernel(x_ref, g1_ref, wqkvr_ref, wa_ref, wu_ref, wg_ref, walpha_ref, balpha_ref,
                  glag_ref, wglab_ref, poolw_ref, pscale_ref, wpoolb_ref, wout_ref, g2_ref,
                  wrt_ref, brt_ref,
                  x1_ref, h2_ref, eid_ref, gate_ref, rank_ref, cnt_ref,
                  state_s, carry_s, cnt_s, qdm_s, kem_s, kdt_s, v_s, dec_s, o_s, ypool_s, gates_s):
    ts = x_ref.shape[0]
    b_idx = pl.program_id(0)
    s_idx = pl.program_id(1)

    @pl.when(s_idx == 0)
    def _():
        state_s[...] = jnp.zeros_like(state_s)
        carry_s[...] = jnp.zeros_like(carry_s)

    @pl.when((b_idx == 0) & (s_idx == 0))
    def _():
        cnt_s[...] = jnp.zeros_like(cnt_s)

    xf = x_ref[...]
    ms = jnp.mean(xf * xf, axis=-1, keepdims=True)
    h = (xf * lax.rsqrt(ms + NORM_EPS) * g1_ref[...]).astype(BF16)

    qkvr = _dot(h, wqkvr_ref[...])
    q = qkvr[:, 0:KEY_W]
    k = qkvr[:, KEY_W:2 * KEY_W]
    v = qkvr[:, 2 * KEY_W:2 * KEY_W + VAL_W]
    r = qkvr[:, 2 * KEY_W + VAL_W:]

    a_low = _dot(h, wa_ref[...])
    z = _dot(a_low.astype(BF16), walpha_ref[...]) + balpha_ref[...]
    log_a = (jnp.minimum(z, 0.0) - jnp.log1p(jnp.exp(-jnp.abs(z)))) * (1.0 / GATE_NORMALIZER)

    u = _dot(h, wu_ref[...])
    ext = jnp.concatenate([carry_s[...], u], axis=0)
    carry_s[...] = u[ts - POOL_HALO:, :]
    pos = (s_idx * ts + lax.broadcasted_iota(jnp.int32, (ts, 1), 0)).astype(F32)
    mixed = []
    for gi, w in enumerate(POOL_WINDOWS):
        a = ext[:, gi * POOL_GW:(gi + 1) * POOL_GW]
        step = 1
        while step < w:
            a = a + pltpu.roll(a, step, axis=0)
            step *= 2
        pooled = a[POOL_HALO:, :] / jnp.minimum(pos + 1.0, float(w))
        diff = pooled - u[:, gi * POOL_GW:(gi + 1) * POOL_GW]
        mixed.append(_dot(diff.astype(BF16), poolw_ref[gi]))
    pm = (jnp.concatenate(mixed, axis=-1) * pscale_ref[...]).astype(BF16)
    ypool_s[...] = _dot(pm, wpoolb_ref[...])
    gates_s[...] = _dot(h, wg_ref[...])

    nc = ts // GLA_CHUNK
    b = _chunk_cumsum(log_a, GLA_CHUNK)
    b3 = b.reshape(nc, GLA_CHUNK, KEY_W)
    b_last = b3[:, GLA_CHUNK - 1:GLA_CHUNK, :]
    lane = lax.broadcasted_iota(jnp.int32, (ts, LANE), 1)
    qd = q * jnp.exp(b) * (GLA_DK ** -0.5)
    ke = (k.reshape(nc, GLA_CHUNK, KEY_W) * jnp.exp(b_last - b3)).reshape(ts, KEY_W)
    for hd in range(GLA_HEADS):
        pair = slice((hd // 2) * LANE, (hd // 2 + 1) * LANE)
        mine = (lane < GLA_DK) if hd % 2 == 0 else (lane >= GLA_DK)
        qdm_s[hd] = jnp.where(mine, qd[:, pair], 0.0).astype(BF16)
        kem_s[hd] = jnp.where(mine, ke[:, pair], 0.0).astype(BF16)
    v_s[...] = v.astype(BF16)
    grp = GLA_GROUP * GLA_CHUNK
    kd = k * jnp.exp(-b)
    for p in range(GLA_HEADS // 2):
        kd_t = kd[:, p * LANE:(p + 1) * LANE].T.astype(BF16)
        for j in range(ts // grp):
            kdt_s[p, j] = kd_t[:, j * grp:(j + 1) * grp]
    dec_rows = jnp.concatenate([jnp.exp(b_last[c]) for c in range(nc)]
                               + [jnp.zeros((LANE - nc, KEY_W), F32)], axis=0)
    dec_t = dec_rows.T
    for c in range(nc):
        for p in range(GLA_HEADS // 2):
            dec_s[c, p] = jnp.broadcast_to(dec_t[p * LANE:(p + 1) * LANE, c:c + 1], (LANE, GLA_DV))

    rg = lax.broadcasted_iota(jnp.int32, (grp, grp), 0)
    cg = lax.broadcasted_iota(jnp.int32, (grp, grp), 1)
    tri_g = (rg >= cg) & (rg // GLA_CHUNK == cg // GLA_CHUNK)

    def chunk_group_body(j, carry):
        rows_g = pl.ds(pl.multiple_of(j * grp, grp), grp)
        for hd in range(GLA_HEADS):
            p = hd // 2
            cols = slice(hd * GLA_DV, (hd + 1) * GLA_DV)
            qm = qdm_s[hd, rows_g, :]
            vh = v_s[rows_g, cols]
            km = kem_s[hd, rows_g, :]
            scores = jnp.where(tri_g, _dot(qm, kdt_s[p, j]), 0.0)
            o_intra = _dot(scores.astype(BF16), vh)
            st = state_s[hd]
            outs = []
            for cc in range(GLA_GROUP):
                c = GLA_GROUP * j + cc
                part = slice(cc * GLA_CHUNK, (cc + 1) * GLA_CHUNK)
                outs.append(o_intra[part] + _dot(qm[part], st.astype(BF16)))
                st = dec_s[c, p] * st + _dot_tn(km[part], vh[part])
            state_s[hd] = st
            o_s[rows_g, cols] = jnp.concatenate(outs, axis=0)
        return carry

    lax.fori_loop(0, nc // GLA_GROUP, chunk_group_body, 0)

    o_all = o_s[...]
    parts = []
    for hd in range(GLA_HEADS):
        oh = o_all[:, hd * GLA_DV:(hd + 1) * GLA_DV]
        parts.append(oh * lax.rsqrt(jnp.mean(oh * oh, axis=-1, keepdims=True) + NORM_EPS))
    o_n = jnp.concatenate(parts, axis=-1) * glag_ref[...]
    o_g = (o_n * (r * jax.nn.sigmoid(r))).astype(BF16)
    y_gla = _dot(o_g, wglab_ref[...])

    merged = (jax.nn.sigmoid(gates_s[:, :D_MODEL]) * y_gla
              + jax.nn.sigmoid(gates_s[:, D_MODEL:]) * ypool_s[...]).astype(BF16)
    x1 = xf + _dot(merged, wout_ref[...])
    x1_ref[...] = x1

    ms2 = jnp.mean(x1 * x1, axis=-1, keepdims=True)
    h2 = x1 * lax.rsqrt(ms2 + NORM_EPS) * g2_ref[...]
    _store_rows(h2_ref, _pack_rows(h2))
    lt = _dot_nt(wrt_ref[...], h2.astype(BF16)) + brt_ref[:, 0:1]

    l0, l1, l2, l3 = lt[0:1], lt[1:2], lt[2:3], lt[3:4]
    gm = jnp.maximum(jnp.maximum(l0, l1), jnp.maximum(l2, l3))
    gidx = jnp.where(l0 == gm, 0, jnp.where(l1 == gm, 1, jnp.where(l2 == gm, 2, 3)))
    gp = 1.0 / (jnp.exp(l0 - gm) + jnp.exp(l1 - gm) + jnp.exp(l2 - gm) + jnp.exp(l3 - gm))
    el = jnp.where(gidx == 0, lt[8:16],
                   jnp.where(gidx == 1, lt[16:24], jnp.where(gidx == 2, lt[24:32], lt[32:40])))
    row8 = lax.broadcasted_iota(jnp.int32, (EXPERTS_PER_GROUP, ts), 0)
    m1 = jnp.max(el, axis=0, keepdims=True)
    i1 = jnp.min(jnp.where(el == m1, row8, EXPERTS_PER_GROUP), axis=0, keepdims=True)
    el2 = jnp.where(row8 == i1, -jnp.inf, el)
    m2 = jnp.max(el2, axis=0, keepdims=True)
    i2 = jnp.min(jnp.where(el2 == m2, row8, EXPERTS_PER_GROUP), axis=0, keepdims=True)
    dd = jnp.exp(m2 - m1)
    p1 = 1.0 / (1.0 + dd)
    p2 = dd / (1.0 + dd)
    e1 = gidx * EXPERTS_PER_GROUP + i1
    e2 = gidx * EXPERTS_PER_GROUP + i2
    eid_ref[0:1, :] = e1
    eid_ref[1:2, :] = e2
    gate_ref[...] = jnp.concatenate([gp * p1, gp * p2, jnp.zeros((6, ts), F32)], axis=0)

    row32 = lax.broadcasted_iota(jnp.int32, (N_EXPERTS, ts), 0)
    oh1 = row32 == e1
    oh2 = row32 == e2
    member = jnp.where(oh1, 1.0, 0.0) + jnp.where(oh2, 1.0, 0.0)
    before = (lax.broadcasted_iota(jnp.int32, (ts, ts), 0)
              < lax.broadcasted_iota(jnp.int32, (ts, ts), 1))
    prefix = _dot(member.astype(BF16), jnp.where(before, 1.0, 0.0).astype(BF16))
    base = cnt_s[...]
    prefix = prefix + jnp.concatenate([base] * (ts // LANE), axis=1)
    rank_ref[0:1, :] = jnp.sum(jnp.where(oh1, prefix, 0.0), axis=0, keepdims=True).astype(jnp.int32)
    rank_ref[1:2, :] = jnp.sum(jnp.where(oh2, prefix, 0.0), axis=0, keepdims=True).astype(jnp.int32)
    new_cnt = base + jnp.sum(member, axis=1, keepdims=True)
    cnt_s[...] = new_cnt
    cnt_ref[...] = new_cnt


def _const_spec(shape):
    nd = len(shape)
    return pl.BlockSpec(shape, lambda b, s, _nd=nd: (0,) * _nd, pipeline_mode=pl.Buffered(1))


def _mixer_call(x, wts, ts):
    B, S, D = x.shape
    T = B * S
    ns = S // ts
    tok_spec = pl.BlockSpec((None, ts, D), lambda b, s: (b, s, 0))
    flat_tok = lambda rows: pl.BlockSpec((rows, ts), lambda b, s: (0, b * ns + s))
    in_specs = [tok_spec] + [_const_spec(w.shape) for w in wts]
    out_shape = (
        jax.ShapeDtypeStruct((B, S, D), F32),
        jax.ShapeDtypeStruct((B, S * ROW_TILES, LANE), jnp.int32),
        jax.ShapeDtypeStruct((2, T), jnp.int32),
        jax.ShapeDtypeStruct((8, T), F32),
        jax.ShapeDtypeStruct((2, T), jnp.int32),
        jax.ShapeDtypeStruct((N_EXPERTS, LANE), F32),
    )
    packed_spec = pl.BlockSpec((None, ts * ROW_TILES, LANE), lambda b, s: (b, s, 0))
    out_specs = (tok_spec, packed_spec, flat_tok(2), flat_tok(8), flat_tok(2),
                 pl.BlockSpec((N_EXPERTS, LANE), lambda b, s: (0, 0)))
    grp = GLA_GROUP * GLA_CHUNK
    scratch = [
        pltpu.VMEM((GLA_HEADS, LANE, GLA_DV), F32),
        pltpu.VMEM((POOL_HALO, POOL_W), F32),
        pltpu.VMEM((N_EXPERTS, LANE), F32),
        pltpu.VMEM((GLA_HEADS, ts, LANE), BF16),
        pltpu.VMEM((GLA_HEADS, ts, LANE), BF16),
        pltpu.VMEM((GLA_HEADS // 2, ts // grp, LANE, grp), BF16),
        pltpu.VMEM((ts, VAL_W), BF16),
        pltpu.VMEM((ts // GLA_CHUNK, GLA_HEADS // 2, LANE, GLA_DV), F32),
        pltpu.VMEM((ts, VAL_W), F32),
        pltpu.VMEM((ts, D), F32),
        pltpu.VMEM((ts, 2 * D), F32),
    ]
    return pl.pallas_call(
        _mixer_kernel,
        grid=(B, ns),
        in_specs=in_specs,
        out_specs=out_specs,
        out_shape=out_shape,
        scratch_shapes=scratch,
        compiler_params=pltpu.CompilerParams(
            dimension_semantics=("arbitrary", "arbitrary"), vmem_limit_bytes=VMEM_LIMIT),
        name="mixer",
    )(x, *wts)


_PAD_BITS = tuple(p for p in (1 << i for i in reversed(range((EXP_BM - 1).bit_length()))) if p >= SUBLANE)


def _dispatch_kernel(pst_ref, cnt_ref, dest_ref, h2_ref, xs_ref, zero_s, sem, zsem):
    td = h2_ref.shape[0]
    i = pl.program_id(0)

    def zero_copy(off, p):
        return pltpu.make_async_copy(zero_s.at[pl.ds(0, p)], xs_ref.at[pl.ds(off, p)], zsem)

    def for_each_pad_piece(e, fn):
        cnt = cnt_ref[e]
        start = pst_ref[e] + cnt
        end = start + (EXP_BM - cnt % EXP_BM) % EXP_BM
        aligned = jnp.minimum((start + SUBLANE - 1) // SUBLANE * SUBLANE, end)
        for j in range(SUBLANE - 1):
            @pl.when(start + j < aligned)
            def _(j=j):
                fn(zero_copy(start + j, 1))

        npad = end - aligned
        off = aligned
        for p in _PAD_BITS:
            hit = (npad & p) != 0

            @pl.when(hit)
            def _(off=off, p=p):
                fn(zero_copy(pl.multiple_of(off, SUBLANE), p))

            off = off + jnp.where(hit, p, 0)

    @pl.when(i == 0)
    def _():
        zero_s[...] = jnp.zeros_like(zero_s)

        def start_e(e, c):
            for_each_pad_piece(e, lambda cp: cp.start())
            return c

        def wait_e(e, c):
            for_each_pad_piece(e, lambda cp: cp.wait())
            return c

        lax.fori_loop(0, N_EXPERTS, start_e, 0)
        lax.fori_loop(0, N_EXPERTS, wait_e, 0)

        zrows = zero_s.shape[0]
        last_cnt = cnt_ref[N_EXPERTS - 1]
        used = pst_ref[N_EXPERTS - 1] + (last_cnt + EXP_BM - 1) // EXP_BM * EXP_BM
        first_piece = used // zrows
        n_pieces = xs_ref.shape[0] // zrows

        def tail_copy(t):
            return zero_copy(pl.multiple_of(t * zrows, zrows), zrows)

        lax.fori_loop(first_piece, n_pieces, lambda t, c: (tail_copy(t).start(), c)[1], 0)
        lax.fori_loop(first_piece, n_pieces, lambda t, c: (tail_copy(t).wait(), c)[1], 0)

    def issue(g, c):
        for u in range(SUBLANE):
            for kk in range(2):
                d = dest_ref[kk * td + g * SUBLANE + u]
                pltpu.make_async_copy(h2_ref.at[g * SUBLANE + u], xs_ref.at[d], sem).start(priority=kk)
        return c

    lax.fori_loop(0, td // SUBLANE, issue, 0)
    for kk in range(2):
        pltpu.make_async_copy(xs_ref.at[pl.ds(0, td)], xs_ref.at[pl.ds(td, td)], sem).wait()


def _dispatch_call(h2, dest3, pstarts, counts, n_rows):
    T, RT, _ = h2.shape
    nt, _, td2 = dest3.shape
    td = td2 // 2
    grid_spec = pltpu.PrefetchScalarGridSpec(
        num_scalar_prefetch=2,
        grid=(nt,),
        in_specs=[
            pl.BlockSpec((None, None, 2 * td), lambda i, *_: (i, 0, 0), memory_space=pltpu.SMEM),
            pl.BlockSpec((td, RT, LANE), lambda i, *_: (i, 0, 0)),
        ],
        out_specs=pl.BlockSpec(memory_space=pl.ANY),
        scratch_shapes=[
            pltpu.VMEM((_PAD_BITS[0], RT, LANE), h2.dtype),
            pltpu.SemaphoreType.DMA(()),
            pltpu.SemaphoreType.DMA(()),
        ],
    )
    return pl.pallas_call(
        _dispatch_kernel,
        grid_spec=grid_spec,
        out_shape=jax.ShapeDtypeStruct((n_rows, RT, LANE), h2.dtype),
        compiler_params=pltpu.CompilerParams(
            dimension_semantics=("arbitrary",), vmem_limit_bytes=VMEM_LIMIT, has_side_effects=True),
        name="dispatch",
    )(pstarts, counts, dest3, h2)


def _expert_kernel(be_ref, nu_ref, x_ref, wg_ref, wu_ref, wd_ref, y_ref):
    b = pl.program_id(0)

    @pl.when(b < nu_ref[0])
    def _():
        lo, hi = _unpack_rows(_load_rows(x_ref))
        xb = jnp.concatenate([lo.astype(BF16), hi.astype(BF16)], axis=1)
        g = _dot(xb, wg_ref[...])
        u = _dot(xb, wu_ref[...])
        hmid = (g * jax.nn.sigmoid(g) * u).astype(BF16)
        _store_rows(y_ref, _pack_rows(_dot(hmid, wd_ref[...])))

    @pl.when(b >= nu_ref[0])
    def _():
        y_ref[...] = jnp.zeros_like(y_ref)


def _expert_call(xs, block_e, n_used, wg, wu, wd):
    R = xs.shape[0] // ROW_TILES
    D = D_MODEL
    nb = R // EXP_BM
    blk = (EXP_BM * ROW_TILES, LANE)
    last = lambda b, nu: jnp.minimum(b, nu[0] - 1)
    grid_spec = pltpu.PrefetchScalarGridSpec(
        num_scalar_prefetch=2,
        grid=(nb,),
        in_specs=[
            pl.BlockSpec(blk, lambda b, be, nu: (last(b, nu), 0)),
            pl.BlockSpec((None, D, D_EXPERT), lambda b, be, nu: (be[last(b, nu)], 0, 0)),
            pl.BlockSpec((None, D, D_EXPERT), lambda b, be, nu: (be[last(b, nu)], 0, 0)),
            pl.BlockSpec((None, D_EXPERT, D), lambda b, be, nu: (be[last(b, nu)], 0, 0)),
        ],
        out_specs=pl.BlockSpec(blk, lambda b, be, nu: (b, 0)),
    )
    return pl.pallas_call(
        _expert_kernel,
        grid_spec=grid_spec,
        out_shape=jax.ShapeDtypeStruct(xs.shape, xs.dtype),
        compiler_params=pltpu.CompilerParams(
            dimension_semantics=("arbitrary",), vmem_limit_bytes=VMEM_LIMIT),
        name="experts",
    )(block_e, n_used, xs, wg, wu, wd)


def _combine_kernel(dcur_ref, dnext_ref, x1_ref, gate_ref, gf_ref, y_ref, out_ref, ybuf, sem):
    tf = x1_ref.shape[0]
    i = pl.program_id(0)
    n = pl.num_programs(0)
    tile_rows = SUBLANE * ROW_TILES

    def issue_all(dref, slot):
        def group(g, c):
            for u in range(SUBLANE):
                for kk in range(2):
                    d = dref[kk * tf + g * SUBLANE + u]
                    slab = pl.ds(pl.multiple_of(g * tile_rows, tile_rows) + u * ROW_TILES, ROW_TILES)
                    pltpu.make_async_copy(y_ref.at[d], ybuf.at[slot, kk, slab], sem.at[slot]).start(priority=kk)
            return c

        lax.fori_loop(0, tf // SUBLANE, group, 0)

    @pl.when(i == 0)
    def _():
        issue_all(dcur_ref, 0)

    for par in range(2):
        @pl.when((i % 2 == par) & (i + 1 < n))
        def _(par=par):
            issue_all(dnext_ref, 1 - par)

    slot = i % 2
    for kk in range(2):
        pltpu.make_async_copy(y_ref.at[pl.ds(0, tf)], y_ref.at[pl.ds(tf, tf)], sem.at[slot]).wait()

    g = jnp.concatenate([gate_ref[...], jnp.zeros((LANE - 8, tf), F32)], axis=0)
    gt = g.T
    w = ROW_TILES * LANE
    y0_lo, y0_hi = _unpack_rows(_load_rows(ybuf.at[slot, 0]))
    y1_lo, y1_hi = _unpack_rows(_load_rows(ybuf.at[slot, 1]))
    g0 = gt[:, 0:1]
    g1 = gt[:, 1:2]
    xo_lo = x1_ref[:, :w] + (g0 * y0_lo + g1 * y1_lo)
    xo_hi = x1_ref[:, w:] + (g0 * y0_hi + g1 * y1_hi)
    ms = (jnp.sum(xo_lo * xo_lo, axis=-1, keepdims=True)
          + jnp.sum(xo_hi * xo_hi, axis=-1, keepdims=True)) * (1.0 / (2 * w))
    scale = lax.rsqrt(ms + NORM_EPS)
    out_ref[:, :w] = xo_lo * scale * gf_ref[:, :w]
    out_ref[:, w:] = xo_hi * scale * gf_ref[:, w:]


def _combine_call(x1, dest3, gates, gf, ys):
    T, D = x1.shape
    nt, _, tf2 = dest3.shape
    tf = tf2 // 2
    return pl.pallas_call(
        _combine_kernel,
        grid=(nt,),
        in_specs=[
            pl.BlockSpec((None, None, 2 * tf), lambda i: (i, 0, 0), memory_space=pltpu.SMEM),
            pl.BlockSpec((None, None, 2 * tf), lambda i: (jnp.minimum(i + 1, nt - 1), 0, 0),
                         memory_space=pltpu.SMEM),
            pl.BlockSpec((tf, D), lambda i: (i, 0)),
            pl.BlockSpec((8, tf), lambda i: (0, i)),
            pl.BlockSpec((1, D), lambda i: (0, 0)),
            pl.BlockSpec(memory_space=pl.ANY),
        ],
        out_specs=pl.BlockSpec((tf, D), lambda i: (i, 0)),
        out_shape=jax.ShapeDtypeStruct((T, D), F32),
        scratch_shapes=[pltpu.VMEM((2, 2, tf * ROW_TILES, LANE), ys.dtype),
                        pltpu.SemaphoreType.DMA((2,))],
        compiler_params=pltpu.CompilerParams(
            dimension_semantics=("arbitrary",), vmem_limit_bytes=VMEM_LIMIT),
        name="combine",
    )(dest3, dest3, x1, gates, gf, ys)


def _pick_tile(n, want):
    t = min(want, n)
    while n % t:
        t //= 2
    return t


def _mixer_weights(norm1_g, w_in, w_alpha_up, b_alpha, gla_norm_g, w_gla_branch, pool_w, pool_scale,
                   w_pool_branch, w_out, norm2_g, w_rg, b_rg, w_re, b_re):
    c0 = 2 * KEY_W + 2 * VAL_W
    c1 = c0 + GATE_RANK
    c2 = c1 + POOL_W
    w_qkvr = w_in[:, :c0].astype(BF16)
    w_a = jnp.pad(w_in[:, c0:c1], ((0, 0), (0, LANE - GATE_RANK))).astype(BF16)
    w_u = w_in[:, c1:c2].astype(BF16)
    w_g = w_in[:, c2:].astype(BF16)
    w_alpha = jnp.pad(w_alpha_up, ((0, LANE - GATE_RANK), (0, 0))).astype(BF16)
    w_re_t = jnp.transpose(w_re, (0, 2, 1)).reshape(N_EXPERTS, D_MODEL)
    wrt = jnp.zeros((ROUTER_ROWS, D_MODEL), F32)
    wrt = wrt.at[0:N_GROUPS].set(w_rg.T).at[8:8 + N_EXPERTS].set(w_re_t).astype(BF16)
    brt = jnp.zeros((ROUTER_ROWS,), F32).at[0:N_GROUPS].set(b_rg).at[8:8 + N_EXPERTS].set(b_re.reshape(-1))
    brt = jnp.broadcast_to(brt[:, None], (ROUTER_ROWS, LANE))
    row = lambda a: a.reshape(1, -1).astype(F32)
    return (row(norm1_g), w_qkvr, w_a, w_u, w_g, w_alpha, row(b_alpha), row(gla_norm_g),
            w_gla_branch.astype(BF16), pool_w.astype(BF16), row(pool_scale), w_pool_branch.astype(BF16),
            w_out.astype(BF16), row(norm2_g), wrt, brt)


def kernel(x, norm1_g, w_in, w_alpha_up, b_alpha, gla_norm_g, w_gla_branch, pool_w, pool_scale,
           w_pool_branch, w_out, norm2_g, w_router_group, b_router_group, w_router_expert,
           b_router_expert, w_exp_gate, w_exp_up, w_exp_down, norm_f_g):
    B, S, D = x.shape
    T = B * S
    depth = w_in.shape[0]
    ts = _pick_tile(S, MIX_TS)
    tr = _pick_tile(T, ROW_TILE)
    n_assign = 2 * T
    n_blocks = -(-(n_assign + N_EXPERTS * (EXP_BM - 1)) // EXP_BM)
    n_rows = n_blocks * EXP_BM

    assert depth == 1, "kernel supports the problem's DEPTH=1"
    for l in range(depth):
        wts = _mixer_weights(norm1_g[l], w_in[l], w_alpha_up[l], b_alpha[l], gla_norm_g[l], w_gla_branch[l],
                             pool_w[l], pool_scale[l], w_pool_branch[l], w_out[l], norm2_g[l],
                             w_router_group[l], b_router_group[l], w_router_expert[l], b_router_expert[l])
        x1, h2, eid, gates, rank, cnt = _mixer_call(x, wts, ts)

        counts = cnt[:, 0].astype(jnp.int32)
        padded = ((counts + EXP_BM - 1) // EXP_BM) * EXP_BM
        pends = jnp.cumsum(padded)
        pstarts = pends - padded
        expert_col = jnp.arange(N_EXPERTS, dtype=jnp.int32)[:, None, None]
        dest = jnp.sum(jnp.where(eid[None] == expert_col, pstarts[:, None, None], 0), axis=0) + rank
        dest3 = dest.reshape(2, T // tr, tr).transpose(1, 0, 2).reshape(T // tr, 1, 2 * tr)
        blk_start = jnp.arange(n_blocks, dtype=jnp.int32) * EXP_BM
        block_e = jnp.minimum(jnp.sum(pends[None, :] <= blk_start[:, None], axis=1), N_EXPERTS - 1).astype(jnp.int32)
        n_used = (pends[-1:] // EXP_BM).astype(jnp.int32)

        xs = _dispatch_call(h2.reshape(T, ROW_TILES, LANE), dest3, pstarts.astype(jnp.int32), counts, n_rows)
        ys = _expert_call(xs.reshape(n_rows * ROW_TILES, LANE), block_e, n_used, w_exp_gate[l].astype(BF16),
                          w_exp_up[l].astype(BF16), w_exp_down[l].astype(BF16))
        out = _combine_call(x1.reshape(T, D), dest3, gates, norm_f_g.reshape(1, D).astype(F32),
                            ys.reshape(n_rows, ROW_TILES, LANE))
        x = out.reshape(B, S, D)
    return x
```

```python
import jax
import jax.numpy as jnp
from jax import lax
from jax.experimental import pallas as pl
from jax.experimental.pallas import tpu as pltpu

F32 = jnp.float32
BF16 = jnp.bfloat16

D_MODEL = 1024
GLA_HEADS = 4
GLA_DK = 64
GLA_DV = 128
KEY_W = GLA_HEADS * GLA_DK
VAL_W = GLA_HEADS * GLA_DV
GATE_RANK = 16
GATE_NORMALIZER = 16.0
GLA_CHUNK = 64
POOL_WINDOWS = (2, 4, 8, 16)
POOL_W = 512
POOL_GW = 128
N_GROUPS = 4
EXPERTS_PER_GROUP = 8
N_EXPERTS = 32
D_EXPERT = 256
NORM_EPS = 1e-6

LANE = 128
SUBLANE = 8
POOL_HALO = 16
ROUTER_ROWS = 48

MIX_TS = 512
ROW_TILE = 512
GLA_GROUP = 2
EXP_BM = 512
VMEM_LIMIT = 56 * 1024 * 1024


def _dot(a, b):
    return jnp.dot(a, b, preferred_element_type=F32)


def _dot_nt(a, b):
    return lax.dot_general(a, b, (((1,), (1,)), ((), ())), preferred_element_type=F32)


def _dot_tn(a, b):
    return lax.dot_general(a, b, (((0,), (0,)), ((), ())), preferred_element_type=F32)


def _bf16_bits(x):
    return lax.bitcast_convert_type(x.astype(BF16).astype(F32), jnp.int32)


def _pack_rows(x):
    w = x.shape[1] // 2
    return lax.shift_right_logical(_bf16_bits(x[:, :w]), 16) | _bf16_bits(x[:, w:])


def _unpack_rows(words):
    lo = lax.bitcast_convert_type(lax.shift_left(words, 16), F32)
    hi = lax.bitcast_convert_type(words & -65536, F32)
    return lo, hi


ROW_TILES = D_MODEL // 2 // LANE


def _store_rows(ref, words):
    n = words.shape[0]
    for t in range(ROW_TILES):
        ref[pl.ds(t, n, stride=ROW_TILES), :] = words[:, t * LANE:(t + 1) * LANE]


def _load_rows(ref):
    n = ref.shape[0] // ROW_TILES
    return jnp.concatenate([ref[pl.ds(t, n, stride=ROW_TILES), :] for t in range(ROW_TILES)], axis=1)


def _chunk_cumsum(x, chunk):
    n, w = x.shape
    pos = lax.broadcasted_iota(jnp.int32, (n, w), 0) % chunk
    step = 1
    while step < chunk:
        if step < 8:
            shifted = pltpu.roll(x, step, axis=0)
        else:
            shifted = jnp.concatenate([jnp.zeros((step, w), x.dtype), x[:n - step]], axis=0)
        x = x + jnp.where(pos >= step, shifted, 0.0)
        step *= 2
    return x


def _mixer_kernel(x_ref, g1_ref, wqkvr_ref, wa_ref, wu_ref, wg_ref, walpha_ref, balpha_ref,
                  glag_ref, wglab_ref, poolw_ref, pscale_ref, wpoolb_ref, wout_ref, g2_ref,
                  wrt_ref, brt_ref,
                  x1_ref, h2_ref, eid_ref, gate_ref, rank_ref, cnt_ref,
                  state_s, carry_s, cnt_s, qdm_s, kem_s, kdt_s, v_s, dec_s, o_s, ypool_s, gates_s):
    ts = x_ref.shape[0]
    b_idx = pl.program_id(0)
    s_idx = pl.program_id(1)

    @pl.when(s_idx == 0)
    def _():
        state_s[...] = jnp.zeros_like(state_s)
        carry_s[...] = jnp.zeros_like(carry_s)

    @pl.when((b_idx == 0) & (s_idx == 0))
    def _():
        cnt_s[...] = jnp.zeros_like(cnt_s)

    xf = x_ref[...]
    ms = jnp.mean(xf * xf, axis=-1, keepdims=True)
    h = (xf * lax.rsqrt(ms + NORM_EPS) * g1_ref[...]).astype(BF16)

    qkvr = _dot(h, wqkvr_ref[...])
    q = qkvr[:, 0:KEY_W]
    k = qkvr[:, KEY_W:2 * KEY_W]
    v = qkvr[:, 2 * KEY_W:2 * KEY_W + VAL_W]
    r = qkvr[:, 2 * KEY_W + VAL_W:]

    a_low = _dot(h, wa_ref[...])
    z = _dot(a_low.astype(BF16), walpha_ref[...]) + balpha_ref[...]
    log_a = (jnp.minimum(z, 0.0) - jnp.log1p(jnp.exp(-jnp.abs(z)))) * (1.0 / GATE_NORMALIZER)

    u = _dot(h, wu_ref[...])
    ext = jnp.concatenate([carry_s[...], u], axis=0)
    carry_s[...] = u[ts - POOL_HALO:, :]
    pos = (s_idx * ts + lax.broadcasted_iota(jnp.int32, (ts, 1), 0)).astype(F32)
    mixed = []
    for gi, w in enumerate(POOL_WINDOWS):
        a = ext[:, gi * POOL_GW:(gi + 1) * POOL_GW]
        step = 1
        while step < w:
            a = a + pltpu.roll(a, step, axis=0)
            step *= 2
        pooled = a[POOL_HALO:, :] / jnp.minimum(pos + 1.0, float(w))
        diff = pooled - u[:, gi * POOL_GW:(gi + 1) * POOL_GW]
        mixed.append(_dot(diff.astype(BF16), poolw_ref[gi]))
    pm = (jnp.concatenate(mixed, axis=-1) * pscale_ref[...]).astype(BF16)
    ypool_s[...] = _dot(pm, wpoolb_ref[...])
    gates_s[...] = _dot(h, wg_ref[...])

    nc = ts // GLA_CHUNK
    b = _chunk_cumsum(log_a, GLA_CHUNK)
    b3 = b.reshape(nc, GLA_CHUNK, KEY_W)
    b_last = b3[:, GLA_CHUNK - 1:GLA_CHUNK, :]
    lane = lax.broadcasted_iota(jnp.int32, (ts, LANE), 1)
    qd = q * jnp.exp(b) * (GLA_DK ** -0.5)
    ke = (k.reshape(nc, GLA_CHUNK, KEY_W) * jnp.exp(b_last - b3)).reshape(ts, KEY_W)
    for hd in range(GLA_HEADS):
        pair = slice((hd // 2) * LANE, (hd // 2 + 1) * LANE)
        mine = (lane < GLA_DK) if hd % 2 == 0 else (lane >= GLA_DK)
        qdm_s[hd] = jnp.where(mine, qd[:, pair], 0.0).astype(BF16)
        kem_s[hd] = jnp.where(mine, ke[:, pair], 0.0).astype(BF16)
    v_s[...] = v.astype(BF16)
    grp = GLA_GROUP * GLA_CHUNK
    kd = k * jnp.exp(-b)
    for p in range(GLA_HEADS // 2):
        kd_t = kd[:, p * LANE:(p + 1) * LANE].T.astype(BF16)
        for j in range(ts // grp):
            kdt_s[p, j] = kd_t[:, j * grp:(j + 1) * grp]
    dec_rows = jnp.concatenate([jnp.exp(b_last[c]) for c in range(nc)]
                               + [jnp.zeros((LANE - nc, KEY_W), F32)], axis=0)
    dec_t = dec_rows.T
    for c in range(nc):
        for p in range(GLA_HEADS // 2):
            dec_s[c, p] = jnp.broadcast_to(dec_t[p * LANE:(p + 1) * LANE, c:c + 1], (LANE, GLA_DV))

    rg = lax.broadcasted_iota(jnp.int32, (grp, grp), 0)
    cg = lax.broadcasted_iota(jnp.int32, (grp, grp), 1)
    tri_g = (rg >= cg) & (rg // GLA_CHUNK == cg // GLA_CHUNK)

    for j in range(nc // GLA_GROUP):
        rows_g = slice(j * grp, (j + 1) * grp)
        for hd in range(GLA_HEADS):
            p = hd // 2
            cols = slice(hd * GLA_DV, (hd + 1) * GLA_DV)
            qm = qdm_s[hd, rows_g, :]
            vh = v_s[rows_g, cols]
            km = kem_s[hd, rows_g, :]
            scores = jnp.where(tri_g, _dot(qm, kdt_s[p, j]), 0.0)
            o_intra = _dot(scores.astype(BF16), vh)
            st = state_s[hd]
            outs = []
            for cc in range(GLA_GROUP):
                c = GLA_GROUP * j + cc
                part = slice(cc * GLA_CHUNK, (cc + 1) * GLA_CHUNK)
                outs.append(o_intra[part] + _dot(qm[part], st.astype(BF16)))
                st = dec_s[c, p] * st + _dot_tn(km[part], vh[part])
            state_s[hd] = st
            o_s[rows_g, cols] = jnp.concatenate(outs, axis=0)

    o_all = o_s[...]
    parts = []
    for hd in range(GLA_HEADS):
        oh = o_all[:, hd * GLA_DV:(hd + 1) * GLA_DV]
        parts.append(oh * lax.rsqrt(jnp.mean(oh * oh, axis=-1, keepdims=True) + NORM_EPS))
    o_n = jnp.concatenate(parts, axis=-1) * glag_ref[...]
    o_g = (o_n * (r * jax.nn.sigmoid(r))).astype(BF16)
    y_gla = _dot(o_g, wglab_ref[...])

    merged = (jax.nn.sigmoid(gates_s[:, :D_MODEL]) * y_gla
              + jax.nn.sigmoid(gates_s[:, D_MODEL:]) * ypool_s[...]).astype(BF16)
    x1 = xf + _dot(merged, wout_ref[...])
    x1_ref[...] = x1

    ms2 = jnp.mean(x1 * x1, axis=-1, keepdims=True)
    h2 = x1 * lax.rsqrt(ms2 + NORM_EPS) * g2_ref[...]
    _store_rows(h2_ref, _pack_rows(h2))
    lt = _dot_nt(wrt_ref[...], h2.astype(BF16)) + brt_ref[:, 0:1]

    l0, l1, l2, l3 = lt[0:1], lt[1:2], lt[2:3], lt[3:4]
    gm = jnp.maximum(jnp.maximum(l0, l1), jnp.maximum(l2, l3))
    gidx = jnp.where(l0 == gm, 0, jnp.where(l1 == gm, 1, jnp.where(l2 == gm, 2, 3)))
    gp = 1.0 / (jnp.exp(l0 - gm) + jnp.exp(l1 - gm) + jnp.exp(l2 - gm) + jnp.exp(l3 - gm))
    el = jnp.where(gidx == 0, lt[8:16],
                   jnp.where(gidx == 1, lt[16:24], jnp.where(gidx == 2, lt[24:32], lt[32:40])))
    row8 = lax.broadcasted_iota(jnp.int32, (EXPERTS_PER_GROUP, ts), 0)
    m1 = jnp.max(el, axis=0, keepdims=True)
    i1 = jnp.min(jnp.where(el == m1, row8, EXPERTS_PER_GROUP), axis=0, keepdims=True)
    el2 = jnp.where(row8 == i1, -jnp.inf, el)
    m2 = jnp.max(el2, axis=0, keepdims=True)
    i2 = jnp.min(jnp.where(el2 == m2, row8, EXPERTS_PER_GROUP), axis=0, keepdims=True)
    dd = jnp.exp(m2 - m1)
    p1 = 1.0 / (1.0 + dd)
    p2 = dd / (1.0 + dd)
    e1 = gidx * EXPERTS_PER_GROUP + i1
    e2 = gidx * EXPERTS_PER_GROUP + i2
    eid_ref[0:1, :] = e1
    eid_ref[1:2, :] = e2
    gate_ref[...] = jnp.concatenate([gp * p1, gp * p2, jnp.zeros((6, ts), F32)], axis=0)

    row32 = lax.broadcasted_iota(jnp.int32, (N_EXPERTS, ts), 0)
    oh1 = row32 == e1
    oh2 = row32 == e2
    member = jnp.where(oh1, 1.0, 0.0) + jnp.where(oh2, 1.0, 0.0)
    before = (lax.broadcasted_iota(jnp.int32, (ts, ts), 0)
              < lax.broadcasted_iota(jnp.int32, (ts, ts), 1))
    prefix = _dot(member.astype(BF16), jnp.where(before, 1.0, 0.0).astype(BF16))
    base = cnt_s[...]
    prefix = prefix + jnp.concatenate([base] * (ts // LANE), axis=1)
    rank_ref[0:1, :] = jnp.sum(jnp.where(oh1, prefix, 0.0), axis=0, keepdims=True).astype(jnp.int32)
    rank_ref[1:2, :] = jnp.sum(jnp.where(oh2, prefix, 0.0), axis=0, keepdims=True).astype(jnp.int32)
    new_cnt = base + jnp.sum(member, axis=1, keepdims=True)
    cnt_s[...] = new_cnt
    cnt_ref[...] = new_cnt


def _const_spec(shape):
    nd = len(shape)
    return pl.BlockSpec(shape, lambda b, s, _nd=nd: (0,) * _nd, pipeline_mode=pl.Buffered(1))


def _mixer_call(x, wts, ts):
    B, S, D = x.shape
    T = B * S
    ns = S // ts
    tok_spec = pl.BlockSpec((None, ts, D), lambda b, s: (b, s, 0))
    flat_tok = lambda rows: pl.BlockSpec((rows, ts), lambda b, s: (0, b * ns + s))
    in_specs = [tok_spec] + [_const_spec(w.shape) for w in wts]
    out_shape = (
        jax.ShapeDtypeStruct((B, S, D), F32),
        jax.ShapeDtypeStruct((B, S * ROW_TILES, LANE), jnp.int32),
        jax.ShapeDtypeStruct((2, T), jnp.int32),
        jax.ShapeDtypeStruct((8, T), F32),
        jax.ShapeDtypeStruct((2, T), jnp.int32),
        jax.ShapeDtypeStruct((N_EXPERTS, LANE), F32),
    )
    packed_spec = pl.BlockSpec((None, ts * ROW_TILES, LANE), lambda b, s: (b, s, 0))
    out_specs = (tok_spec, packed_spec, flat_tok(2), flat_tok(8), flat_tok(2),
                 pl.BlockSpec((N_EXPERTS, LANE), lambda b, s: (0, 0)))
    grp = GLA_GROUP * GLA_CHUNK
    scratch = [
        pltpu.VMEM((GLA_HEADS, LANE, GLA_DV), F32),
        pltpu.VMEM((POOL_HALO, POOL_W), F32),
        pltpu.VMEM((N_EXPERTS, LANE), F32),
        pltpu.VMEM((GLA_HEADS, ts, LANE), BF16),
        pltpu.VMEM((GLA_HEADS, ts, LANE), BF16),
        pltpu.VMEM((GLA_HEADS // 2, ts // grp, LANE, grp), BF16),
        pltpu.VMEM((ts, VAL_W), BF16),
        pltpu.VMEM((ts // GLA_CHUNK, GLA_HEADS // 2, LANE, GLA_DV), F32),
        pltpu.VMEM((ts, VAL_W), F32),
        pltpu.VMEM((ts, D), F32),
        pltpu.VMEM((ts, 2 * D), F32),
    ]
    return pl.pallas_call(
        _mixer_kernel,
        grid=(B, ns),
        in_specs=in_specs,
        out_specs=out_specs,
        out_shape=out_shape,
        scratch_shapes=scratch,
        compiler_params=pltpu.CompilerParams(
            dimension_semantics=("arbitrary", "arbitrary"), vmem_limit_bytes=VMEM_LIMIT),
        name="mixer",
    )(x, *wts)


_PAD_BITS = tuple(p for p in (1 << i for i in reversed(range((EXP_BM - 1).bit_length()))) if p >= SUBLANE)


def _dispatch_kernel(pst_ref, cnt_ref, dest_ref, h2_ref, xs_ref, zero_s, sem, zsem):
    td = h2_ref.shape[0]
    i = pl.program_id(0)

    def zero_copy(off, p):
        return pltpu.make_async_copy(zero_s.at[pl.ds(0, p)], xs_ref.at[pl.ds(off, p)], zsem)

    def for_each_pad_piece(e, fn):
        cnt = cnt_ref[e]
        start = pst_ref[e] + cnt
        end = start + (EXP_BM - cnt % EXP_BM) % EXP_BM
        aligned = jnp.minimum((start + SUBLANE - 1) // SUBLANE * SUBLANE, end)
        for j in range(SUBLANE - 1):
            @pl.when(start + j < aligned)
            def _(j=j):
                fn(zero_copy(start + j, 1))

        npad = end - aligned
        off = aligned
        for p in _PAD_BITS:
            hit = (npad & p) != 0

            @pl.when(hit)
            def _(off=off, p=p):
                fn(zero_copy(pl.multiple_of(off, SUBLANE), p))

            off = off + jnp.where(hit, p, 0)

    @pl.when(i == 0)
    def _():
        zero_s[...] = jnp.zeros_like(zero_s)

        def start_e(e, c):
            for_each_pad_piece(e, lambda cp: cp.start())
            return c

        def wait_e(e, c):
            for_each_pad_piece(e, lambda cp: cp.wait())
            return c

        lax.fori_loop(0, N_EXPERTS, start_e, 0)
        lax.fori_loop(0, N_EXPERTS, wait_e, 0)

        zrows = zero_s.shape[0]
        last_cnt = cnt_ref[N_EXPERTS - 1]
        used = pst_ref[N_EXPERTS - 1] + (last_cnt + EXP_BM - 1) // EXP_BM * EXP_BM
        first_piece = used // zrows
        n_pieces = xs_ref.shape[0] // zrows

        def tail_copy(t):
            return zero_copy(pl.multiple_of(t * zrows, zrows), zrows)

        lax.fori_loop(first_piece, n_pieces, lambda t, c: (tail_copy(t).start(), c)[1], 0)
        lax.fori_loop(first_piece, n_pieces, lambda t, c: (tail_copy(t).wait(), c)[1], 0)

    def issue(g, c):
        for u in range(SUBLANE):
            for kk in range(2):
                d = dest_ref[kk * td + g * SUBLANE + u]
                pltpu.make_async_copy(h2_ref.at[g * SUBLANE + u], xs_ref.at[d], sem).start(priority=kk)
        return c

    lax.fori_loop(0, td // SUBLANE, issue, 0)
    for kk in range(2):
        pltpu.make_async_copy(xs_ref.at[pl.ds(0, td)], xs_ref.at[pl.ds(td, td)], sem).wait()


def _dispatch_call(h2, dest3, pstarts, counts, n_rows):
    T, RT, _ = h2.shape
    nt, _, td2 = dest3.shape
    td = td2 // 2
    grid_spec = pltpu.PrefetchScalarGridSpec(
        num_scalar_prefetch=2,
        grid=(nt,),
        in_specs=[
            pl.BlockSpec((None, None, 2 * td), lambda i, *_: (i, 0, 0), memory_space=pltpu.SMEM),
            pl.BlockSpec((td, RT, LANE), lambda i, *_: (i, 0, 0)),
        ],
        out_specs=pl.BlockSpec(memory_space=pl.ANY),
        scratch_shapes=[
            pltpu.VMEM((_PAD_BITS[0], RT, LANE), h2.dtype),
            pltpu.SemaphoreType.DMA(()),
            pltpu.SemaphoreType.DMA(()),
        ],
    )
    return pl.pallas_call(
        _dispatch_kernel,
        grid_spec=grid_spec,
        out_shape=jax.ShapeDtypeStruct((n_rows, RT, LANE), h2.dtype),
        compiler_params=pltpu.CompilerParams(
            dimension_semantics=("arbitrary",), vmem_limit_bytes=VMEM_LIMIT, has_side_effects=True),
        name="dispatch",
    )(pstarts, counts, dest3, h2)


def _expert_kernel(be_ref, nu_ref, x_ref, wg_ref, wu_ref, wd_ref, y_ref):
    b = pl.program_id(0)

    @pl.when(b < nu_ref[0])
    def _():
        lo, hi = _unpack_rows(_load_rows(x_ref))
        xb = jnp.concatenate([lo.astype(BF16), hi.astype(BF16)], axis=1)
        g = _dot(xb, wg_ref[...])
        u = _dot(xb, wu_ref[...])
        hmid = (g * jax.nn.sigmoid(g) * u).astype(BF16)
        _store_rows(y_ref, _pack_rows(_dot(hmid, wd_ref[...])))

    @pl.when(b >= nu_ref[0])
    def _():
        y_ref[...] = jnp.zeros_like(y_ref)


def _expert_call(xs, block_e, n_used, wg, wu, wd):
    R = xs.shape[0] // ROW_TILES
    D = D_MODEL
    nb = R // EXP_BM
    blk = (EXP_BM * ROW_TILES, LANE)
    last = lambda b, nu: jnp.minimum(b, nu[0] - 1)
    grid_spec = pltpu.PrefetchScalarGridSpec(
        num_scalar_prefetch=2,
        grid=(nb,),
        in_specs=[
            pl.BlockSpec(blk, lambda b, be, nu: (last(b, nu), 0)),
            pl.BlockSpec((None, D, D_EXPERT), lambda b, be, nu: (be[last(b, nu)], 0, 0)),
            pl.BlockSpec((None, D, D_EXPERT), lambda b, be, nu: (be[last(b, nu)], 0, 0)),
            pl.BlockSpec((None, D_EXPERT, D), lambda b, be, nu: (be[last(b, nu)], 0, 0)),
        ],
        out_specs=pl.BlockSpec(blk, lambda b, be, nu: (b, 0)),
    )
    return pl.pallas_call(
        _expert_kernel,
        grid_spec=grid_spec,
        out_shape=jax.ShapeDtypeStruct(xs.shape, xs.dtype),
        compiler_params=pltpu.CompilerParams(
            dimension_semantics=("arbitrary",), vmem_limit_bytes=VMEM_LIMIT),
        name="experts",
    )(block_e, n_used, xs, wg, wu, wd)


def _combine_kernel(dcur_ref, dnext_ref, x1_ref, gate_ref, gf_ref, y_ref, out_ref, ybuf, sem):
    tf = x1_ref.shape[0]
    i = pl.program_id(0)
    n = pl.num_programs(0)
    tile_rows = SUBLANE * ROW_TILES

    def issue_all(dref, slot):
        def group(g, c):
            for u in range(SUBLANE):
                for kk in range(2):
                    d = dref[kk * tf + g * SUBLANE + u]
                    slab = pl.ds(pl.multiple_of(g * tile_rows, tile_rows) + u * ROW_TILES, ROW_TILES)
                    pltpu.make_async_copy(y_ref.at[d], ybuf.at[slot, kk, slab], sem.at[slot]).start(priority=kk)
            return c

        lax.fori_loop(0, tf // SUBLANE, group, 0)

    @pl.when(i == 0)
    def _():
        issue_all(dcur_ref, 0)

    for par in range(2):
        @pl.when((i % 2 == par) & (i + 1 < n))
        def _(par=par):
            issue_all(dnext_ref, 1 - par)

    slot = i % 2
    for kk in range(2):
        pltpu.make_async_copy(y_ref.at[pl.ds(0, tf)], y_ref.at[pl.ds(tf, tf)], sem.at[slot]).wait()

    g = jnp.concatenate([gate_ref[...], jnp.zeros((LANE - 8, tf), F32)], axis=0)
    gt = g.T
    w = ROW_TILES * LANE
    y0_lo, y0_hi = _unpack_rows(_load_rows(ybuf.at[slot, 0]))
    y1_lo, y1_hi = _unpack_rows(_load_rows(ybuf.at[slot, 1]))
    g0 = gt[:, 0:1]
    g1 = gt[:, 1:2]
    xo_lo = x1_ref[:, :w] + (g0 * y0_lo + g1 * y1_lo)
    xo_hi = x1_ref[:, w:] + (g0 * y0_hi + g1 * y1_hi)
    ms = (jnp.sum(xo_lo * xo_lo, axis=-1, keepdims=True)
          + jnp.sum(xo_hi * xo_hi, axis=-1, keepdims=True)) * (1.0 / (2 * w))
    scale = lax.rsqrt(ms + NORM_EPS)
    out_ref[:, :w] = xo_lo * scale * gf_ref[:, :w]
    out_ref[:, w:] = xo_hi * scale * gf_ref[:, w:]


def _combine_call(x1, dest3, gates, gf, ys):
    T, D = x1.shape
    nt, _, tf2 = dest3.shape
    tf = tf2 // 2
    return pl.pallas_call(
        _combine_kernel,
        grid=(nt,),
        in_specs=[
            pl.BlockSpec((None, None, 2 * tf), lambda i: (i, 0, 0), memory_space=pltpu.SMEM),
            pl.BlockSpec((None, None, 2 * tf), lambda i: (jnp.minimum(i + 1, nt - 1), 0, 0),
                         memory_space=pltpu.SMEM),
            pl.BlockSpec((tf, D), lambda i: (i, 0)),
            pl.BlockSpec((8, tf), lambda i: (0, i)),
            pl.BlockSpec((1, D), lambda i: (0, 0)),
            pl.BlockSpec(memory_space=pl.ANY),
        ],
        out_specs=pl.BlockSpec((tf, D), lambda i: (i, 0)),
        out_shape=jax.ShapeDtypeStruct((T, D), F32),
        scratch_shapes=[pltpu.VMEM((2, 2, tf * ROW_TILES, LANE), ys.dtype),
                        pltpu.SemaphoreType.DMA((2,))],
        compiler_params=pltpu.CompilerParams(
            dimension_semantics=("arbitrary",), vmem_limit_bytes=VMEM_LIMIT),
        name="combine",
    )(dest3, dest3, x1, gates, gf, ys)


def _pick_tile(n, want):
    t = min(want, n)
    while n % t:
        t //= 2
    return t


def _mixer_weights(norm1_g, w_in, w_alpha_up, b_alpha, gla_norm_g, w_gla_branch, pool_w, pool_scale,
                   w_pool_branch, w_out, norm2_g, w_rg, b_rg, w_re, b_re):
    c0 = 2 * KEY_W + 2 * VAL_W
    c1 = c0 + GATE_RANK
    c2 = c1 + POOL_W
    w_qkvr = w_in[:, :c0].astype(BF16)
    w_a = jnp.pad(w_in[:, c0:c1], ((0, 0), (0, LANE - GATE_RANK))).astype(BF16)
    w_u = w_in[:, c1:c2].astype(BF16)
    w_g = w_in[:, c2:].astype(BF16)
    w_alpha = jnp.pad(w_alpha_up, ((0, LANE - GATE_RANK), (0, 0))).astype(BF16)
    w_re_t = jnp.transpose(w_re, (0, 2, 1)).reshape(N_EXPERTS, D_MODEL)
    wrt = jnp.zeros((ROUTER_ROWS, D_MODEL), F32)
    wrt = wrt.at[0:N_GROUPS].set(w_rg.T).at[8:8 + N_EXPERTS].set(w_re_t).astype(BF16)
    brt = jnp.zeros((ROUTER_ROWS,), F32).at[0:N_GROUPS].set(b_rg).at[8:8 + N_EXPERTS].set(b_re.reshape(-1))
    brt = jnp.broadcast_to(brt[:, None], (ROUTER_ROWS, LANE))
    row = lambda a: a.reshape(1, -1).astype(F32)
    return (row(norm1_g), w_qkvr, w_a, w_u, w_g, w_alpha, row(b_alpha), row(gla_norm_g),
            w_gla_branch.astype(BF16), pool_w.astype(BF16), row(pool_scale), w_pool_branch.astype(BF16),
            w_out.astype(BF16), row(norm2_g), wrt, brt)


def kernel(x, norm1_g, w_in, w_alpha_up, b_alpha, gla_norm_g, w_gla_branch, pool_w, pool_scale,
           w_pool_branch, w_out, norm2_g, w_router_group, b_router_group, w_router_expert,
           b_router_expert, w_exp_gate, w_exp_up, w_exp_down, norm_f_g):
    B, S, D = x.shape
    T = B * S
    depth = w_in.shape[0]
    ts = _pick_tile(S, MIX_TS)
    tr = _pick_tile(T, ROW_TILE)
    n_assign = 2 * T
    n_blocks = -(-(n_assign + N_EXPERTS * (EXP_BM - 1)) // EXP_BM)
    n_rows = n_blocks * EXP_BM

    assert depth == 1, "kernel supports the problem's DEPTH=1"
    for l in range(depth):
        wts = _mixer_weights(norm1_g[l], w_in[l], w_alpha_up[l], b_alpha[l], gla_norm_g[l], w_gla_branch[l],
                             pool_w[l], pool_scale[l], w_pool_branch[l], w_out[l], norm2_g[l],
                             w_router_group[l], b_router_group[l], w_router_expert[l], b_router_expert[l])
        x1, h2, eid, gates, rank, cnt = _mixer_call(x, wts, ts)

        counts = cnt[:, 0].astype(jnp.int32)
        padded = ((counts + EXP_BM - 1) // EXP_BM) * EXP_BM
        pends = jnp.cumsum(padded)
        pstarts = pends - padded
        expert_col = jnp.arange(N_EXPERTS, dtype=jnp.int32)[:, None, None]
        dest = jnp.sum(jnp.where(eid[None] == expert_col, pstarts[:, None, None], 0), axis=0) + rank
        dest3 = dest.reshape(2, T // tr, tr).transpose(1, 0, 2).reshape(T // tr, 1, 2 * tr)
        blk_start = jnp.arange(n_blocks, dtype=jnp.int32) * EXP_BM
        block_e = jnp.minimum(jnp.sum(pends[None, :] <= blk_start[:, None], axis=1), N_EXPERTS - 1).astype(jnp.int32)
        n_used = (pends[-1:] // EXP_BM).astype(jnp.int32)

        xs = _dispatch_call(h2.reshape(T, ROW_TILES, LANE), dest3, pstarts.astype(jnp.int32), counts, n_rows)
        ys = _expert_call(xs.reshape(n_rows * ROW_TILES, LANE), block_e, n_used, w_exp_gate[l].astype(BF16),
                          w_exp_up[l].astype(BF16), w_exp_down[l].astype(BF16))
        out = _combine_call(x1.reshape(T, D), dest3, gates, norm_f_g.reshape(1, D).astype(F32),
                            ys.reshape(n_rows, ROW_TILES, LANE))
        x = out.reshape(B, S, D)
    return x
```

```python
import jax
import jax.numpy as jnp
from jax import lax
from jax.experimental import pallas as pl
from jax.experimental.pallas import tpu as pltpu

F32 = jnp.float32
BF16 = jnp.bfloat16

D_MODEL = 1024
GLA_HEADS = 4
GLA_DK = 64
GLA_DV = 128
KEY_W = GLA_HEADS * GLA_DK
VAL_W = GLA_HEADS * GLA_DV
GATE_RANK = 16
GATE_NORMALIZER = 16.0
GLA_CHUNK = 64
POOL_WINDOWS = (2, 4, 8, 16)
POOL_W = 512
POOL_GW = 128
N_GROUPS = 4
EXPERTS_PER_GROUP = 8
N_EXPERTS = 32
D_EXPERT = 256
NORM_EPS = 1e-6

LANE = 128
SUBLANE = 8
POOL_HALO = 16
ROUTER_ROWS = 48

MIX_TS = 512
ROW_TILE = 1024
GLA_GROUP = 2
EXP_BM = 512
VMEM_LIMIT = 56 * 1024 * 1024


def _dot(a, b):
    return jnp.dot(a, b, preferred_element_type=F32)


def _dot_nt(a, b):
    return lax.dot_general(a, b, (((1,), (1,)), ((), ())), preferred_element_type=F32)


def _dot_tn(a, b):
    return lax.dot_general(a, b, (((0,), (0,)), ((), ())), preferred_element_type=F32)


def _bf16_bits(x):
    return lax.bitcast_convert_type(x.astype(BF16).astype(F32), jnp.int32)


def _pack_rows(x):
    w = x.shape[1] // 2
    return lax.shift_right_logical(_bf16_bits(x[:, :w]), 16) | _bf16_bits(x[:, w:])


def _unpack_rows(words):
    lo = lax.bitcast_convert_type(lax.shift_left(words, 16), F32)
    hi = lax.bitcast_convert_type(words & -65536, F32)
    return lo, hi


ROW_TILES = D_MODEL // 2 // LANE


def _store_rows(ref, words):
    n = words.shape[0]
    for t in range(ROW_TILES):
        ref[pl.ds(t, n, stride=ROW_TILES), :] = words[:, t * LANE:(t + 1) * LANE]


def _load_rows(ref):
    n = ref.shape[0] // ROW_TILES
    return jnp.concatenate([ref[pl.ds(t, n, stride=ROW_TILES), :] for t in range(ROW_TILES)], axis=1)


def _chunk_cumsum(x, chunk):
    n, w = x.shape
    pos = lax.broadcasted_iota(jnp.int32, (n, w), 0) % chunk
    step = 1
    while step < chunk:
        if step < 8:
            shifted = pltpu.roll(x, step, axis=0)
        else:
            shifted = jnp.concatenate([jnp.zeros((step, w), x.dtype), x[:n - step]], axis=0)
        x = x + jnp.where(pos >= step, shifted, 0.0)
        step *= 2
    return x


def _mixer_kernel(x_ref, g1_ref, wqkvr_ref, wa_ref, wu_ref, wg_ref, walpha_ref, balpha_ref,
                  glag_ref, wglab_ref, poolw_ref, pscale_ref, wpoolb_ref, wout_ref, g2_ref,
                  wrt_ref, brt_ref,
                  x1_ref, h2_ref, eid_ref, gate_ref, rank_ref, cnt_ref,
                  state_s, carry_s, cnt_s, qdm_s, kem_s, kdt_s, v_s, dec_s, o_s, ypool_s, gates_s):
    ts = x_ref.shape[0]
    b_idx = pl.program_id(0)
    s_idx = pl.program_id(1)

    @pl.when(s_idx == 0)
    def _():
        state_s[...] = jnp.zeros_like(state_s)
        carry_s[...] = jnp.zeros_like(carry_s)

    @pl.when((b_idx == 0) & (s_idx == 0))
    def _():
        cnt_s[...] = jnp.zeros_like(cnt_s)

    xf = x_ref[...]
    ms = jnp.mean(xf * xf, axis=-1, keepdims=True)
    h = (xf * lax.rsqrt(ms + NORM_EPS) * g1_ref[...]).astype(BF16)

    qkvr = _dot(h, wqkvr_ref[...])
    q = qkvr[:, 0:KEY_W]
    k = qkvr[:, KEY_W:2 * KEY_W]
    v = qkvr[:, 2 * KEY_W:2 * KEY_W + VAL_W]
    r = qkvr[:, 2 * KEY_W + VAL_W:]

    a_low = _dot(h, wa_ref[...])
    z = _dot(a_low.astype(BF16), walpha_ref[...]) + balpha_ref[...]
    log_a = (jnp.minimum(z, 0.0) - jnp.log1p(jnp.exp(-jnp.abs(z)))) * (1.0 / GATE_NORMALIZER)

    u = _dot(h, wu_ref[...])
    ext = jnp.concatenate([carry_s[...], u], axis=0)
    carry_s[...] = u[ts - POOL_HALO:, :]
    pos = (s_idx * ts + lax.broadcasted_iota(jnp.int32, (ts, 1), 0)).astype(F32)
    mixed = []
    for gi, w in enumerate(POOL_WINDOWS):
        a = ext[:, gi * POOL_GW:(gi + 1) * POOL_GW]
        step = 1
        while step < w:
            a = a + pltpu.roll(a, step, axis=0)
            step *= 2
        pooled = a[POOL_HALO:, :] / jnp.minimum(pos + 1.0, float(w))
        diff = pooled - u[:, gi * POOL_GW:(gi + 1) * POOL_GW]
        mixed.append(_dot(diff.astype(BF16), poolw_ref[gi]))
    pm = (jnp.concatenate(mixed, axis=-1) * pscale_ref[...]).astype(BF16)
    ypool_s[...] = _dot(pm, wpoolb_ref[...])
    gates_s[...] = _dot(h, wg_ref[...])

    nc = ts // GLA_CHUNK
    b = _chunk_cumsum(log_a, GLA_CHUNK)
    b3 = b.reshape(nc, GLA_CHUNK, KEY_W)
    b_last = b3[:, GLA_CHUNK - 1:GLA_CHUNK, :]
    lane = lax.broadcasted_iota(jnp.int32, (ts, LANE), 1)
    qd = q * jnp.exp(b) * (GLA_DK ** -0.5)
    ke = (k.reshape(nc, GLA_CHUNK, KEY_W) * jnp.exp(b_last - b3)).reshape(ts, KEY_W)
    for hd in range(GLA_HEADS):
        pair = slice((hd // 2) * LANE, (hd // 2 + 1) * LANE)
        mine = (lane < GLA_DK) if hd % 2 == 0 else (lane >= GLA_DK)
        qdm_s[hd] = jnp.where(mine, qd[:, pair], 0.0).astype(BF16)
        kem_s[hd] = jnp.where(mine, ke[:, pair], 0.0).astype(BF16)
    v_s[...] = v.astype(BF16)
    grp = GLA_GROUP * GLA_CHUNK
    kd = k * jnp.exp(-b)
    for p in range(GLA_HEADS // 2):
        kd_t = kd[:, p * LANE:(p + 1) * LANE].T.astype(BF16)
        for j in range(ts // grp):
            kdt_s[p, j] = kd_t[:, j * grp:(j + 1) * grp]
    dec_rows = jnp.concatenate([jnp.exp(b_last[c]) for c in range(nc)]
                               + [jnp.zeros((LANE - nc, KEY_W), F32)], axis=0)
    dec_t = dec_rows.T
    for c in range(nc):
        for p in range(GLA_HEADS // 2):
            dec_s[c, p] = jnp.broadcast_to(dec_t[p * LANE:(p + 1) * LANE, c:c + 1], (LANE, GLA_DV))

    rg = lax.broadcasted_iota(jnp.int32, (grp, grp), 0)
    cg = lax.broadcasted_iota(jnp.int32, (grp, grp), 1)
    tri_g = (rg >= cg) & (rg // GLA_CHUNK == cg // GLA_CHUNK)

    for j in range(nc // GLA_GROUP):
        rows_g = slice(j * grp, (j + 1) * grp)
        for hd in range(GLA_HEADS):
            p = hd // 2
            cols = slice(hd * GLA_DV, (hd + 1) * GLA_DV)
            qm = qdm_s[hd, rows_g, :]
            vh = v_s[rows_g, cols]
            km = kem_s[hd, rows_g, :]
            scores = jnp.where(tri_g, _dot(qm, kdt_s[p, j]), 0.0)
            o_intra = _dot(scores.astype(BF16), vh)
            st = state_s[hd]
            outs = []
            for cc in range(GLA_GROUP):
                c = GLA_GROUP * j + cc
                part = slice(cc * GLA_CHUNK, (cc + 1) * GLA_CHUNK)
                outs.append(o_intra[part] + _dot(qm[part], st.astype(BF16)))
                st = dec_s[c, p] * st + _dot_tn(km[part], vh[part])
            state_s[hd] = st
            o_s[rows_g, cols] = jnp.concatenate(outs, axis=0)

    o_all = o_s[...]
    parts = []
    for hd in range(GLA_HEADS):
        oh = o_all[:, hd * GLA_DV:(hd + 1) * GLA_DV]
        parts.append(oh * lax.rsqrt(jnp.mean(oh * oh, axis=-1, keepdims=True) + NORM_EPS))
    o_n = jnp.concatenate(parts, axis=-1) * glag_ref[...]
    o_g = (o_n * (r * jax.nn.sigmoid(r))).astype(BF16)
    y_gla = _dot(o_g, wglab_ref[...])

    merged = (jax.nn.sigmoid(gates_s[:, :D_MODEL]) * y_gla
              + jax.nn.sigmoid(gates_s[:, D_MODEL:]) * ypool_s[...]).astype(BF16)
    x1 = xf + _dot(merged, wout_ref[...])
    x1_ref[...] = x1

    ms2 = jnp.mean(x1 * x1, axis=-1, keepdims=True)
    h2 = x1 * lax.rsqrt(ms2 + NORM_EPS) * g2_ref[...]
    _store_rows(h2_ref, _pack_rows(h2))
    lt = _dot_nt(wrt_ref[...], h2.astype(BF16)) + brt_ref[:, 0:1]

    l0, l1, l2, l3 = lt[0:1], lt[1:2], lt[2:3], lt[3:4]
    gm = jnp.maximum(jnp.maximum(l0, l1), jnp.maximum(l2, l3))
    gidx = jnp.where(l0 == gm, 0, jnp.where(l1 == gm, 1, jnp.where(l2 == gm, 2, 3)))
    gp = 1.0 / (jnp.exp(l0 - gm) + jnp.exp(l1 - gm) + jnp.exp(l2 - gm) + jnp.exp(l3 - gm))
    el = jnp.where(gidx == 0, lt[8:16],
                   jnp.where(gidx == 1, lt[16:24], jnp.where(gidx == 2, lt[24:32], lt[32:40])))
    row8 = lax.broadcasted_iota(jnp.int32, (EXPERTS_PER_GROUP, ts), 0)
    m1 = jnp.max(el, axis=0, keepdims=True)
    i1 = jnp.min(jnp.where(el == m1, row8, EXPERTS_PER_GROUP), axis=0, keepdims=True)
    el2 = jnp.where(row8 == i1, -jnp.inf, el)
    m2 = jnp.max(el2, axis=0, keepdims=True)
    i2 = jnp.min(jnp.where(el2 == m2, row8, EXPERTS_PER_GROUP), axis=0, keepdims=True)
    dd = jnp.exp(m2 - m1)
    p1 = 1.0 / (1.0 + dd)
    p2 = dd / (1.0 + dd)
    e1 = gidx * EXPERTS_PER_GROUP + i1
    e2 = gidx * EXPERTS_PER_GROUP + i2
    eid_ref[0:1, :] = e1
    eid_ref[1:2, :] = e2
    gate_ref[...] = jnp.concatenate([gp * p1, gp * p2, jnp.zeros((6, ts), F32)], axis=0)

    row32 = lax.broadcasted_iota(jnp.int32, (N_EXPERTS, ts), 0)
    oh1 = row32 == e1
    oh2 = row32 == e2
    member = jnp.where(oh1, 1.0, 0.0) + jnp.where(oh2, 1.0, 0.0)
    before = (lax.broadcasted_iota(jnp.int32, (ts, ts), 0)
              < lax.broadcasted_iota(jnp.int32, (ts, ts), 1))
    prefix = _dot(member.astype(BF16), jnp.where(before, 1.0, 0.0).astype(BF16))
    base = cnt_s[...]
    prefix = prefix + jnp.concatenate([base] * (ts // LANE), axis=1)
    rank_ref[0:1, :] = jnp.sum(jnp.where(oh1, prefix, 0.0), axis=0, keepdims=True).astype(jnp.int32)
    rank_ref[1:2, :] = jnp.sum(jnp.where(oh2, prefix, 0.0), axis=0, keepdims=True).astype(jnp.int32)
    new_cnt = base + jnp.sum(member, axis=1, keepdims=True)
    cnt_s[...] = new_cnt
    cnt_ref[...] = new_cnt


def _const_spec(shape):
    nd = len(shape)
    return pl.BlockSpec(shape, lambda b, s, _nd=nd: (0,) * _nd, pipeline_mode=pl.Buffered(1))


def _mixer_call(x, wts, ts):
    B, S, D = x.shape
    T = B * S
    ns = S // ts
    tok_spec = pl.BlockSpec((None, ts, D), lambda b, s: (b, s, 0))
    flat_tok = lambda rows: pl.BlockSpec((rows, ts), lambda b, s: (0, b * ns + s))
    in_specs = [tok_spec] + [_const_spec(w.shape) for w in wts]
    out_shape = (
        jax.ShapeDtypeStruct((B, S, D), F32),
        jax.ShapeDtypeStruct((B, S * ROW_TILES, LANE), jnp.int32),
        jax.ShapeDtypeStruct((2, T), jnp.int32),
        jax.ShapeDtypeStruct((8, T), F32),
        jax.ShapeDtypeStruct((2, T), jnp.int32),
        jax.ShapeDtypeStruct((N_EXPERTS, LANE), F32),
    )
    packed_spec = pl.BlockSpec((None, ts * ROW_TILES, LANE), lambda b, s: (b, s, 0))
    out_specs = (tok_spec, packed_spec, flat_tok(2), flat_tok(8), flat_tok(2),
                 pl.BlockSpec((N_EXPERTS, LANE), lambda b, s: (0, 0)))
    grp = GLA_GROUP * GLA_CHUNK
    scratch = [
        pltpu.VMEM((GLA_HEADS, LANE, GLA_DV), F32),
        pltpu.VMEM((POOL_HALO, POOL_W), F32),
        pltpu.VMEM((N_EXPERTS, LANE), F32),
        pltpu.VMEM((GLA_HEADS, ts, LANE), BF16),
        pltpu.VMEM((GLA_HEADS, ts, LANE), BF16),
        pltpu.VMEM((GLA_HEADS // 2, ts // grp, LANE, grp), BF16),
        pltpu.VMEM((ts, VAL_W), BF16),
        pltpu.VMEM((ts // GLA_CHUNK, GLA_HEADS // 2, LANE, GLA_DV), F32),
        pltpu.VMEM((ts, VAL_W), F32),
        pltpu.VMEM((ts, D), F32),
        pltpu.VMEM((ts, 2 * D), F32),
    ]
    return pl.pallas_call(
        _mixer_kernel,
        grid=(B, ns),
        in_specs=in_specs,
        out_specs=out_specs,
        out_shape=out_shape,
        scratch_shapes=scratch,
        compiler_params=pltpu.CompilerParams(
            dimension_semantics=("arbitrary", "arbitrary"), vmem_limit_bytes=VMEM_LIMIT),
        name="mixer",
    )(x, *wts)


_PAD_BITS = tuple(p for p in (1 << i for i in reversed(range((EXP_BM - 1).bit_length()))) if p >= SUBLANE)


def _dispatch_kernel(pst_ref, cnt_ref, dest_ref, h2_ref, xs_ref, zero_s, sem, zsem):
    td = h2_ref.shape[0]
    i = pl.program_id(0)

    def zero_copy(off, p):
        return pltpu.make_async_copy(zero_s.at[pl.ds(0, p)], xs_ref.at[pl.ds(off, p)], zsem)

    def for_each_pad_piece(e, fn):
        cnt = cnt_ref[e]
        start = pst_ref[e] + cnt
        end = start + (EXP_BM - cnt % EXP_BM) % EXP_BM
        aligned = jnp.minimum((start + SUBLANE - 1) // SUBLANE * SUBLANE, end)
        for j in range(SUBLANE - 1):
            @pl.when(start + j < aligned)
            def _(j=j):
                fn(zero_copy(start + j, 1))

        npad = end - aligned
        off = aligned
        for p in _PAD_BITS:
            hit = (npad & p) != 0

            @pl.when(hit)
            def _(off=off, p=p):
                fn(zero_copy(pl.multiple_of(off, SUBLANE), p))

            off = off + jnp.where(hit, p, 0)

    @pl.when(i == 0)
    def _():
        zero_s[...] = jnp.zeros_like(zero_s)

        def start_e(e, c):
            for_each_pad_piece(e, lambda cp: cp.start())
            return c

        def wait_e(e, c):
            for_each_pad_piece(e, lambda cp: cp.wait())
            return c

        lax.fori_loop(0, N_EXPERTS, start_e, 0)
        lax.fori_loop(0, N_EXPERTS, wait_e, 0)

        zrows = zero_s.shape[0]
        last_cnt = cnt_ref[N_EXPERTS - 1]
        used = pst_ref[N_EXPERTS - 1] + (last_cnt + EXP_BM - 1) // EXP_BM * EXP_BM
        first_piece = used // zrows
        n_pieces = xs_ref.shape[0] // zrows

        def tail_copy(t):
            return zero_copy(pl.multiple_of(t * zrows, zrows), zrows)

        lax.fori_loop(first_piece, n_pieces, lambda t, c: (tail_copy(t).start(), c)[1], 0)
        lax.fori_loop(first_piece, n_pieces, lambda t, c: (tail_copy(t).wait(), c)[1], 0)

    def issue(g, c):
        for u in range(SUBLANE):
            for kk in range(2):
                d = dest_ref[kk * td + g * SUBLANE + u]
                pltpu.make_async_copy(h2_ref.at[g * SUBLANE + u], xs_ref.at[d], sem).start(priority=kk)
        return c

    lax.fori_loop(0, td // SUBLANE, issue, 0)
    for kk in range(2):
        pltpu.make_async_copy(xs_ref.at[pl.ds(0, td)], xs_ref.at[pl.ds(td, td)], sem).wait()


def _dispatch_call(h2, dest3, pstarts, counts, n_rows):
    T, RT, _ = h2.shape
    nt, _, td2 = dest3.shape
    td = td2 // 2
    grid_spec = pltpu.PrefetchScalarGridSpec(
        num_scalar_prefetch=2,
        grid=(nt,),
        in_specs=[
            pl.BlockSpec((None, None, 2 * td), lambda i, *_: (i, 0, 0), memory_space=pltpu.SMEM),
            pl.BlockSpec((td, RT, LANE), lambda i, *_: (i, 0, 0)),
        ],
        out_specs=pl.BlockSpec(memory_space=pl.ANY),
        scratch_shapes=[
            pltpu.VMEM((_PAD_BITS[0], RT, LANE), h2.dtype),
            pltpu.SemaphoreType.DMA(()),
            pltpu.SemaphoreType.DMA(()),
        ],
    )
    return pl.pallas_call(
        _dispatch_kernel,
        grid_spec=grid_spec,
        out_shape=jax.ShapeDtypeStruct((n_rows, RT, LANE), h2.dtype),
        compiler_params=pltpu.CompilerParams(
            dimension_semantics=("arbitrary",), vmem_limit_bytes=VMEM_LIMIT, has_side_effects=True),
        name="dispatch",
    )(pstarts, counts, dest3, h2)


def _expert_kernel(be_ref, nu_ref, x_ref, wg_ref, wu_ref, wd_ref, y_ref):
    b = pl.program_id(0)

    @pl.when(b < nu_ref[0])
    def _():
        lo, hi = _unpack_rows(_load_rows(x_ref))
        xb = jnp.concatenate([lo.astype(BF16), hi.astype(BF16)], axis=1)
        g = _dot(xb, wg_ref[...])
        u = _dot(xb, wu_ref[...])
        hmid = (g * jax.nn.sigmoid(g) * u).astype(BF16)
        _store_rows(y_ref, _pack_rows(_dot(hmid, wd_ref[...])))

    @pl.when(b >= nu_ref[0])
    def _():
        y_ref[...] = jnp.zeros_like(y_ref)


def _expert_call(xs, block_e, n_used, wg, wu, wd):
    R = xs.shape[0] // ROW_TILES
    D = D_MODEL
    nb = R // EXP_BM
    blk = (EXP_BM * ROW_TILES, LANE)
    last = lambda b, nu: jnp.minimum(b, nu[0] - 1)
    grid_spec = pltpu.PrefetchScalarGridSpec(
        num_scalar_prefetch=2,
        grid=(nb,),
        in_specs=[
            pl.BlockSpec(blk, lambda b, be, nu: (last(b, nu), 0)),
            pl.BlockSpec((None, D, D_EXPERT), lambda b, be, nu: (be[last(b, nu)], 0, 0)),
            pl.BlockSpec((None, D, D_EXPERT), lambda b, be, nu: (be[last(b, nu)], 0, 0)),
            pl.BlockSpec((None, D_EXPERT, D), lambda b, be, nu: (be[last(b, nu)], 0, 0)),
        ],
        out_specs=pl.BlockSpec(blk, lambda b, be, nu: (b, 0)),
    )
    return pl.pallas_call(
        _expert_kernel,
        grid_spec=grid_spec,
        out_shape=jax.ShapeDtypeStruct(xs.shape, xs.dtype),
        compiler_params=pltpu.CompilerParams(
            dimension_semantics=("arbitrary",), vmem_limit_bytes=VMEM_LIMIT),
        name="experts",
    )(block_e, n_used, xs, wg, wu, wd)


def _combine_kernel(dcur_ref, dnext_ref, x1_ref, gate_ref, gf_ref, y_ref, out_ref, ybuf, sem):
    tf = x1_ref.shape[0]
    i = pl.program_id(0)
    n = pl.num_programs(0)
    tile_rows = SUBLANE * ROW_TILES

    def issue_all(dref, slot):
        def group(g, c):
            for u in range(SUBLANE):
                for kk in range(2):
                    d = dref[kk * tf + g * SUBLANE + u]
                    slab = pl.ds(pl.multiple_of(g * tile_rows, tile_rows) + u * ROW_TILES, ROW_TILES)
                    pltpu.make_async_copy(y_ref.at[d], ybuf.at[slot, kk, slab], sem.at[slot]).start(priority=kk)
            return c

        lax.fori_loop(0, tf // SUBLANE, group, 0)

    @pl.when(i == 0)
    def _():
        issue_all(dcur_ref, 0)

    for par in range(2):
        @pl.when((i % 2 == par) & (i + 1 < n))
        def _(par=par):
            issue_all(dnext_ref, 1 - par)

    slot = i % 2
    for kk in range(2):
        pltpu.make_async_copy(y_ref.at[pl.ds(0, tf)], y_ref.at[pl.ds(tf, tf)], sem.at[slot]).wait()

    g = jnp.concatenate([gate_ref[...], jnp.zeros((LANE - 8, tf), F32)], axis=0)
    gt = g.T
    w = ROW_TILES * LANE
    y0_lo, y0_hi = _unpack_rows(_load_rows(ybuf.at[slot, 0]))
    y1_lo, y1_hi = _unpack_rows(_load_rows(ybuf.at[slot, 1]))
    g0 = gt[:, 0:1]
    g1 = gt[:, 1:2]
    xo_lo = x1_ref[:, :w] + (g0 * y0_lo + g1 * y1_lo)
    xo_hi = x1_ref[:, w:] + (g0 * y0_hi + g1 * y1_hi)
    ms = (jnp.sum(xo_lo * xo_lo, axis=-1, keepdims=True)
          + jnp.sum(xo_hi * xo_hi, axis=-1, keepdims=True)) * (1.0 / (2 * w))
    scale = lax.rsqrt(ms + NORM_EPS)
    out_ref[:, :w] = xo_lo * scale * gf_ref[:, :w]
    out_ref[:, w:] = xo_hi * scale * gf_ref[:, w:]


def _combine_call(x1, dest3, gates, gf, ys):
    T, D = x1.shape
    nt, _, tf2 = dest3.shape
    tf = tf2 // 2
    return pl.pallas_call(
        _combine_kernel,
        grid=(nt,),
        in_specs=[
            pl.BlockSpec((None, None, 2 * tf), lambda i: (i, 0, 0), memory_space=pltpu.SMEM),
            pl.BlockSpec((None, None, 2 * tf), lambda i: (jnp.minimum(i + 1, nt - 1), 0, 0),
                         memory_space=pltpu.SMEM),
            pl.BlockSpec((tf, D), lambda i: (i, 0)),
            pl.BlockSpec((8, tf), lambda i: (0, i)),
            pl.BlockSpec((1, D), lambda i: (0, 0)),
            pl.BlockSpec(memory_space=pl.ANY),
        ],
        out_specs=pl.BlockSpec((tf, D), lambda i: (i, 0)),
        out_shape=jax.ShapeDtypeStruct((T, D), F32),
        scratch_shapes=[pltpu.VMEM((2, 2, tf * ROW_TILES, LANE), ys.dtype),
                        pltpu.SemaphoreType.DMA((2,))],
        compiler_params=pltpu.CompilerParams(
            dimension_semantics=("arbitrary",), vmem_limit_bytes=VMEM_LIMIT),
        name="combine",
    )(dest3, dest3, x1, gates, gf, ys)


def _pick_tile(n, want):
    t = min(want, n)
    while n % t:
        t //= 2
    return t


def _mixer_weights(norm1_g, w_in, w_alpha_up, b_alpha, gla_norm_g, w_gla_branch, pool_w, pool_scale,
                   w_pool_branch, w_out, norm2_g, w_rg, b_rg, w_re, b_re):
    c0 = 2 * KEY_W + 2 * VAL_W
    c1 = c0 + GATE_RANK
    c2 = c1 + POOL_W
    w_qkvr = w_in[:, :c0].astype(BF16)
    w_a = jnp.pad(w_in[:, c0:c1], ((0, 0), (0, LANE - GATE_RANK))).astype(BF16)
    w_u = w_in[:, c1:c2].astype(BF16)
    w_g = w_in[:, c2:].astype(BF16)
    w_alpha = jnp.pad(w_alpha_up, ((0, LANE - GATE_RANK), (0, 0))).astype(BF16)
    w_re_t = jnp.transpose(w_re, (0, 2, 1)).reshape(N_EXPERTS, D_MODEL)
    wrt = jnp.zeros((ROUTER_ROWS, D_MODEL), F32)
    wrt = wrt.at[0:N_GROUPS].set(w_rg.T).at[8:8 + N_EXPERTS].set(w_re_t).astype(BF16)
    brt = jnp.zeros((ROUTER_ROWS,), F32).at[0:N_GROUPS].set(b_rg).at[8:8 + N_EXPERTS].set(b_re.reshape(-1))
    brt = jnp.broadcast_to(brt[:, None], (ROUTER_ROWS, LANE))
    row = lambda a: a.reshape(1, -1).astype(F32)
    return (row(norm1_g), w_qkvr, w_a, w_u, w_g, w_alpha, row(b_alpha), row(gla_norm_g),
            w_gla_branch.astype(BF16), pool_w.astype(BF16), row(pool_scale), w_pool_branch.astype(BF16),
            w_out.astype(BF16), row(norm2_g), wrt, brt)


def kernel(x, norm1_g, w_in, w_alpha_up, b_alpha, gla_norm_g, w_gla_branch, pool_w, pool_scale,
           w_pool_branch, w_out, norm2_g, w_router_group, b_router_group, w_router_expert,
           b_router_expert, w_exp_gate, w_exp_up, w_exp_down, norm_f_g):
    B, S, D = x.shape
    T = B * S
    depth = w_in.shape[0]
    ts = _pick_tile(S, MIX_TS)
    tr = _pick_tile(T, ROW_TILE)
    n_assign = 2 * T
    n_blocks = -(-(n_assign + N_EXPERTS * (EXP_BM - 1)) // EXP_BM)
    n_rows = n_blocks * EXP_BM

    assert depth == 1, "kernel supports the problem's DEPTH=1"
    for l in range(depth):
        wts = _mixer_weights(norm1_g[l], w_in[l], w_alpha_up[l], b_alpha[l], gla_norm_g[l], w_gla_branch[l],
                             pool_w[l], pool_scale[l], w_pool_branch[l], w_out[l], norm2_g[l],
                             w_router_group[l], b_router_group[l], w_router_expert[l], b_router_expert[l])
        x1, h2, eid, gates, rank, cnt = _mixer_call(x, wts, ts)

        counts = cnt[:, 0].astype(jnp.int32)
        padded = ((counts + EXP_BM - 1) // EXP_BM) * EXP_BM
        pends = jnp.cumsum(padded)
        pstarts = pends - padded
        expert_col = jnp.arange(N_EXPERTS, dtype=jnp.int32)[:, None, None]
        dest = jnp.sum(jnp.where(eid[None] == expert_col, pstarts[:, None, None], 0), axis=0) + rank
        dest3 = dest.reshape(2, T // tr, tr).transpose(1, 0, 2).reshape(T // tr, 1, 2 * tr)
        blk_start = jnp.arange(n_blocks, dtype=jnp.int32) * EXP_BM
        block_e = jnp.minimum(jnp.sum(pends[None, :] <= blk_start[:, None], axis=1), N_EXPERTS - 1).astype(jnp.int32)
        n_used = (pends[-1:] // EXP_BM).astype(jnp.int32)

        xs = _dispatch_call(h2.reshape(T, ROW_TILES, LANE), dest3, pstarts.astype(jnp.int32), counts, n_rows)
        ys = _expert_call(xs.reshape(n_rows * ROW_TILES, LANE), block_e, n_used, w_exp_gate[l].astype(BF16),
                          w_exp_up[l].astype(BF16), w_exp_down[l].astype(BF16))
        out = _combine_call(x1.reshape(T, D), dest3, gates, norm_f_g.reshape(1, D).astype(F32),
                            ys.reshape(n_rows, ROW_TILES, LANE))
        x = out.reshape(B, S, D)
    return x
```

```python
import jax
import jax.numpy as jnp
from jax import lax
from jax.experimental import pallas as pl
from jax.experimental.pallas import tpu as pltpu

F32 = jnp.float32
BF16 = jnp.bfloat16

D_MODEL = 1024
GLA_HEADS = 4
GLA_DK = 64
GLA_DV = 128
KEY_W = GLA_HEADS * GLA_DK
VAL_W = GLA_HEADS * GLA_DV
GATE_RANK = 16
GATE_NORMALIZER = 16.0
GLA_CHUNK = 64
POOL_WINDOWS = (2, 4, 8, 16)
POOL_W = 512
POOL_GW = 128
N_GROUPS = 4
EXPERTS_PER_GROUP = 8
N_EXPERTS = 32
D_EXPERT = 256
NORM_EPS = 1e-6

LANE = 128
SUBLANE = 8
POOL_HALO = 16
ROUTER_ROWS = 48

MIX_TS = 512
DISPATCH_TILE = 2048
COMBINE_TILE = 512
GLA_GROUP = 2
EXP_BM = 512
VMEM_LIMIT = 56 * 1024 * 1024


def _dot(a, b):
    return jnp.dot(a, b, preferred_element_type=F32)


def _dot_nt(a, b):
    return lax.dot_general(a, b, (((1,), (1,)), ((), ())), preferred_element_type=F32)


def _dot_tn(a, b):
    return lax.dot_general(a, b, (((0,), (0,)), ((), ())), preferred_element_type=F32)


def _bf16_bits(x):
    return lax.bitcast_convert_type(x.astype(BF16).astype(F32), jnp.int32)


def _pack_rows(x):
    w = x.shape[1] // 2
    return lax.shift_right_logical(_bf16_bits(x[:, :w]), 16) | _bf16_bits(x[:, w:])


def _unpack_rows(words):
    lo = lax.bitcast_convert_type(lax.shift_left(words, 16), F32)
    hi = lax.bitcast_convert_type(words & -65536, F32)
    return lo, hi


ROW_TILES = D_MODEL // 2 // LANE


def _store_rows(ref, words):
    n = words.shape[0]
    for t in range(ROW_TILES):
        ref[pl.ds(t, n, stride=ROW_TILES), :] = words[:, t * LANE:(t + 1) * LANE]


def _load_rows(ref):
    n = ref.shape[0] // ROW_TILES
    return jnp.concatenate([ref[pl.ds(t, n, stride=ROW_TILES), :] for t in range(ROW_TILES)], axis=1)


def _chunk_cumsum(x, chunk):
    n, w = x.shape
    pos = lax.broadcasted_iota(jnp.int32, (n, w), 0) % chunk
    step = 1
    while step < chunk:
        if step < 8:
            shifted = pltpu.roll(x, step, axis=0)
        else:
            shifted = jnp.concatenate([jnp.zeros((step, w), x.dtype), x[:n - step]], axis=0)
        x = x + jnp.where(pos >= step, shifted, 0.0)
        step *= 2
    return x


def _mixer_kernel(x_ref, g1_ref, wqkvr_ref, wa_ref, wu_ref, wg_ref, walpha_ref, balpha_ref,
                  glag_ref, wglab_ref, poolw_ref, pscale_ref, wpoolb_ref, wout_ref, g2_ref,
                  wrt_ref, brt_ref,
                  x1_ref, h2_ref, eid_ref, gate_ref, rank_ref, cnt_ref,
                  state_s, carry_s, cnt_s, qdm_s, kem_s, kdt_s, v_s, dec_s, o_s, ypool_s, gates_s):
    ts = x_ref.shape[0]
    b_idx = pl.program_id(0)
    s_idx = pl.program_id(1)

    @pl.when(s_idx == 0)
    def _():
        state_s[...] = jnp.zeros_like(state_s)
        carry_s[...] = jnp.zeros_like(carry_s)

    @pl.when((b_idx == 0) & (s_idx == 0))
    def _():
        cnt_s[...] = jnp.zeros_like(cnt_s)

    xf = x_ref[...]
    ms = jnp.mean(xf * xf, axis=-1, keepdims=True)
    h = (xf * lax.rsqrt(ms + NORM_EPS) * g1_ref[...]).astype(BF16)

    qkvr = _dot(h, wqkvr_ref[...])
    q = qkvr[:, 0:KEY_W]
    k = qkvr[:, KEY_W:2 * KEY_W]
    v = qkvr[:, 2 * KEY_W:2 * KEY_W + VAL_W]
    r = qkvr[:, 2 * KEY_W + VAL_W:]

    a_low = _dot(h, wa_ref[...])
    z = _dot(a_low.astype(BF16), walpha_ref[...]) + balpha_ref[...]
    log_a = (jnp.minimum(z, 0.0) - jnp.log1p(jnp.exp(-jnp.abs(z)))) * (1.0 / GATE_NORMALIZER)

    u = _dot(h, wu_ref[...])
    ext = jnp.concatenate([carry_s[...], u], axis=0)
    carry_s[...] = u[ts - POOL_HALO:, :]
    pos = (s_idx * ts + lax.broadcasted_iota(jnp.int32, (ts, 1), 0)).astype(F32)
    mixed = []
    for gi, w in enumerate(POOL_WINDOWS):
        a = ext[:, gi * POOL_GW:(gi + 1) * POOL_GW]
        step = 1
        while step < w:
            a = a + pltpu.roll(a, step, axis=0)
            step *= 2
        pooled = a[POOL_HALO:, :] / jnp.minimum(pos + 1.0, float(w))
        diff = pooled - u[:, gi * POOL_GW:(gi + 1) * POOL_GW]
        mixed.append(_dot(diff.astype(BF16), poolw_ref[gi]))
    pm = (jnp.concatenate(mixed, axis=-1) * pscale_ref[...]).astype(BF16)
    ypool_s[...] = _dot(pm, wpoolb_ref[...])
    gates_s[...] = _dot(h, wg_ref[...])

    nc = ts // GLA_CHUNK
    b = _chunk_cumsum(log_a, GLA_CHUNK)
    b3 = b.reshape(nc, GLA_CHUNK, KEY_W)
    b_last = b3[:, GLA_CHUNK - 1:GLA_CHUNK, :]
    lane = lax.broadcasted_iota(jnp.int32, (ts, LANE), 1)
    qd = q * jnp.exp(b) * (GLA_DK ** -0.5)
    ke = (k.reshape(nc, GLA_CHUNK, KEY_W) * jnp.exp(b_last - b3)).reshape(ts, KEY_W)
    for hd in range(GLA_HEADS):
        pair = slice((hd // 2) * LANE, (hd // 2 + 1) * LANE)
        mine = (lane < GLA_DK) if hd % 2 == 0 else (lane >= GLA_DK)
        qdm_s[hd] = jnp.where(mine, qd[:, pair], 0.0).astype(BF16)
        kem_s[hd] = jnp.where(mine, ke[:, pair], 0.0).astype(BF16)
    v_s[...] = v.astype(BF16)
    grp = GLA_GROUP * GLA_CHUNK
    kd = k * jnp.exp(-b)
    for p in range(GLA_HEADS // 2):
        kd_t = kd[:, p * LANE:(p + 1) * LANE].T.astype(BF16)
        for j in range(ts // grp):
            kdt_s[p, j] = kd_t[:, j * grp:(j + 1) * grp]
    dec_rows = jnp.concatenate([jnp.exp(b_last[c]) for c in range(nc)]
                               + [jnp.zeros((LANE - nc, KEY_W), F32)], axis=0)
    dec_t = dec_rows.T
    for c in range(nc):
        for p in range(GLA_HEADS // 2):
            dec_s[c, p] = jnp.broadcast_to(dec_t[p * LANE:(p + 1) * LANE, c:c + 1], (LANE, GLA_DV))

    rg = lax.broadcasted_iota(jnp.int32, (grp, grp), 0)
    cg = lax.broadcasted_iota(jnp.int32, (grp, grp), 1)
    tri_g = (rg >= cg) & (rg // GLA_CHUNK == cg // GLA_CHUNK)

    for j in range(nc // GLA_GROUP):
        rows_g = slice(j * grp, (j + 1) * grp)
        for hd in range(GLA_HEADS):
            p = hd // 2
            cols = slice(hd * GLA_DV, (hd + 1) * GLA_DV)
            qm = qdm_s[hd, rows_g, :]
            vh = v_s[rows_g, cols]
            km = kem_s[hd, rows_g, :]
            scores = jnp.where(tri_g, _dot(qm, kdt_s[p, j]), 0.0)
            o_intra = _dot(scores.astype(BF16), vh)
            st = state_s[hd]
            outs = []
            for cc in range(GLA_GROUP):
                c = GLA_GROUP * j + cc
                part = slice(cc * GLA_CHUNK, (cc + 1) * GLA_CHUNK)
                outs.append(o_intra[part] + _dot(qm[part], st.astype(BF16)))
                st = dec_s[c, p] * st + _dot_tn(km[part], vh[part])
            state_s[hd] = st
            o_s[rows_g, cols] = jnp.concatenate(outs, axis=0)

    o_all = o_s[...]
    parts = []
    for hd in range(GLA_HEADS):
        oh = o_all[:, hd * GLA_DV:(hd + 1) * GLA_DV]
        parts.append(oh * lax.rsqrt(jnp.mean(oh * oh, axis=-1, keepdims=True) + NORM_EPS))
    o_n = jnp.concatenate(parts, axis=-1) * glag_ref[...]
    o_g = (o_n * (r * jax.nn.sigmoid(r))).astype(BF16)
    y_gla = _dot(o_g, wglab_ref[...])

    merged = (jax.nn.sigmoid(gates_s[:, :D_MODEL]) * y_gla
              + jax.nn.sigmoid(gates_s[:, D_MODEL:]) * ypool_s[...]).astype(BF16)
    x1 = xf + _dot(merged, wout_ref[...])
    x1_ref[...] = x1

    ms2 = jnp.mean(x1 * x1, axis=-1, keepdims=True)
    h2 = x1 * lax.rsqrt(ms2 + NORM_EPS) * g2_ref[...]
    _store_rows(h2_ref, _pack_rows(h2))
    lt = _dot_nt(wrt_ref[...], h2.astype(BF16)) + brt_ref[:, 0:1]

    l0, l1, l2, l3 = lt[0:1], lt[1:2], lt[2:3], lt[3:4]
    gm = jnp.maximum(jnp.maximum(l0, l1), jnp.maximum(l2, l3))
    gidx = jnp.where(l0 == gm, 0, jnp.where(l1 == gm, 1, jnp.where(l2 == gm, 2, 3)))
    gp = 1.0 / (jnp.exp(l0 - gm) + jnp.exp(l1 - gm) + jnp.exp(l2 - gm) + jnp.exp(l3 - gm))
    el = jnp.where(gidx == 0, lt[8:16],
                   jnp.where(gidx == 1, lt[16:24], jnp.where(gidx == 2, lt[24:32], lt[32:40])))
    row8 = lax.broadcasted_iota(jnp.int32, (EXPERTS_PER_GROUP, ts), 0)
    m1 = jnp.max(el, axis=0, keepdims=True)
    i1 = jnp.min(jnp.where(el == m1, row8, EXPERTS_PER_GROUP), axis=0, keepdims=True)
    el2 = jnp.where(row8 == i1, -jnp.inf, el)
    m2 = jnp.max(el2, axis=0, keepdims=True)
    i2 = jnp.min(jnp.where(el2 == m2, row8, EXPERTS_PER_GROUP), axis=0, keepdims=True)
    dd = jnp.exp(m2 - m1)
    p1 = 1.0 / (1.0 + dd)
    p2 = dd / (1.0 + dd)
    e1 = gidx * EXPERTS_PER_GROUP + i1
    e2 = gidx * EXPERTS_PER_GROUP + i2
    eid_ref[0:1, :] = e1
    eid_ref[1:2, :] = e2
    gate_ref[...] = jnp.concatenate([gp * p1, gp * p2, jnp.zeros((6, ts), F32)], axis=0)

    row32 = lax.broadcasted_iota(jnp.int32, (N_EXPERTS, ts), 0)
    oh1 = row32 == e1
    oh2 = row32 == e2
    member = jnp.where(oh1, 1.0, 0.0) + jnp.where(oh2, 1.0, 0.0)
    before = (lax.broadcasted_iota(jnp.int32, (ts, ts), 0)
              < lax.broadcasted_iota(jnp.int32, (ts, ts), 1))
    prefix = _dot(member.astype(BF16), jnp.where(before, 1.0, 0.0).astype(BF16))
    base = cnt_s[...]
    prefix = prefix + jnp.concatenate([base] * (ts // LANE), axis=1)
    rank_ref[0:1, :] = jnp.sum(jnp.where(oh1, prefix, 0.0), axis=0, keepdims=True).astype(jnp.int32)
    rank_ref[1:2, :] = jnp.sum(jnp.where(oh2, prefix, 0.0), axis=0, keepdims=True).astype(jnp.int32)
    new_cnt = base + jnp.sum(member, axis=1, keepdims=True)
    cnt_s[...] = new_cnt
    cnt_ref[...] = new_cnt


def _const_spec(shape):
    nd = len(shape)
    return pl.BlockSpec(shape, lambda b, s, _nd=nd: (0,) * _nd, pipeline_mode=pl.Buffered(1))


def _mixer_call(x, wts, ts):
    B, S, D = x.shape
    T = B * S
    ns = S // ts
    tok_spec = pl.BlockSpec((None, ts, D), lambda b, s: (b, s, 0))
    flat_tok = lambda rows: pl.BlockSpec((rows, ts), lambda b, s: (0, b * ns + s))
    in_specs = [tok_spec] + [_const_spec(w.shape) for w in wts]
    out_shape = (
        jax.ShapeDtypeStruct((B, S, D), F32),
        jax.ShapeDtypeStruct((B, S * ROW_TILES, LANE), jnp.int32),
        jax.ShapeDtypeStruct((2, T), jnp.int32),
        jax.ShapeDtypeStruct((8, T), F32),
        jax.ShapeDtypeStruct((2, T), jnp.int32),
        jax.ShapeDtypeStruct((N_EXPERTS, LANE), F32),
    )
    packed_spec = pl.BlockSpec((None, ts * ROW_TILES, LANE), lambda b, s: (b, s, 0))
    out_specs = (tok_spec, packed_spec, flat_tok(2), flat_tok(8), flat_tok(2),
                 pl.BlockSpec((N_EXPERTS, LANE), lambda b, s: (0, 0)))
    grp = GLA_GROUP * GLA_CHUNK
    scratch = [
        pltpu.VMEM((GLA_HEADS, LANE, GLA_DV), F32),
        pltpu.VMEM((POOL_HALO, POOL_W), F32),
        pltpu.VMEM((N_EXPERTS, LANE), F32),
        pltpu.VMEM((GLA_HEADS, ts, LANE), BF16),
        pltpu.VMEM((GLA_HEADS, ts, LANE), BF16),
        pltpu.VMEM((GLA_HEADS // 2, ts // grp, LANE, grp), BF16),
        pltpu.VMEM((ts, VAL_W), BF16),
        pltpu.VMEM((ts // GLA_CHUNK, GLA_HEADS // 2, LANE, GLA_DV), F32),
        pltpu.VMEM((ts, VAL_W), F32),
        pltpu.VMEM((ts, D), F32),
        pltpu.VMEM((ts, 2 * D), F32),
    ]
    return pl.pallas_call(
        _mixer_kernel,
        grid=(B, ns),
        in_specs=in_specs,
        out_specs=out_specs,
        out_shape=out_shape,
        scratch_shapes=scratch,
        compiler_params=pltpu.CompilerParams(
            dimension_semantics=("arbitrary", "arbitrary"), vmem_limit_bytes=VMEM_LIMIT),
        name="mixer",
    )(x, *wts)


_PAD_BITS = tuple(p for p in (1 << i for i in reversed(range((EXP_BM - 1).bit_length()))) if p >= SUBLANE)


def _dispatch_kernel(pst_ref, cnt_ref, dest_ref, h2_ref, xs_ref, zero_s, sem, zsem):
    td = h2_ref.shape[0]
    i = pl.program_id(0)

    def zero_copy(off, p):
        return pltpu.make_async_copy(zero_s.at[pl.ds(0, p)], xs_ref.at[pl.ds(off, p)], zsem)

    def for_each_pad_piece(e, fn):
        cnt = cnt_ref[e]
        start = pst_ref[e] + cnt
        end = start + (EXP_BM - cnt % EXP_BM) % EXP_BM
        aligned = jnp.minimum((start + SUBLANE - 1) // SUBLANE * SUBLANE, end)
        for j in range(SUBLANE - 1):
            @pl.when(start + j < aligned)
            def _(j=j):
                fn(zero_copy(start + j, 1))

        npad = end - aligned
        off = aligned
        for p in _PAD_BITS:
            hit = (npad & p) != 0

            @pl.when(hit)
            def _(off=off, p=p):
                fn(zero_copy(pl.multiple_of(off, SUBLANE), p))

            off = off + jnp.where(hit, p, 0)

    @pl.when(i == 0)
    def _():
        zero_s[...] = jnp.zeros_like(zero_s)

        def start_e(e, c):
            for_each_pad_piece(e, lambda cp: cp.start())
            return c

        def wait_e(e, c):
            for_each_pad_piece(e, lambda cp: cp.wait())
            return c

        lax.fori_loop(0, N_EXPERTS, start_e, 0)
        lax.fori_loop(0, N_EXPERTS, wait_e, 0)

        zrows = zero_s.shape[0]
        last_cnt = cnt_ref[N_EXPERTS - 1]
        used = pst_ref[N_EXPERTS - 1] + (last_cnt + EXP_BM - 1) // EXP_BM * EXP_BM
        first_piece = used // zrows
        n_pieces = xs_ref.shape[0] // zrows

        def tail_copy(t):
            return zero_copy(pl.multiple_of(t * zrows, zrows), zrows)

        lax.fori_loop(first_piece, n_pieces, lambda t, c: (tail_copy(t).start(), c)[1], 0)
        lax.fori_loop(first_piece, n_pieces, lambda t, c: (tail_copy(t).wait(), c)[1], 0)

    def issue(g, c):
        for u in range(SUBLANE):
            for kk in range(2):
                d = dest_ref[kk * td + g * SUBLANE + u]
                pltpu.make_async_copy(h2_ref.at[g * SUBLANE + u], xs_ref.at[d], sem).start(priority=kk)
        return c

    lax.fori_loop(0, td // SUBLANE, issue, 0)
    for kk in range(2):
        pltpu.make_async_copy(xs_ref.at[pl.ds(0, td)], xs_ref.at[pl.ds(td, td)], sem).wait()


def _dispatch_call(h2, dest3, pstarts, counts, n_rows):
    T, RT, _ = h2.shape
    nt, _, td2 = dest3.shape
    td = td2 // 2
    grid_spec = pltpu.PrefetchScalarGridSpec(
        num_scalar_prefetch=2,
        grid=(nt,),
        in_specs=[
            pl.BlockSpec((None, None, 2 * td), lambda i, *_: (i, 0, 0), memory_space=pltpu.SMEM),
            pl.BlockSpec((td, RT, LANE), lambda i, *_: (i, 0, 0)),
        ],
        out_specs=pl.BlockSpec(memory_space=pl.ANY),
        scratch_shapes=[
            pltpu.VMEM((_PAD_BITS[0], RT, LANE), h2.dtype),
            pltpu.SemaphoreType.DMA(()),
            pltpu.SemaphoreType.DMA(()),
        ],
    )
    return pl.pallas_call(
        _dispatch_kernel,
        grid_spec=grid_spec,
        out_shape=jax.ShapeDtypeStruct((n_rows, RT, LANE), h2.dtype),
        compiler_params=pltpu.CompilerParams(
            dimension_semantics=("arbitrary",), vmem_limit_bytes=VMEM_LIMIT, has_side_effects=True),
        name="dispatch",
    )(pstarts, counts, dest3, h2)


def _expert_kernel(be_ref, nu_ref, x_ref, wg_ref, wu_ref, wd_ref, y_ref):
    b = pl.program_id(0)

    @pl.when(b < nu_ref[0])
    def _():
        lo, hi = _unpack_rows(_load_rows(x_ref))
        xb = jnp.concatenate([lo.astype(BF16), hi.astype(BF16)], axis=1)
        g = _dot(xb, wg_ref[...])
        u = _dot(xb, wu_ref[...])
        hmid = (g * jax.nn.sigmoid(g) * u).astype(BF16)
        _store_rows(y_ref, _pack_rows(_dot(hmid, wd_ref[...])))

    @pl.when(b >= nu_ref[0])
    def _():
        y_ref[...] = jnp.zeros_like(y_ref)


def _expert_call(xs, block_e, n_used, wg, wu, wd):
    R = xs.shape[0] // ROW_TILES
    D = D_MODEL
    nb = R // EXP_BM
    blk = (EXP_BM * ROW_TILES, LANE)
    last = lambda b, nu: jnp.minimum(b, nu[0] - 1)
    grid_spec = pltpu.PrefetchScalarGridSpec(
        num_scalar_prefetch=2,
        grid=(nb,),
        in_specs=[
            pl.BlockSpec(blk, lambda b, be, nu: (last(b, nu), 0)),
            pl.BlockSpec((None, D, D_EXPERT), lambda b, be, nu: (be[last(b, nu)], 0, 0)),
            pl.BlockSpec((None, D, D_EXPERT), lambda b, be, nu: (be[last(b, nu)], 0, 0)),
            pl.BlockSpec((None, D_EXPERT, D), lambda b, be, nu: (be[last(b, nu)], 0, 0)),
        ],
        out_specs=pl.BlockSpec(blk, lambda b, be, nu: (b, 0)),
    )
    return pl.pallas_call(
        _expert_kernel,
        grid_spec=grid_spec,
        out_shape=jax.ShapeDtypeStruct(xs.shape, xs.dtype),
        compiler_params=pltpu.CompilerParams(
            dimension_semantics=("arbitrary",), vmem_limit_bytes=VMEM_LIMIT),
        name="experts",
    )(block_e, n_used, xs, wg, wu, wd)


def _combine_kernel(dcur_ref, dnext_ref, x1_ref, gate_ref, gf_ref, y_ref, out_ref, ybuf, sem):
    tf = x1_ref.shape[0]
    i = pl.program_id(0)
    n = pl.num_programs(0)
    tile_rows = SUBLANE * ROW_TILES

    def issue_all(dref, slot):
        def group(g, c):
            for u in range(SUBLANE):
                for kk in range(2):
                    d = dref[kk * tf + g * SUBLANE + u]
                    slab = pl.ds(pl.multiple_of(g * tile_rows, tile_rows) + u * ROW_TILES, ROW_TILES)
                    pltpu.make_async_copy(y_ref.at[d], ybuf.at[slot, kk, slab], sem.at[slot]).start(priority=kk)
            return c

        lax.fori_loop(0, tf // SUBLANE, group, 0)

    @pl.when(i == 0)
    def _():
        issue_all(dcur_ref, 0)

    for par in range(2):
        @pl.when((i % 2 == par) & (i + 1 < n))
        def _(par=par):
            issue_all(dnext_ref, 1 - par)

    slot = i % 2
    for kk in range(2):
        pltpu.make_async_copy(y_ref.at[pl.ds(0, tf)], y_ref.at[pl.ds(tf, tf)], sem.at[slot]).wait()

    g = jnp.concatenate([gate_ref[...], jnp.zeros((LANE - 8, tf), F32)], axis=0)
    gt = g.T
    w = ROW_TILES * LANE
    y0_lo, y0_hi = _unpack_rows(_load_rows(ybuf.at[slot, 0]))
    y1_lo, y1_hi = _unpack_rows(_load_rows(ybuf.at[slot, 1]))
    g0 = gt[:, 0:1]
    g1 = gt[:, 1:2]
    xo_lo = x1_ref[:, :w] + (g0 * y0_lo + g1 * y1_lo)
    xo_hi = x1_ref[:, w:] + (g0 * y0_hi + g1 * y1_hi)
    ms = (jnp.sum(xo_lo * xo_lo, axis=-1, keepdims=True)
          + jnp.sum(xo_hi * xo_hi, axis=-1, keepdims=True)) * (1.0 / (2 * w))
    scale = lax.rsqrt(ms + NORM_EPS)
    out_ref[:, :w] = xo_lo * scale * gf_ref[:, :w]
    out_ref[:, w:] = xo_hi * scale * gf_ref[:, w:]


def _combine_call(x1, dest3, gates, gf, ys):
    T, D = x1.shape
    nt, _, tf2 = dest3.shape
    tf = tf2 // 2
    return pl.pallas_call(
        _combine_kernel,
        grid=(nt,),
        in_specs=[
            pl.BlockSpec((None, None, 2 * tf), lambda i: (i, 0, 0), memory_space=pltpu.SMEM),
            pl.BlockSpec((None, None, 2 * tf), lambda i: (jnp.minimum(i + 1, nt - 1), 0, 0),
                         memory_space=pltpu.SMEM),
            pl.BlockSpec((tf, D), lambda i: (i, 0)),
            pl.BlockSpec((8, tf), lambda i: (0, i)),
            pl.BlockSpec((1, D), lambda i: (0, 0)),
            pl.BlockSpec(memory_space=pl.ANY),
        ],
        out_specs=pl.BlockSpec((tf, D), lambda i: (i, 0)),
        out_shape=jax.ShapeDtypeStruct((T, D), F32),
        scratch_shapes=[pltpu.VMEM((2, 2, tf * ROW_TILES, LANE), ys.dtype),
                        pltpu.SemaphoreType.DMA((2,))],
        compiler_params=pltpu.CompilerParams(
            dimension_semantics=("arbitrary",), vmem_limit_bytes=VMEM_LIMIT),
        name="combine",
    )(dest3, dest3, x1, gates, gf, ys)


def _pick_tile(n, want):
    t = min(want, n)
    while n % t:
        t //= 2
    return t


def _mixer_weights(norm1_g, w_in, w_alpha_up, b_alpha, gla_norm_g, w_gla_branch, pool_w, pool_scale,
                   w_pool_branch, w_out, norm2_g, w_rg, b_rg, w_re, b_re):
    c0 = 2 * KEY_W + 2 * VAL_W
    c1 = c0 + GATE_RANK
    c2 = c1 + POOL_W
    w_qkvr = w_in[:, :c0].astype(BF16)
    w_a = jnp.pad(w_in[:, c0:c1], ((0, 0), (0, LANE - GATE_RANK))).astype(BF16)
    w_u = w_in[:, c1:c2].astype(BF16)
    w_g = w_in[:, c2:].astype(BF16)
    w_alpha = jnp.pad(w_alpha_up, ((0, LANE - GATE_RANK), (0, 0))).astype(BF16)
    w_re_t = jnp.transpose(w_re, (0, 2, 1)).reshape(N_EXPERTS, D_MODEL)
    wrt = jnp.zeros((ROUTER_ROWS, D_MODEL), F32)
    wrt = wrt.at[0:N_GROUPS].set(w_rg.T).at[8:8 + N_EXPERTS].set(w_re_t).astype(BF16)
    brt = jnp.zeros((ROUTER_ROWS,), F32).at[0:N_GROUPS].set(b_rg).at[8:8 + N_EXPERTS].set(b_re.reshape(-1))
    brt = jnp.broadcast_to(brt[:, None], (ROUTER_ROWS, LANE))
    row = lambda a: a.reshape(1, -1).astype(F32)
    return (row(norm1_g), w_qkvr, w_a, w_u, w_g, w_alpha, row(b_alpha), row(gla_norm_g),
            w_gla_branch.astype(BF16), pool_w.astype(BF16), row(pool_scale), w_pool_branch.astype(BF16),
            w_out.astype(BF16), row(norm2_g), wrt, brt)


def kernel(x, norm1_g, w_in, w_alpha_up, b_alpha, gla_norm_g, w_gla_branch, pool_w, pool_scale,
           w_pool_branch, w_out, norm2_g, w_router_group, b_router_group, w_router_expert,
           b_router_expert, w_exp_gate, w_exp_up, w_exp_down, norm_f_g):
    B, S, D = x.shape
    T = B * S
    depth = w_in.shape[0]
    ts = _pick_tile(S, MIX_TS)
    td = _pick_tile(T, DISPATCH_TILE)
    tc = _pick_tile(T, COMBINE_TILE)
    n_assign = 2 * T
    n_blocks = -(-(n_assign + N_EXPERTS * (EXP_BM - 1)) // EXP_BM)
    n_rows = n_blocks * EXP_BM

    assert depth == 1, "kernel supports the problem's DEPTH=1"
    for l in range(depth):
        wts = _mixer_weights(norm1_g[l], w_in[l], w_alpha_up[l], b_alpha[l], gla_norm_g[l], w_gla_branch[l],
                             pool_w[l], pool_scale[l], w_pool_branch[l], w_out[l], norm2_g[l],
                             w_router_group[l], b_router_group[l], w_router_expert[l], b_router_expert[l])
        x1, h2, eid, gates, rank, cnt = _mixer_call(x, wts, ts)

        counts = cnt[:, 0].astype(jnp.int32)
        padded = ((counts + EXP_BM - 1) // EXP_BM) * EXP_BM
        pends = jnp.cumsum(padded)
        pstarts = pends - padded
        expert_col = jnp.arange(N_EXPERTS, dtype=jnp.int32)[:, None, None]
        dest = jnp.sum(jnp.where(eid[None] == expert_col, pstarts[:, None, None], 0), axis=0) + rank
        tile_dest = lambda t: dest.reshape(2, T // t, t).transpose(1, 0, 2).reshape(T // t, 1, 2 * t)
        dest_d = tile_dest(td)
        dest_c = tile_dest(tc)
        blk_start = jnp.arange(n_blocks, dtype=jnp.int32) * EXP_BM
        block_e = jnp.minimum(jnp.sum(pends[None, :] <= blk_start[:, None], axis=1), N_EXPERTS - 1).astype(jnp.int32)
        n_used = (pends[-1:] // EXP_BM).astype(jnp.int32)

        xs = _dispatch_call(h2.reshape(T, ROW_TILES, LANE), dest_d, pstarts.astype(jnp.int32), counts, n_rows)
        ys = _expert_call(xs.reshape(n_rows * ROW_TILES, LANE), block_e, n_used, w_exp_gate[l].astype(BF16),
                          w_exp_up[l].astype(BF16), w_exp_down[l].astype(BF16))
        out = _combine_call(x1.reshape(T, D), dest_c, gates, norm_f_g.reshape(1, D).astype(F32),
                            ys.reshape(n_rows, ROW_TILES, LANE))
        x = out.reshape(B, S, D)
    return x
```

```python
import jax
import jax.numpy as jnp
from jax import lax
from jax.experimental import pallas as pl
from jax.experimental.pallas import tpu as pltpu

F32 = jnp.float32
BF16 = jnp.bfloat16

D_MODEL = 1024
GLA_HEADS = 4
GLA_DK = 64
GLA_DV = 128
KEY_W = GLA_HEADS * GLA_DK
VAL_W = GLA_HEADS * GLA_DV
GATE_RANK = 16
GATE_NORMALIZER = 16.0
GLA_CHUNK = 64
POOL_WINDOWS = (2, 4, 8, 16)
POOL_W = 512
POOL_GW = 128
N_GROUPS = 4
EXPERTS_PER_GROUP = 8
N_EXPERTS = 32
D_EXPERT = 256
NORM_EPS = 1e-6

LANE = 128
SUBLANE = 8
POOL_HALO = 16
ROUTER_ROWS = 48

MIX_TS = 512
DISPATCH_TILE = 2048
COMBINE_TILE = 512
GLA_GROUP = 2
EXP_BM = 1024
VMEM_LIMIT = 56 * 1024 * 1024


def _dot(a, b):
    return jnp.dot(a, b, preferred_element_type=F32)


def _dot_nt(a, b):
    return lax.dot_general(a, b, (((1,), (1,)), ((), ())), preferred_element_type=F32)


def _dot_tn(a, b):
    return lax.dot_general(a, b, (((0,), (0,)), ((), ())), preferred_element_type=F32)


def _bf16_bits(x):
    return lax.bitcast_convert_type(x.astype(BF16).astype(F32), jnp.int32)


def _pack_rows(x):
    w = x.shape[1] // 2
    return lax.shift_right_logical(_bf16_bits(x[:, :w]), 16) | _bf16_bits(x[:, w:])


def _unpack_rows(words):
    lo = lax.bitcast_convert_type(lax.shift_left(words, 16), F32)
    hi = lax.bitcast_convert_type(words & -65536, F32)
    return lo, hi


ROW_TILES = D_MODEL // 2 // LANE


def _store_rows(ref, words):
    n = words.shape[0]
    for t in range(ROW_TILES):
        ref[pl.ds(t, n, stride=ROW_TILES), :] = words[:, t * LANE:(t + 1) * LANE]


def _load_rows(ref):
    n = ref.shape[0] // ROW_TILES
    return jnp.concatenate([ref[pl.ds(t, n, stride=ROW_TILES), :] for t in range(ROW_TILES)], axis=1)


def _chunk_cumsum(x, chunk):
    n, w = x.shape
    pos = lax.broadcasted_iota(jnp.int32, (n, w), 0) % chunk
    step = 1
    while step < chunk:
        if step < 8:
            shifted = pltpu.roll(x, step, axis=0)
        else:
            shifted = jnp.concatenate([jnp.zeros((step, w), x.dtype), x[:n - step]], axis=0)
        x = x + jnp.where(pos >= step, shifted, 0.0)
        step *= 2
    return x


def _mixer_kernel(x_ref, g1_ref, wqkvr_ref, wa_ref, wu_ref, wg_ref, walpha_ref, balpha_ref,
                  glag_ref, wglab_ref, poolw_ref, pscale_ref, wpoolb_ref, wout_ref, g2_ref,
                  wrt_ref, brt_ref,
                  x1_ref, h2_ref, eid_ref, gate_ref, rank_ref, cnt_ref,
                  state_s, carry_s, cnt_s, qdm_s, kem_s, kdt_s, v_s, dec_s, o_s, ypool_s, gates_s):
    ts = x_ref.shape[0]
    b_idx = pl.program_id(0)
    s_idx = pl.program_id(1)

    @pl.when(s_idx == 0)
    def _():
        state_s[...] = jnp.zeros_like(state_s)
        carry_s[...] = jnp.zeros_like(carry_s)

    @pl.when((b_idx == 0) & (s_idx == 0))
    def _():
        cnt_s[...] = jnp.zeros_like(cnt_s)

    xf = x_ref[...]
    ms = jnp.mean(xf * xf, axis=-1, keepdims=True)
    h = (xf * lax.rsqrt(ms + NORM_EPS) * g1_ref[...]).astype(BF16)

    qkvr = _dot(h, wqkvr_ref[...])
    q = qkvr[:, 0:KEY_W]
    k = qkvr[:, KEY_W:2 * KEY_W]
    v = qkvr[:, 2 * KEY_W:2 * KEY_W + VAL_W]
    r = qkvr[:, 2 * KEY_W + VAL_W:]

    a_low = _dot(h, wa_ref[...])
    z = _dot(a_low.astype(BF16), walpha_ref[...]) + balpha_ref[...]
    log_a = (jnp.minimum(z, 0.0) - jnp.log1p(jnp.exp(-jnp.abs(z)))) * (1.0 / GATE_NORMALIZER)

    u = _dot(h, wu_ref[...])
    ext = jnp.concatenate([carry_s[...], u], axis=0)
    carry_s[...] = u[ts - POOL_HALO:, :]
    pos = (s_idx * ts + lax.broadcasted_iota(jnp.int32, (ts, 1), 0)).astype(F32)
    mixed = []
    for gi, w in enumerate(POOL_WINDOWS):
        a = ext[:, gi * POOL_GW:(gi + 1) * POOL_GW]
        step = 1
        while step < w:
            a = a + pltpu.roll(a, step, axis=0)
            step *= 2
        pooled = a[POOL_HALO:, :] / jnp.minimum(pos + 1.0, float(w))
        diff = pooled - u[:, gi * POOL_GW:(gi + 1) * POOL_GW]
        mixed.append(_dot(diff.astype(BF16), poolw_ref[gi]))
    pm = (jnp.concatenate(mixed, axis=-1) * pscale_ref[...]).astype(BF16)
    ypool_s[...] = _dot(pm, wpoolb_ref[...])
    gates_s[...] = _dot(h, wg_ref[...])

    nc = ts // GLA_CHUNK
    b = _chunk_cumsum(log_a, GLA_CHUNK)
    b3 = b.reshape(nc, GLA_CHUNK, KEY_W)
    b_last = b3[:, GLA_CHUNK - 1:GLA_CHUNK, :]
    lane = lax.broadcasted_iota(jnp.int32, (ts, LANE), 1)
    qd = q * jnp.exp(b) * (GLA_DK ** -0.5)
    ke = (k.reshape(nc, GLA_CHUNK, KEY_W) * jnp.exp(b_last - b3)).reshape(ts, KEY_W)
    for hd in range(GLA_HEADS):
        pair = slice((hd // 2) * LANE, (hd // 2 + 1) * LANE)
        mine = (lane < GLA_DK) if hd % 2 == 0 else (lane >= GLA_DK)
        qdm_s[hd] = jnp.where(mine, qd[:, pair], 0.0).astype(BF16)
        kem_s[hd] = jnp.where(mine, ke[:, pair], 0.0).astype(BF16)
    v_s[...] = v.astype(BF16)
    grp = GLA_GROUP * GLA_CHUNK
    kd = k * jnp.exp(-b)
    for p in range(GLA_HEADS // 2):
        kd_t = kd[:, p * LANE:(p + 1) * LANE].T.astype(BF16)
        for j in range(ts // grp):
            kdt_s[p, j] = kd_t[:, j * grp:(j + 1) * grp]
    dec_rows = jnp.concatenate([jnp.exp(b_last[c]) for c in range(nc)]
                               + [jnp.zeros((LANE - nc, KEY_W), F32)], axis=0)
    dec_t = dec_rows.T
    for c in range(nc):
        for p in range(GLA_HEADS // 2):
            dec_s[c, p] = jnp.broadcast_to(dec_t[p * LANE:(p + 1) * LANE, c:c + 1], (LANE, GLA_DV))

    rg = lax.broadcasted_iota(jnp.int32, (grp, grp), 0)
    cg = lax.broadcasted_iota(jnp.int32, (grp, grp), 1)
    tri_g = (rg >= cg) & (rg // GLA_CHUNK == cg // GLA_CHUNK)

    for j in range(nc // GLA_GROUP):
        rows_g = slice(j * grp, (j + 1) * grp)
        for hd in range(GLA_HEADS):
            p = hd // 2
            cols = slice(hd * GLA_DV, (hd + 1) * GLA_DV)
            qm = qdm_s[hd, rows_g, :]
            vh = v_s[rows_g, cols]
            km = kem_s[hd, rows_g, :]
            scores = jnp.where(tri_g, _dot(qm, kdt_s[p, j]), 0.0)
            o_intra = _dot(scores.astype(BF16), vh)
            st = state_s[hd]
            outs = []
            for cc in range(GLA_GROUP):
                c = GLA_GROUP * j + cc
                part = slice(cc * GLA_CHUNK, (cc + 1) * GLA_CHUNK)
                outs.append(o_intra[part] + _dot(qm[part], st.astype(BF16)))
                st = dec_s[c, p] * st + _dot_tn(km[part], vh[part])
            state_s[hd] = st
            o_s[rows_g, cols] = jnp.concatenate(outs, axis=0)

    o_all = o_s[...]
    parts = []
    for hd in range(GLA_HEADS):
        oh = o_all[:, hd * GLA_DV:(hd + 1) * GLA_DV]
        parts.append(oh * lax.rsqrt(jnp.mean(oh * oh, axis=-1, keepdims=True) + NORM_EPS))
    o_n = jnp.concatenate(parts, axis=-1) * glag_ref[...]
    o_g = (o_n * (r * jax.nn.sigmoid(r))).astype(BF16)
    y_gla = _dot(o_g, wglab_ref[...])

    merged = (jax.nn.sigmoid(gates_s[:, :D_MODEL]) * y_gla
              + jax.nn.sigmoid(gates_s[:, D_MODEL:]) * ypool_s[...]).astype(BF16)
    x1 = xf + _dot(merged, wout_ref[...])
    x1_ref[...] = x1

    ms2 = jnp.mean(x1 * x1, axis=-1, keepdims=True)
    h2 = x1 * lax.rsqrt(ms2 + NORM_EPS) * g2_ref[...]
    _store_rows(h2_ref, _pack_rows(h2))
    lt = _dot_nt(wrt_ref[...], h2.astype(BF16)) + brt_ref[:, 0:1]

    l0, l1, l2, l3 = lt[0:1], lt[1:2], lt[2:3], lt[3:4]
    gm = jnp.maximum(jnp.maximum(l0, l1), jnp.maximum(l2, l3))
    gidx = jnp.where(l0 == gm, 0, jnp.where(l1 == gm, 1, jnp.where(l2 == gm, 2, 3)))
    gp = 1.0 / (jnp.exp(l0 - gm) + jnp.exp(l1 - gm) + jnp.exp(l2 - gm) + jnp.exp(l3 - gm))
    el = jnp.where(gidx == 0, lt[8:16],
                   jnp.where(gidx == 1, lt[16:24], jnp.where(gidx == 2, lt[24:32], lt[32:40])))
    row8 = lax.broadcasted_iota(jnp.int32, (EXPERTS_PER_GROUP, ts), 0)
    m1 = jnp.max(el, axis=0, keepdims=True)
    i1 = jnp.min(jnp.where(el == m1, row8, EXPERTS_PER_GROUP), axis=0, keepdims=True)
    el2 = jnp.where(row8 == i1, -jnp.inf, el)
    m2 = jnp.max(el2, axis=0, keepdims=True)
    i2 = jnp.min(jnp.where(el2 == m2, row8, EXPERTS_PER_GROUP), axis=0, keepdims=True)
    dd = jnp.exp(m2 - m1)
    p1 = 1.0 / (1.0 + dd)
    p2 = dd / (1.0 + dd)
    e1 = gidx * EXPERTS_PER_GROUP + i1
    e2 = gidx * EXPERTS_PER_GROUP + i2
    eid_ref[0:1, :] = e1
    eid_ref[1:2, :] = e2
    gate_ref[...] = jnp.concatenate([gp * p1, gp * p2, jnp.zeros((6, ts), F32)], axis=0)

    row32 = lax.broadcasted_iota(jnp.int32, (N_EXPERTS, ts), 0)
    oh1 = row32 == e1
    oh2 = row32 == e2
    member = jnp.where(oh1, 1.0, 0.0) + jnp.where(oh2, 1.0, 0.0)
    before = (lax.broadcasted_iota(jnp.int32, (ts, ts), 0)
              < lax.broadcasted_iota(jnp.int32, (ts, ts), 1))
    prefix = _dot(member.astype(BF16), jnp.where(before, 1.0, 0.0).astype(BF16))
    base = cnt_s[...]
    prefix = prefix + jnp.concatenate([base] * (ts // LANE), axis=1)
    rank_ref[0:1, :] = jnp.sum(jnp.where(oh1, prefix, 0.0), axis=0, keepdims=True).astype(jnp.int32)
    rank_ref[1:2, :] = jnp.sum(jnp.where(oh2, prefix, 0.0), axis=0, keepdims=True).astype(jnp.int32)
    new_cnt = base + jnp.sum(member, axis=1, keepdims=True)
    cnt_s[...] = new_cnt
    cnt_ref[...] = new_cnt


def _const_spec(shape):
    nd = len(shape)
    return pl.BlockSpec(shape, lambda b, s, _nd=nd: (0,) * _nd, pipeline_mode=pl.Buffered(1))


def _mixer_call(x, wts, ts):
    B, S, D = x.shape
    T = B * S
    ns = S // ts
    tok_spec = pl.BlockSpec((None, ts, D), lambda b, s: (b, s, 0))
    flat_tok = lambda rows: pl.BlockSpec((rows, ts), lambda b, s: (0, b * ns + s))
    in_specs = [tok_spec] + [_const_spec(w.shape) for w in wts]
    out_shape = (
        jax.ShapeDtypeStruct((B, S, D), F32),
        jax.ShapeDtypeStruct((B, S * ROW_TILES, LANE), jnp.int32),
        jax.ShapeDtypeStruct((2, T), jnp.int32),
        jax.ShapeDtypeStruct((8, T), F32),
        jax.ShapeDtypeStruct((2, T), jnp.int32),
        jax.ShapeDtypeStruct((N_EXPERTS, LANE), F32),
    )
    packed_spec = pl.BlockSpec((None, ts * ROW_TILES, LANE), lambda b, s: (b, s, 0))
    out_specs = (tok_spec, packed_spec, flat_tok(2), flat_tok(8), flat_tok(2),
                 pl.BlockSpec((N_EXPERTS, LANE), lambda b, s: (0, 0)))
    grp = GLA_GROUP * GLA_CHUNK
    scratch = [
        pltpu.VMEM((GLA_HEADS, LANE, GLA_DV), F32),
        pltpu.VMEM((POOL_HALO, POOL_W), F32),
        pltpu.VMEM((N_EXPERTS, LANE), F32),
        pltpu.VMEM((GLA_HEADS, ts, LANE), BF16),
        pltpu.VMEM((GLA_HEADS, ts, LANE), BF16),
        pltpu.VMEM((GLA_HEADS // 2, ts // grp, LANE, grp), BF16),
        pltpu.VMEM((ts, VAL_W), BF16),
        pltpu.VMEM((ts // GLA_CHUNK, GLA_HEADS // 2, LANE, GLA_DV), F32),
        pltpu.VMEM((ts, VAL_W), F32),
        pltpu.VMEM((ts, D), F32),
        pltpu.VMEM((ts, 2 * D), F32),
    ]
    return pl.pallas_call(
        _mixer_kernel,
        grid=(B, ns),
        in_specs=in_specs,
        out_specs=out_specs,
        out_shape=out_shape,
        scratch_shapes=scratch,
        compiler_params=pltpu.CompilerParams(
            dimension_semantics=("arbitrary", "arbitrary"), vmem_limit_bytes=VMEM_LIMIT),
        name="mixer",
    )(x, *wts)


_PAD_BITS = tuple(p for p in (1 << i for i in reversed(range((EXP_BM - 1).bit_length()))) if p >= SUBLANE)


def _dispatch_kernel(pst_ref, cnt_ref, dest_ref, h2_ref, xs_ref, zero_s, sem, zsem):
    td = h2_ref.shape[0]
    i = pl.program_id(0)

    def zero_copy(off, p):
        return pltpu.make_async_copy(zero_s.at[pl.ds(0, p)], xs_ref.at[pl.ds(off, p)], zsem)

    def for_each_pad_piece(e, fn):
        cnt = cnt_ref[e]
        start = pst_ref[e] + cnt
        end = start + (EXP_BM - cnt % EXP_BM) % EXP_BM
        aligned = jnp.minimum((start + SUBLANE - 1) // SUBLANE * SUBLANE, end)
        for j in range(SUBLANE - 1):
            @pl.when(start + j < aligned)
            def _(j=j):
                fn(zero_copy(start + j, 1))

        npad = end - aligned
        off = aligned
        for p in _PAD_BITS:
            hit = (npad & p) != 0

            @pl.when(hit)
            def _(off=off, p=p):
                fn(zero_copy(pl.multiple_of(off, SUBLANE), p))

            off = off + jnp.where(hit, p, 0)

    @pl.when(i == 0)
    def _():
        zero_s[...] = jnp.zeros_like(zero_s)

        def start_e(e, c):
            for_each_pad_piece(e, lambda cp: cp.start())
            return c

        def wait_e(e, c):
            for_each_pad_piece(e, lambda cp: cp.wait())
            return c

        lax.fori_loop(0, N_EXPERTS, start_e, 0)
        lax.fori_loop(0, N_EXPERTS, wait_e, 0)

        zrows = zero_s.shape[0]
        last_cnt = cnt_ref[N_EXPERTS - 1]
        used = pst_ref[N_EXPERTS - 1] + (last_cnt + EXP_BM - 1) // EXP_BM * EXP_BM
        first_piece = used // zrows
        n_pieces = xs_ref.shape[0] // zrows

        def tail_copy(t):
            return zero_copy(pl.multiple_of(t * zrows, zrows), zrows)

        lax.fori_loop(first_piece, n_pieces, lambda t, c: (tail_copy(t).start(), c)[1], 0)
        lax.fori_loop(first_piece, n_pieces, lambda t, c: (tail_copy(t).wait(), c)[1], 0)

    def issue(g, c):
        for u in range(SUBLANE):
            for kk in range(2):
                d = dest_ref[kk * td + g * SUBLANE + u]
                pltpu.make_async_copy(h2_ref.at[g * SUBLANE + u], xs_ref.at[d], sem).start(priority=kk)
        return c

    lax.fori_loop(0, td // SUBLANE, issue, 0)
    for kk in range(2):
        pltpu.make_async_copy(xs_ref.at[pl.ds(0, td)], xs_ref.at[pl.ds(td, td)], sem).wait()


def _dispatch_call(h2, dest3, pstarts, counts, n_rows):
    T, RT, _ = h2.shape
    nt, _, td2 = dest3.shape
    td = td2 // 2
    grid_spec = pltpu.PrefetchScalarGridSpec(
        num_scalar_prefetch=2,
        grid=(nt,),
        in_specs=[
            pl.BlockSpec((None, None, 2 * td), lambda i, *_: (i, 0, 0), memory_space=pltpu.SMEM),
            pl.BlockSpec((td, RT, LANE), lambda i, *_: (i, 0, 0)),
        ],
        out_specs=pl.BlockSpec(memory_space=pl.ANY),
        scratch_shapes=[
            pltpu.VMEM((_PAD_BITS[0], RT, LANE), h2.dtype),
            pltpu.SemaphoreType.DMA(()),
            pltpu.SemaphoreType.DMA(()),
        ],
    )
    return pl.pallas_call(
        _dispatch_kernel,
        grid_spec=grid_spec,
        out_shape=jax.ShapeDtypeStruct((n_rows, RT, LANE), h2.dtype),
        compiler_params=pltpu.CompilerParams(
            dimension_semantics=("arbitrary",), vmem_limit_bytes=VMEM_LIMIT, has_side_effects=True),
        name="dispatch",
    )(pstarts, counts, dest3, h2)


def _expert_kernel(be_ref, nu_ref, x_ref, wg_ref, wu_ref, wd_ref, y_ref):
    b = pl.program_id(0)

    @pl.when(b < nu_ref[0])
    def _():
        lo, hi = _unpack_rows(_load_rows(x_ref))
        xb = jnp.concatenate([lo.astype(BF16), hi.astype(BF16)], axis=1)
        g = _dot(xb, wg_ref[...])
        u = _dot(xb, wu_ref[...])
        hmid = (g * jax.nn.sigmoid(g) * u).astype(BF16)
        _store_rows(y_ref, _pack_rows(_dot(hmid, wd_ref[...])))

    @pl.when(b >= nu_ref[0])
    def _():
        y_ref[...] = jnp.zeros_like(y_ref)


def _expert_call(xs, block_e, n_used, wg, wu, wd):
    R = xs.shape[0] // ROW_TILES
    D = D_MODEL
    nb = R // EXP_BM
    blk = (EXP_BM * ROW_TILES, LANE)
    last = lambda b, nu: jnp.minimum(b, nu[0] - 1)
    grid_spec = pltpu.PrefetchScalarGridSpec(
        num_scalar_prefetch=2,
        grid=(nb,),
        in_specs=[
            pl.BlockSpec(blk, lambda b, be, nu: (last(b, nu), 0)),
            pl.BlockSpec((None, D, D_EXPERT), lambda b, be, nu: (be[last(b, nu)], 0, 0)),
            pl.BlockSpec((None, D, D_EXPERT), lambda b, be, nu: (be[last(b, nu)], 0, 0)),
            pl.BlockSpec((None, D_EXPERT, D), lambda b, be, nu: (be[last(b, nu)], 0, 0)),
        ],
        out_specs=pl.BlockSpec(blk, lambda b, be, nu: (b, 0)),
    )
    return pl.pallas_call(
        _expert_kernel,
        grid_spec=grid_spec,
        out_shape=jax.ShapeDtypeStruct(xs.shape, xs.dtype),
        compiler_params=pltpu.CompilerParams(
            dimension_semantics=("arbitrary",), vmem_limit_bytes=VMEM_LIMIT),
        name="experts",
    )(block_e, n_used, xs, wg, wu, wd)


def _combine_kernel(dcur_ref, dnext_ref, x1_ref, gate_ref, gf_ref, y_ref, out_ref, ybuf, sem):
    tf = x1_ref.shape[0]
    i = pl.program_id(0)
    n = pl.num_programs(0)
    tile_rows = SUBLANE * ROW_TILES

    def issue_all(dref, slot):
        def group(g, c):
            for u in range(SUBLANE):
                for kk in range(2):
                    d = dref[kk * tf + g * SUBLANE + u]
                    slab = pl.ds(pl.multiple_of(g * tile_rows, tile_rows) + u * ROW_TILES, ROW_TILES)
                    pltpu.make_async_copy(y_ref.at[d], ybuf.at[slot, kk, slab], sem.at[slot]).start(priority=kk)
            return c

        lax.fori_loop(0, tf // SUBLANE, group, 0)

    @pl.when(i == 0)
    def _():
        issue_all(dcur_ref, 0)

    for par in range(2):
        @pl.when((i % 2 == par) & (i + 1 < n))
        def _(par=par):
            issue_all(dnext_ref, 1 - par)

    slot = i % 2
    for kk in range(2):
        pltpu.make_async_copy(y_ref.at[pl.ds(0, tf)], y_ref.at[pl.ds(tf, tf)], sem.at[slot]).wait()

    g = jnp.concatenate([gate_ref[...], jnp.zeros((LANE - 8, tf), F32)], axis=0)
    gt = g.T
    w = ROW_TILES * LANE
    y0_lo, y0_hi = _unpack_rows(_load_rows(ybuf.at[slot, 0]))
    y1_lo, y1_hi = _unpack_rows(_load_rows(ybuf.at[slot, 1]))
    g0 = gt[:, 0:1]
    g1 = gt[:, 1:2]
    xo_lo = x1_ref[:, :w] + (g0 * y0_lo + g1 * y1_lo)
    xo_hi = x1_ref[:, w:] + (g0 * y0_hi + g1 * y1_hi)
    ms = (jnp.sum(xo_lo * xo_lo, axis=-1, keepdims=True)
          + jnp.sum(xo_hi * xo_hi, axis=-1, keepdims=True)) * (1.0 / (2 * w))
    scale = lax.rsqrt(ms + NORM_EPS)
    out_ref[:, :w] = xo_lo * scale * gf_ref[:, :w]
    out_ref[:, w:] = xo_hi * scale * gf_ref[:, w:]


def _combine_call(x1, dest3, gates, gf, ys):
    T, D = x1.shape
    nt, _, tf2 = dest3.shape
    tf = tf2 // 2
    return pl.pallas_call(
        _combine_kernel,
        grid=(nt,),
        in_specs=[
            pl.BlockSpec((None, None, 2 * tf), lambda i: (i, 0, 0), memory_space=pltpu.SMEM),
            pl.BlockSpec((None, None, 2 * tf), lambda i: (jnp.minimum(i + 1, nt - 1), 0, 0),
                         memory_space=pltpu.SMEM),
            pl.BlockSpec((tf, D), lambda i: (i, 0)),
            pl.BlockSpec((8, tf), lambda i: (0, i)),
            pl.BlockSpec((1, D), lambda i: (0, 0)),
            pl.BlockSpec(memory_space=pl.ANY),
        ],
        out_specs=pl.BlockSpec((tf, D), lambda i: (i, 0)),
        out_shape=jax.ShapeDtypeStruct((T, D), F32),
        scratch_shapes=[pltpu.VMEM((2, 2, tf * ROW_TILES, LANE), ys.dtype),
                        pltpu.SemaphoreType.DMA((2,))],
        compiler_params=pltpu.CompilerParams(
            dimension_semantics=("arbitrary",), vmem_limit_bytes=VMEM_LIMIT),
        name="combine",
    )(dest3, dest3, x1, gates, gf, ys)


def _pick_tile(n, want):
    t = min(want, n)
    while n % t:
        t //= 2
    return t


def _mixer_weights(norm1_g, w_in, w_alpha_up, b_alpha, gla_norm_g, w_gla_branch, pool_w, pool_scale,
                   w_pool_branch, w_out, norm2_g, w_rg, b_rg, w_re, b_re):
    c0 = 2 * KEY_W + 2 * VAL_W
    c1 = c0 + GATE_RANK
    c2 = c1 + POOL_W
    w_qkvr = w_in[:, :c0].astype(BF16)
    w_a = jnp.pad(w_in[:, c0:c1], ((0, 0), (0, LANE - GATE_RANK))).astype(BF16)
    w_u = w_in[:, c1:c2].astype(BF16)
    w_g = w_in[:, c2:].astype(BF16)
    w_alpha = jnp.pad(w_alpha_up, ((0, LANE - GATE_RANK), (0, 0))).astype(BF16)
    w_re_t = jnp.transpose(w_re, (0, 2, 1)).reshape(N_EXPERTS, D_MODEL)
    wrt = jnp.zeros((ROUTER_ROWS, D_MODEL), F32)
    wrt = wrt.at[0:N_GROUPS].set(w_rg.T).at[8:8 + N_EXPERTS].set(w_re_t).astype(BF16)
    brt = jnp.zeros((ROUTER_ROWS,), F32).at[0:N_GROUPS].set(b_rg).at[8:8 + N_EXPERTS].set(b_re.reshape(-1))
    brt = jnp.broadcast_to(brt[:, None], (ROUTER_ROWS, LANE))
    row = lambda a: a.reshape(1, -1).astype(F32)
    return (row(norm1_g), w_qkvr, w_a, w_u, w_g, w_alpha, row(b_alpha), row(gla_norm_g),
            w_gla_branch.astype(BF16), pool_w.astype(BF16), row(pool_scale), w_pool_branch.astype(BF16),
            w_out.astype(BF16), row(norm2_g), wrt, brt)


def kernel(x, norm1_g, w_in, w_alpha_up, b_alpha, gla_norm_g, w_gla_branch, pool_w, pool_scale,
           w_pool_branch, w_out, norm2_g, w_router_group, b_router_group, w_router_expert,
           b_router_expert, w_exp_gate, w_exp_up, w_exp_down, norm_f_g):
    B, S, D = x.shape
    T = B * S
    depth = w_in.shape[0]
    ts = _pick_tile(S, MIX_TS)
    td = _pick_tile(T, DISPATCH_TILE)
    tc = _pick_tile(T, COMBINE_TILE)
    n_assign = 2 * T
    n_blocks = -(-(n_assign + N_EXPERTS * (EXP_BM - 1)) // EXP_BM)
    n_rows = n_blocks * EXP_BM

    assert depth == 1, "kernel supports the problem's DEPTH=1"
    for l in range(depth):
        wts = _mixer_weights(norm1_g[l], w_in[l], w_alpha_up[l], b_alpha[l], gla_norm_g[l], w_gla_branch[l],
                             pool_w[l], pool_scale[l], w_pool_branch[l], w_out[l], norm2_g[l],
                             w_router_group[l], b_router_group[l], w_router_expert[l], b_router_expert[l])
        x1, h2, eid, gates, rank, cnt = _mixer_call(x, wts, ts)

        counts = cnt[:, 0].astype(jnp.int32)
        padded = ((counts + EXP_BM - 1) // EXP_BM) * EXP_BM
        pends = jnp.cumsum(padded)
        pstarts = pends - padded
        expert_col = jnp.arange(N_EXPERTS, dtype=jnp.int32)[:, None, None]
        dest = jnp.sum(jnp.where(eid[None] == expert_col, pstarts[:, None, None], 0), axis=0) + rank
        tile_dest = lambda t: dest.reshape(2, T // t, t).transpose(1, 0, 2).reshape(T // t, 1, 2 * t)
        dest_d = tile_dest(td)
        dest_c = tile_dest(tc)
        blk_start = jnp.arange(n_blocks, dtype=jnp.int32) * EXP_BM
        block_e = jnp.minimum(jnp.sum(pends[None, :] <= blk_start[:, None], axis=1), N_EXPERTS - 1).astype(jnp.int32)
        n_used = (pends[-1:] // EXP_BM).astype(jnp.int32)

        xs = _dispatch_call(h2.reshape(T, ROW_TILES, LANE), dest_d, pstarts.astype(jnp.int32), counts, n_rows)
        ys = _expert_call(xs.reshape(n_rows * ROW_TILES, LANE), block_e, n_used, w_exp_gate[l].astype(BF16),
                          w_exp_up[l].astype(BF16), w_exp_down[l].astype(BF16))
        out = _combine_call(x1.reshape(T, D), dest_c, gates, norm_f_g.reshape(1, D).astype(F32),
                            ys.reshape(n_rows, ROW_TILES, LANE))
        x = out.reshape(B, S, D)
    return x
```

```python
import jax
import jax.numpy as jnp
from jax import lax
from jax.experimental import pallas as pl
from jax.experimental.pallas import tpu as pltpu

F32 = jnp.float32
BF16 = jnp.bfloat16

D_MODEL = 1024
GLA_HEADS = 4
GLA_DK = 64
GLA_DV = 128
KEY_W = GLA_HEADS * GLA_DK
VAL_W = GLA_HEADS * GLA_DV
GATE_RANK = 16
GATE_NORMALIZER = 16.0
GLA_CHUNK = 64
POOL_WINDOWS = (2, 4, 8, 16)
POOL_W = 512
POOL_GW = 128
N_GROUPS = 4
EXPERTS_PER_GROUP = 8
N_EXPERTS = 32
D_EXPERT = 256
NORM_EPS = 1e-6

LANE = 128
SUBLANE = 8
POOL_HALO = 16
ROUTER_ROWS = 48

MIX_TS = 512
DISPATCH_TILE = 2048
COMBINE_TILE = 512
GLA_GROUP = 2
EXP_BM = 1024
VMEM_LIMIT = 56 * 1024 * 1024


def _dot(a, b):
    return jnp.dot(a, b, preferred_element_type=F32)


def _dot_nt(a, b):
    return lax.dot_general(a, b, (((1,), (1,)), ((), ())), preferred_element_type=F32)


def _dot_tn(a, b):
    return lax.dot_general(a, b, (((0,), (0,)), ((), ())), preferred_element_type=F32)


def _bf16_bits(x):
    return lax.bitcast_convert_type(x.astype(BF16).astype(F32), jnp.int32)


def _pack_rows(x):
    w = x.shape[1] // 2
    return lax.shift_right_logical(_bf16_bits(x[:, :w]), 16) | _bf16_bits(x[:, w:])


def _unpack_rows(words):
    lo = lax.bitcast_convert_type(lax.shift_left(words, 16), F32)
    hi = lax.bitcast_convert_type(words & -65536, F32)
    return lo, hi


ROW_TILES = D_MODEL // 2 // LANE


def _store_rows(ref, words):
    n = words.shape[0]
    for t in range(ROW_TILES):
        ref[pl.ds(t, n, stride=ROW_TILES), :] = words[:, t * LANE:(t + 1) * LANE]


def _load_rows(ref):
    n = ref.shape[0] // ROW_TILES
    return jnp.concatenate([ref[pl.ds(t, n, stride=ROW_TILES), :] for t in range(ROW_TILES)], axis=1)


def _chunk_cumsum(x, chunk):
    n, w = x.shape
    pos = lax.broadcasted_iota(jnp.int32, (n, w), 0) % chunk
    step = 1
    while step < chunk:
        if step < 8:
            shifted = pltpu.roll(x, step, axis=0)
        else:
            shifted = jnp.concatenate([jnp.zeros((step, w), x.dtype), x[:n - step]], axis=0)
        x = x + jnp.where(pos >= step, shifted, 0.0)
        step *= 2
    return x


def _mixer_kernel(x_ref, g1_ref, win_ref, walpha_ref, balpha_ref,
                  glag_ref, wglab_ref, poolw_ref, pscale_ref, wpoolb_ref, wout_ref, g2_ref,
                  wrt_ref, brt_ref,
                  x1_ref, h2_ref, eid_ref, gate_ref, rank_ref, cnt_ref,
                  state_s, carry_s, cnt_s, qdm_s, kem_s, kdt_s, v_s, dec_s, o_s, ypool_s, gates_s):
    ts = x_ref.shape[0]
    b_idx = pl.program_id(0)
    s_idx = pl.program_id(1)

    @pl.when(s_idx == 0)
    def _():
        state_s[...] = jnp.zeros_like(state_s)
        carry_s[...] = jnp.zeros_like(carry_s)

    @pl.when((b_idx == 0) & (s_idx == 0))
    def _():
        cnt_s[...] = jnp.zeros_like(cnt_s)

    xf = x_ref[...]
    ms = jnp.mean(xf * xf, axis=-1, keepdims=True)
    h = (xf * lax.rsqrt(ms + NORM_EPS) * g1_ref[...]).astype(BF16)

    c_a = 2 * KEY_W + 2 * VAL_W
    c_u = c_a + LANE
    c_g = c_u + POOL_W
    qkvr = _dot(h, win_ref[:, 0:c_a])
    q = qkvr[:, 0:KEY_W]
    k = qkvr[:, KEY_W:2 * KEY_W]
    v = qkvr[:, 2 * KEY_W:2 * KEY_W + VAL_W]
    r = qkvr[:, 2 * KEY_W + VAL_W:]

    a_low = _dot(h, win_ref[:, c_a:c_u])
    z = _dot(a_low.astype(BF16), walpha_ref[...]) + balpha_ref[...]
    log_a = (jnp.minimum(z, 0.0) - jnp.log1p(jnp.exp(-jnp.abs(z)))) * (1.0 / GATE_NORMALIZER)

    u = _dot(h, win_ref[:, c_u:c_g])
    ext = jnp.concatenate([carry_s[...], u], axis=0)
    carry_s[...] = u[ts - POOL_HALO:, :]
    pos = (s_idx * ts + lax.broadcasted_iota(jnp.int32, (ts, 1), 0)).astype(F32)
    mixed = []
    for gi, w in enumerate(POOL_WINDOWS):
        a = ext[:, gi * POOL_GW:(gi + 1) * POOL_GW]
        step = 1
        while step < w:
            a = a + pltpu.roll(a, step, axis=0)
            step *= 2
        pooled = a[POOL_HALO:, :] / jnp.minimum(pos + 1.0, float(w))
        diff = pooled - u[:, gi * POOL_GW:(gi + 1) * POOL_GW]
        mixed.append(_dot(diff.astype(BF16), poolw_ref[gi]))
    pm = (jnp.concatenate(mixed, axis=-1) * pscale_ref[...]).astype(BF16)
    ypool_s[...] = _dot(pm, wpoolb_ref[...])
    gates_s[...] = _dot(h, win_ref[:, c_g:])

    nc = ts // GLA_CHUNK
    b = _chunk_cumsum(log_a, GLA_CHUNK)
    b3 = b.reshape(nc, GLA_CHUNK, KEY_W)
    b_last = b3[:, GLA_CHUNK - 1:GLA_CHUNK, :]
    lane = lax.broadcasted_iota(jnp.int32, (ts, LANE), 1)
    qd = q * jnp.exp(b) * (GLA_DK ** -0.5)
    ke = (k.reshape(nc, GLA_CHUNK, KEY_W) * jnp.exp(b_last - b3)).reshape(ts, KEY_W)
    for hd in range(GLA_HEADS):
        pair = slice((hd // 2) * LANE, (hd // 2 + 1) * LANE)
        mine = (lane < GLA_DK) if hd % 2 == 0 else (lane >= GLA_DK)
        qdm_s[hd] = jnp.where(mine, qd[:, pair], 0.0).astype(BF16)
        kem_s[hd] = jnp.where(mine, ke[:, pair], 0.0).astype(BF16)
    v_s[...] = v.astype(BF16)
    grp = GLA_GROUP * GLA_CHUNK
    kd = k * jnp.exp(-b)
    for p in range(GLA_HEADS // 2):
        kd_t = kd[:, p * LANE:(p + 1) * LANE].T.astype(BF16)
        for j in range(ts // grp):
            kdt_s[p, j] = kd_t[:, j * grp:(j + 1) * grp]
    dec_rows = jnp.concatenate([jnp.exp(b_last[c]) for c in range(nc)]
                               + [jnp.zeros((LANE - nc, KEY_W), F32)], axis=0)
    dec_t = dec_rows.T
    for c in range(nc):
        for p in range(GLA_HEADS // 2):
            dec_s[c, p] = jnp.broadcast_to(dec_t[p * LANE:(p + 1) * LANE, c:c + 1], (LANE, GLA_DV))

    rg = lax.broadcasted_iota(jnp.int32, (grp, grp), 0)
    cg = lax.broadcasted_iota(jnp.int32, (grp, grp), 1)
    tri_g = (rg >= cg) & (rg // GLA_CHUNK == cg // GLA_CHUNK)

    for j in range(nc // GLA_GROUP):
        rows_g = slice(j * grp, (j + 1) * grp)
        for hd in range(GLA_HEADS):
            p = hd // 2
            cols = slice(hd * GLA_DV, (hd + 1) * GLA_DV)
            qm = qdm_s[hd, rows_g, :]
            vh = v_s[rows_g, cols]
            km = kem_s[hd, rows_g, :]
            scores = jnp.where(tri_g, _dot(qm, kdt_s[p, j]), 0.0)
            o_intra = _dot(scores.astype(BF16), vh)
            st = state_s[hd]
            outs = []
            for cc in range(GLA_GROUP):
                c = GLA_GROUP * j + cc
                part = slice(cc * GLA_CHUNK, (cc + 1) * GLA_CHUNK)
                outs.append(o_intra[part] + _dot(qm[part], st.astype(BF16)))
                st = dec_s[c, p] * st + _dot_tn(km[part], vh[part])
            state_s[hd] = st
            o_s[rows_g, cols] = jnp.concatenate(outs, axis=0)

    o_all = o_s[...]
    parts = []
    for hd in range(GLA_HEADS):
        oh = o_all[:, hd * GLA_DV:(hd + 1) * GLA_DV]
        parts.append(oh * lax.rsqrt(jnp.mean(oh * oh, axis=-1, keepdims=True) + NORM_EPS))
    o_n = jnp.concatenate(parts, axis=-1) * glag_ref[...]
    o_g = (o_n * (r * jax.nn.sigmoid(r))).astype(BF16)
    y_gla = _dot(o_g, wglab_ref[...])

    merged = (jax.nn.sigmoid(gates_s[:, :D_MODEL]) * y_gla
              + jax.nn.sigmoid(gates_s[:, D_MODEL:]) * ypool_s[...]).astype(BF16)
    x1 = xf + _dot(merged, wout_ref[...])
    x1_ref[...] = x1

    ms2 = jnp.mean(x1 * x1, axis=-1, keepdims=True)
    h2 = x1 * lax.rsqrt(ms2 + NORM_EPS) * g2_ref[...]
    _store_rows(h2_ref, _pack_rows(h2))
    lt = _dot_nt(wrt_ref[...], h2.astype(BF16)) + brt_ref[:, 0:1]

    l0, l1, l2, l3 = lt[0:1], lt[1:2], lt[2:3], lt[3:4]
    gm = jnp.maximum(jnp.maximum(l0, l1), jnp.maximum(l2, l3))
    gidx = jnp.where(l0 == gm, 0, jnp.where(l1 == gm, 1, jnp.where(l2 == gm, 2, 3)))
    gp = 1.0 / (jnp.exp(l0 - gm) + jnp.exp(l1 - gm) + jnp.exp(l2 - gm) + jnp.exp(l3 - gm))
    el = jnp.where(gidx == 0, lt[8:16],
                   jnp.where(gidx == 1, lt[16:24], jnp.where(gidx == 2, lt[24:32], lt[32:40])))
    row8 = lax.broadcasted_iota(jnp.int32, (EXPERTS_PER_GROUP, ts), 0)
    m1 = jnp.max(el, axis=0, keepdims=True)
    i1 = jnp.min(jnp.where(el == m1, row8, EXPERTS_PER_GROUP), axis=0, keepdims=True)
    el2 = jnp.where(row8 == i1, -jnp.inf, el)
    m2 = jnp.max(el2, axis=0, keepdims=True)
    i2 = jnp.min(jnp.where(el2 == m2, row8, EXPERTS_PER_GROUP), axis=0, keepdims=True)
    dd = jnp.exp(m2 - m1)
    p1 = 1.0 / (1.0 + dd)
    p2 = dd / (1.0 + dd)
    e1 = gidx * EXPERTS_PER_GROUP + i1
    e2 = gidx * EXPERTS_PER_GROUP + i2
    eid_ref[0:1, :] = e1
    eid_ref[1:2, :] = e2
    gate_ref[...] = jnp.concatenate([gp * p1, gp * p2, jnp.zeros((6, ts), F32)], axis=0)

    row32 = lax.broadcasted_iota(jnp.int32, (N_EXPERTS, ts), 0)
    oh1 = row32 == e1
    oh2 = row32 == e2
    member = jnp.where(oh1, 1.0, 0.0) + jnp.where(oh2, 1.0, 0.0)
    before = (lax.broadcasted_iota(jnp.int32, (ts, ts), 0)
              < lax.broadcasted_iota(jnp.int32, (ts, ts), 1))
    prefix = _dot(member.astype(BF16), jnp.where(before, 1.0, 0.0).astype(BF16))
    base = cnt_s[...]
    prefix = prefix + jnp.concatenate([base] * (ts // LANE), axis=1)
    rank_ref[0:1, :] = jnp.sum(jnp.where(oh1, prefix, 0.0), axis=0, keepdims=True).astype(jnp.int32)
    rank_ref[1:2, :] = jnp.sum(jnp.where(oh2, prefix, 0.0), axis=0, keepdims=True).astype(jnp.int32)
    new_cnt = base + jnp.sum(member, axis=1, keepdims=True)
    cnt_s[...] = new_cnt
    cnt_ref[...] = new_cnt


def _const_spec(shape):
    nd = len(shape)
    return pl.BlockSpec(shape, lambda b, s, _nd=nd: (0,) * _nd, pipeline_mode=pl.Buffered(1))


def _mixer_call(x, wts, ts):
    B, S, D = x.shape
    T = B * S
    ns = S // ts
    tok_spec = pl.BlockSpec((None, ts, D), lambda b, s: (b, s, 0))
    flat_tok = lambda rows: pl.BlockSpec((rows, ts), lambda b, s: (0, b * ns + s))
    in_specs = [tok_spec] + [_const_spec(w.shape) for w in wts]
    out_shape = (
        jax.ShapeDtypeStruct((B, S, D), F32),
        jax.ShapeDtypeStruct((B, S * ROW_TILES, LANE), jnp.int32),
        jax.ShapeDtypeStruct((2, T), jnp.int32),
        jax.ShapeDtypeStruct((8, T), F32),
        jax.ShapeDtypeStruct((2, T), jnp.int32),
        jax.ShapeDtypeStruct((N_EXPERTS, LANE), F32),
    )
    packed_spec = pl.BlockSpec((None, ts * ROW_TILES, LANE), lambda b, s: (b, s, 0))
    out_specs = (tok_spec, packed_spec, flat_tok(2), flat_tok(8), flat_tok(2),
                 pl.BlockSpec((N_EXPERTS, LANE), lambda b, s: (0, 0)))
    grp = GLA_GROUP * GLA_CHUNK
    scratch = [
        pltpu.VMEM((GLA_HEADS, LANE, GLA_DV), F32),
        pltpu.VMEM((POOL_HALO, POOL_W), F32),
        pltpu.VMEM((N_EXPERTS, LANE), F32),
        pltpu.VMEM((GLA_HEADS, ts, LANE), BF16),
        pltpu.VMEM((GLA_HEADS, ts, LANE), BF16),
        pltpu.VMEM((GLA_HEADS // 2, ts // grp, LANE, grp), BF16),
        pltpu.VMEM((ts, VAL_W), BF16),
        pltpu.VMEM((ts // GLA_CHUNK, GLA_HEADS // 2, LANE, GLA_DV), F32),
        pltpu.VMEM((ts, VAL_W), F32),
        pltpu.VMEM((ts, D), F32),
        pltpu.VMEM((ts, 2 * D), F32),
    ]
    return pl.pallas_call(
        _mixer_kernel,
        grid=(B, ns),
        in_specs=in_specs,
        out_specs=out_specs,
        out_shape=out_shape,
        scratch_shapes=scratch,
        compiler_params=pltpu.CompilerParams(
            dimension_semantics=("arbitrary", "arbitrary"), vmem_limit_bytes=VMEM_LIMIT),
        name="mixer",
    )(x, *wts)


_PAD_BITS = tuple(p for p in (1 << i for i in reversed(range((EXP_BM - 1).bit_length()))) if p >= SUBLANE)


def _dispatch_kernel(pst_ref, cnt_ref, dest_ref, h2_ref, xs_ref, zero_s, sem, zsem):
    td = h2_ref.shape[0]
    i = pl.program_id(0)

    def zero_copy(off, p):
        return pltpu.make_async_copy(zero_s.at[pl.ds(0, p)], xs_ref.at[pl.ds(off, p)], zsem)

    def for_each_pad_piece(e, fn):
        cnt = cnt_ref[e]
        start = pst_ref[e] + cnt
        end = start + (EXP_BM - cnt % EXP_BM) % EXP_BM
        aligned = jnp.minimum((start + SUBLANE - 1) // SUBLANE * SUBLANE, end)
        for j in range(SUBLANE - 1):
            @pl.when(start + j < aligned)
            def _(j=j):
                fn(zero_copy(start + j, 1))

        npad = end - aligned
        off = aligned
        for p in _PAD_BITS:
            hit = (npad & p) != 0

            @pl.when(hit)
            def _(off=off, p=p):
                fn(zero_copy(pl.multiple_of(off, SUBLANE), p))

            off = off + jnp.where(hit, p, 0)

    @pl.when(i == 0)
    def _():
        zero_s[...] = jnp.zeros_like(zero_s)

        def start_e(e, c):
            for_each_pad_piece(e, lambda cp: cp.start())
            return c

        def wait_e(e, c):
            for_each_pad_piece(e, lambda cp: cp.wait())
            return c

        lax.fori_loop(0, N_EXPERTS, start_e, 0)
        lax.fori_loop(0, N_EXPERTS, wait_e, 0)

        zrows = zero_s.shape[0]
        last_cnt = cnt_ref[N_EXPERTS - 1]
        used = pst_ref[N_EXPERTS - 1] + (last_cnt + EXP_BM - 1) // EXP_BM * EXP_BM
        first_piece = used // zrows
        n_pieces = xs_ref.shape[0] // zrows

        def tail_copy(t):
            return zero_copy(pl.multiple_of(t * zrows, zrows), zrows)

        lax.fori_loop(first_piece, n_pieces, lambda t, c: (tail_copy(t).start(), c)[1], 0)
        lax.fori_loop(first_piece, n_pieces, lambda t, c: (tail_copy(t).wait(), c)[1], 0)

    def issue(g, c):
        for u in range(SUBLANE):
            for kk in range(2):
                d = dest_ref[kk * td + g * SUBLANE + u]
                pltpu.make_async_copy(h2_ref.at[g * SUBLANE + u], xs_ref.at[d], sem).start(priority=kk)
        return c

    lax.fori_loop(0, td // SUBLANE, issue, 0)
    for kk in range(2):
        pltpu.make_async_copy(xs_ref.at[pl.ds(0, td)], xs_ref.at[pl.ds(td, td)], sem).wait()


def _dispatch_call(h2, dest3, pstarts, counts, n_rows):
    T, RT, _ = h2.shape
    nt, _, td2 = dest3.shape
    td = td2 // 2
    grid_spec = pltpu.PrefetchScalarGridSpec(
        num_scalar_prefetch=2,
        grid=(nt,),
        in_specs=[
            pl.BlockSpec((None, None, 2 * td), lambda i, *_: (i, 0, 0), memory_space=pltpu.SMEM),
            pl.BlockSpec((td, RT, LANE), lambda i, *_: (i, 0, 0)),
        ],
        out_specs=pl.BlockSpec(memory_space=pl.ANY),
        scratch_shapes=[
            pltpu.VMEM((_PAD_BITS[0], RT, LANE), h2.dtype),
            pltpu.SemaphoreType.DMA(()),
            pltpu.SemaphoreType.DMA(()),
        ],
    )
    return pl.pallas_call(
        _dispatch_kernel,
        grid_spec=grid_spec,
        out_shape=jax.ShapeDtypeStruct((n_rows, RT, LANE), h2.dtype),
        compiler_params=pltpu.CompilerParams(
            dimension_semantics=("arbitrary",), vmem_limit_bytes=VMEM_LIMIT, has_side_effects=True),
        name="dispatch",
    )(pstarts, counts, dest3, h2)


def _expert_kernel(be_ref, nu_ref, x_ref, wg_ref, wu_ref, wd_ref, y_ref):
    b = pl.program_id(0)

    @pl.when(b < nu_ref[0])
    def _():
        lo, hi = _unpack_rows(_load_rows(x_ref))
        xb = jnp.concatenate([lo.astype(BF16), hi.astype(BF16)], axis=1)
        g = _dot(xb, wg_ref[...])
        u = _dot(xb, wu_ref[...])
        hmid = (g * jax.nn.sigmoid(g) * u).astype(BF16)
        _store_rows(y_ref, _pack_rows(_dot(hmid, wd_ref[...])))

    @pl.when(b >= nu_ref[0])
    def _():
        y_ref[...] = jnp.zeros_like(y_ref)


def _expert_call(xs, block_e, n_used, wg, wu, wd):
    R = xs.shape[0] // ROW_TILES
    D = D_MODEL
    nb = R // EXP_BM
    blk = (EXP_BM * ROW_TILES, LANE)
    last = lambda b, nu: jnp.minimum(b, nu[0] - 1)
    grid_spec = pltpu.PrefetchScalarGridSpec(
        num_scalar_prefetch=2,
        grid=(nb,),
        in_specs=[
            pl.BlockSpec(blk, lambda b, be, nu: (last(b, nu), 0)),
            pl.BlockSpec((None, D, D_EXPERT), lambda b, be, nu: (be[last(b, nu)], 0, 0)),
            pl.BlockSpec((None, D, D_EXPERT), lambda b, be, nu: (be[last(b, nu)], 0, 0)),
            pl.BlockSpec((None, D_EXPERT, D), lambda b, be, nu: (be[last(b, nu)], 0, 0)),
        ],
        out_specs=pl.BlockSpec(blk, lambda b, be, nu: (b, 0)),
    )
    return pl.pallas_call(
        _expert_kernel,
        grid_spec=grid_spec,
        out_shape=jax.ShapeDtypeStruct(xs.shape, xs.dtype),
        compiler_params=pltpu.CompilerParams(
            dimension_semantics=("arbitrary",), vmem_limit_bytes=VMEM_LIMIT),
        name="experts",
    )(block_e, n_used, xs, wg, wu, wd)


def _combine_kernel(dcur_ref, dnext_ref, x1_ref, gate_ref, gf_ref, y_ref, out_ref, ybuf, sem):
    tf = x1_ref.shape[0]
    i = pl.program_id(0)
    n = pl.num_programs(0)
    tile_rows = SUBLANE * ROW_TILES

    def issue_all(dref, slot):
        def group(g, c):
            for u in range(SUBLANE):
                for kk in range(2):
                    d = dref[kk * tf + g * SUBLANE + u]
                    slab = pl.ds(pl.multiple_of(g * tile_rows, tile_rows) + u * ROW_TILES, ROW_TILES)
                    pltpu.make_async_copy(y_ref.at[d], ybuf.at[slot, kk, slab], sem.at[slot]).start(priority=kk)
            return c

        lax.fori_loop(0, tf // SUBLANE, group, 0)

    @pl.when(i == 0)
    def _():
        issue_all(dcur_ref, 0)

    for par in range(2):
        @pl.when((i % 2 == par) & (i + 1 < n))
        def _(par=par):
            issue_all(dnext_ref, 1 - par)

    slot = i % 2
    for kk in range(2):
        pltpu.make_async_copy(y_ref.at[pl.ds(0, tf)], y_ref.at[pl.ds(tf, tf)], sem.at[slot]).wait()

    g = jnp.concatenate([gate_ref[...], jnp.zeros((LANE - 8, tf), F32)], axis=0)
    gt = g.T
    w = ROW_TILES * LANE
    y0_lo, y0_hi = _unpack_rows(_load_rows(ybuf.at[slot, 0]))
    y1_lo, y1_hi = _unpack_rows(_load_rows(ybuf.at[slot, 1]))
    g0 = gt[:, 0:1]
    g1 = gt[:, 1:2]
    xo_lo = x1_ref[:, :w] + (g0 * y0_lo + g1 * y1_lo)
    xo_hi = x1_ref[:, w:] + (g0 * y0_hi + g1 * y1_hi)
    ms = (jnp.sum(xo_lo * xo_lo, axis=-1, keepdims=True)
          + jnp.sum(xo_hi * xo_hi, axis=-1, keepdims=True)) * (1.0 / (2 * w))
    scale = lax.rsqrt(ms + NORM_EPS)
    out_ref[:, :w] = xo_lo * scale * gf_ref[:, :w]
    out_ref[:, w:] = xo_hi * scale * gf_ref[:, w:]


def _combine_call(x1, dest3, gates, gf, ys):
    T, D = x1.shape
    nt, _, tf2 = dest3.shape
    tf = tf2 // 2
    return pl.pallas_call(
        _combine_kernel,
        grid=(nt,),
        in_specs=[
            pl.BlockSpec((None, None, 2 * tf), lambda i: (i, 0, 0), memory_space=pltpu.SMEM),
            pl.BlockSpec((None, None, 2 * tf), lambda i: (jnp.minimum(i + 1, nt - 1), 0, 0),
                         memory_space=pltpu.SMEM),
            pl.BlockSpec((tf, D), lambda i: (i, 0)),
            pl.BlockSpec((8, tf), lambda i: (0, i)),
            pl.BlockSpec((1, D), lambda i: (0, 0)),
            pl.BlockSpec(memory_space=pl.ANY),
        ],
        out_specs=pl.BlockSpec((tf, D), lambda i: (i, 0)),
        out_shape=jax.ShapeDtypeStruct((T, D), F32),
        scratch_shapes=[pltpu.VMEM((2, 2, tf * ROW_TILES, LANE), ys.dtype),
                        pltpu.SemaphoreType.DMA((2,))],
        compiler_params=pltpu.CompilerParams(
            dimension_semantics=("arbitrary",), vmem_limit_bytes=VMEM_LIMIT),
        name="combine",
    )(dest3, dest3, x1, gates, gf, ys)


def _pick_tile(n, want):
    t = min(want, n)
    while n % t:
        t //= 2
    return t


def _mixer_weights(norm1_g, w_in, w_alpha_up, b_alpha, gla_norm_g, w_gla_branch, pool_w, pool_scale,
                   w_pool_branch, w_out, norm2_g, w_rg, b_rg, w_re, b_re):
    c0 = 2 * KEY_W + 2 * VAL_W
    c1 = c0 + GATE_RANK
    c2 = c1 + POOL_W
    w_cat = jnp.concatenate([w_in[:, :c0], jnp.pad(w_in[:, c0:c1], ((0, 0), (0, LANE - GATE_RANK))),
                             w_in[:, c1:]], axis=1).astype(BF16)
    w_alpha = jnp.pad(w_alpha_up, ((0, LANE - GATE_RANK), (0, 0))).astype(BF16)
    w_re_t = jnp.transpose(w_re, (0, 2, 1)).reshape(N_EXPERTS, D_MODEL)
    wrt = jnp.zeros((ROUTER_ROWS, D_MODEL), F32)
    wrt = wrt.at[0:N_GROUPS].set(w_rg.T).at[8:8 + N_EXPERTS].set(w_re_t).astype(BF16)
    brt = jnp.zeros((ROUTER_ROWS,), F32).at[0:N_GROUPS].set(b_rg).at[8:8 + N_EXPERTS].set(b_re.reshape(-1))
    brt = jnp.broadcast_to(brt[:, None], (ROUTER_ROWS, LANE))
    row = lambda a: a.reshape(1, -1).astype(F32)
    return (row(norm1_g), w_cat, w_alpha, row(b_alpha), row(gla_norm_g),
            w_gla_branch.astype(BF16), pool_w.astype(BF16), row(pool_scale), w_pool_branch.astype(BF16),
            w_out.astype(BF16), row(norm2_g), wrt, brt)


def kernel(x, norm1_g, w_in, w_alpha_up, b_alpha, gla_norm_g, w_gla_branch, pool_w, pool_scale,
           w_pool_branch, w_out, norm2_g, w_router_group, b_router_group, w_router_expert,
           b_router_expert, w_exp_gate, w_exp_up, w_exp_down, norm_f_g):
    B, S, D = x.shape
    T = B * S
    depth = w_in.shape[0]
    ts = _pick_tile(S, MIX_TS)
    td = _pick_tile(T, DISPATCH_TILE)
    tc = _pick_tile(T, COMBINE_TILE)
    n_assign = 2 * T
    n_blocks = -(-(n_assign + N_EXPERTS * (EXP_BM - 1)) // EXP_BM)
    n_rows = n_blocks * EXP_BM

    assert depth == 1, "kernel supports the problem's DEPTH=1"
    for l in range(depth):
        wts = _mixer_weights(norm1_g[l], w_in[l], w_alpha_up[l], b_alpha[l], gla_norm_g[l], w_gla_branch[l],
                             pool_w[l], pool_scale[l], w_pool_branch[l], w_out[l], norm2_g[l],
                             w_router_group[l], b_router_group[l], w_router_expert[l], b_router_expert[l])
        x1, h2, eid, gates, rank, cnt = _mixer_call(x, wts, ts)

        counts = cnt[:, 0].astype(jnp.int32)
        padded = ((counts + EXP_BM - 1) // EXP_BM) * EXP_BM
        pends = jnp.cumsum(padded)
        pstarts = pends - padded
        expert_col = jnp.arange(N_EXPERTS, dtype=jnp.int32)[:, None, None]
        dest = jnp.sum(jnp.where(eid[None] == expert_col, pstarts[:, None, None], 0), axis=0) + rank
        tile_dest = lambda t: dest.reshape(2, T // t, t).transpose(1, 0, 2).reshape(T // t, 1, 2 * t)
        dest_d = tile_dest(td)
        dest_c = tile_dest(tc)
        blk_start = jnp.arange(n_blocks, dtype=jnp.int32) * EXP_BM
        block_e = jnp.minimum(jnp.sum(pends[None, :] <= blk_start[:, None], axis=1), N_EXPERTS - 1).astype(jnp.int32)
        n_used = (pends[-1:] // EXP_BM).astype(jnp.int32)

        xs = _dispatch_call(h2.reshape(T, ROW_TILES, LANE), dest_d, pstarts.astype(jnp.int32), counts, n_rows)
        ys = _expert_call(xs.reshape(n_rows * ROW_TILES, LANE), block_e, n_used, w_exp_gate[l].astype(BF16),
                          w_exp_up[l].astype(BF16), w_exp_down[l].astype(BF16))
        out = _combine_call(x1.reshape(T, D), dest_c, gates, norm_f_g.reshape(1, D).astype(F32),
                            ys.reshape(n_rows, ROW_TILES, LANE))
        x = out.reshape(B, S, D)
    return x
```

```python
import jax
import jax.numpy as jnp
from jax import lax
from jax.experimental import pallas as pl
from jax.experimental.pallas import tpu as pltpu

F32 = jnp.float32
BF16 = jnp.bfloat16

D_MODEL = 1024
GLA_HEADS = 4
GLA_DK = 64
GLA_DV = 128
KEY_W = GLA_HEADS * GLA_DK
VAL_W = GLA_HEADS * GLA_DV
GATE_RANK = 16
GATE_NORMALIZER = 16.0
GLA_CHUNK = 64
POOL_WINDOWS = (2, 4, 8, 16)
POOL_W = 512
POOL_GW = 128
N_GROUPS = 4
EXPERTS_PER_GROUP = 8
N_EXPERTS = 32
D_EXPERT = 256
NORM_EPS = 1e-6

LANE = 128
SUBLANE = 8
POOL_HALO = 16
ROUTER_ROWS = 48

MIX_TS = 512
DISPATCH_TILE = 2048
COMBINE_TILE = 512
GLA_GROUP = 2
EXP_BM = 1024
VMEM_LIMIT = 56 * 1024 * 1024


def _dot(a, b):
    return jnp.dot(a, b, preferred_element_type=F32)


def _dot_nt(a, b):
    return lax.dot_general(a, b, (((1,), (1,)), ((), ())), preferred_element_type=F32)


def _dot_tn(a, b):
    return lax.dot_general(a, b, (((0,), (0,)), ((), ())), preferred_element_type=F32)


def _bf16_bits(x):
    return lax.bitcast_convert_type(x.astype(BF16).astype(F32), jnp.int32)


def _pack_rows(x):
    w = x.shape[1] // 2
    return lax.shift_right_logical(_bf16_bits(x[:, :w]), 16) | _bf16_bits(x[:, w:])


def _unpack_rows(words):
    lo = lax.bitcast_convert_type(lax.shift_left(words, 16), F32)
    hi = lax.bitcast_convert_type(words & -65536, F32)
    return lo, hi


ROW_TILES = D_MODEL // 2 // LANE


def _store_rows(ref, words):
    n = words.shape[0]
    for t in range(ROW_TILES):
        ref[pl.ds(t, n, stride=ROW_TILES), :] = words[:, t * LANE:(t + 1) * LANE]


def _load_rows(ref):
    n = ref.shape[0] // ROW_TILES
    return jnp.concatenate([ref[pl.ds(t, n, stride=ROW_TILES), :] for t in range(ROW_TILES)], axis=1)


def _chunk_cumsum(x, chunk):
    n, w = x.shape
    pos = lax.broadcasted_iota(jnp.int32, (n, w), 0) % chunk
    step = 1
    while step < chunk:
        if step < 8:
            shifted = pltpu.roll(x, step, axis=0)
        else:
            shifted = jnp.concatenate([jnp.zeros((step, w), x.dtype), x[:n - step]], axis=0)
        x = x + jnp.where(pos >= step, shifted, 0.0)
        step *= 2
    return x


def _mixer_kernel(x_ref, g1_ref, win_ref, walpha_ref, balpha_ref,
                  glag_ref, wglab_ref, poolw_ref, pscale_ref, wpoolb_ref, wout_ref, g2_ref,
                  wrt_ref, brt_ref,
                  x1_ref, h2_ref, eid_ref, gate_ref, rank_ref, cnt_ref,
                  state_s, carry_s, cnt_s, qdm_s, kem_s, kdt_s, v_s, dec_s, o_s, ypool_s, gates_s):
    ts = x_ref.shape[0]
    b_idx = pl.program_id(0)
    s_idx = pl.program_id(1)

    @pl.when(s_idx == 0)
    def _():
        state_s[...] = jnp.zeros_like(state_s)
        carry_s[...] = jnp.zeros_like(carry_s)

    @pl.when((b_idx == 0) & (s_idx == 0))
    def _():
        cnt_s[...] = jnp.zeros_like(cnt_s)

    xf = x_ref[...]
    ms = jnp.mean(xf * xf, axis=-1, keepdims=True)
    h = (xf * lax.rsqrt(ms + NORM_EPS) * g1_ref[...]).astype(BF16)

    c_a = 2 * KEY_W + 2 * VAL_W
    c_u = c_a + LANE
    c_g = c_u + POOL_W
    qkvr = _dot(h, win_ref[:, 0:c_a])
    q = qkvr[:, 0:KEY_W]
    k = qkvr[:, KEY_W:2 * KEY_W]
    v = qkvr[:, 2 * KEY_W:2 * KEY_W + VAL_W]
    r = qkvr[:, 2 * KEY_W + VAL_W:]

    a_low = _dot(h, win_ref[:, c_a:c_u])
    z = _dot(a_low.astype(BF16), walpha_ref[...]) + balpha_ref[...]
    log_a = (jnp.minimum(z, 0.0) - jnp.log1p(jnp.exp(-jnp.abs(z)))) * (1.0 / GATE_NORMALIZER)

    u = _dot(h, win_ref[:, c_u:c_g])
    ext = jnp.concatenate([carry_s[...], u], axis=0)
    carry_s[...] = u[ts - POOL_HALO:, :]
    pos = (s_idx * ts + lax.broadcasted_iota(jnp.int32, (ts, 1), 0)).astype(F32)
    mixed = []
    for gi, w in enumerate(POOL_WINDOWS):
        a = ext[:, gi * POOL_GW:(gi + 1) * POOL_GW]
        step = 1
        while step < w:
            a = a + pltpu.roll(a, step, axis=0)
            step *= 2
        pooled = a[POOL_HALO:, :] / jnp.minimum(pos + 1.0, float(w))
        diff = pooled - u[:, gi * POOL_GW:(gi + 1) * POOL_GW]
        mixed.append(_dot(diff.astype(BF16), poolw_ref[gi]))
    pm = (jnp.concatenate(mixed, axis=-1) * pscale_ref[...]).astype(BF16)
    ypool_s[...] = _dot(pm, wpoolb_ref[...])
    gates_s[...] = _dot(h, win_ref[:, c_g:])

    nc = ts // GLA_CHUNK
    b = _chunk_cumsum(log_a, GLA_CHUNK)
    b3 = b.reshape(nc, GLA_CHUNK, KEY_W)
    b_last = b3[:, GLA_CHUNK - 1:GLA_CHUNK, :]
    lane = lax.broadcasted_iota(jnp.int32, (ts, LANE), 1)
    qd = q * jnp.exp(b) * (GLA_DK ** -0.5)
    ke = (k.reshape(nc, GLA_CHUNK, KEY_W) * jnp.exp(b_last - b3)).reshape(ts, KEY_W)
    for hd in range(GLA_HEADS):
        pair = slice((hd // 2) * LANE, (hd // 2 + 1) * LANE)
        mine = (lane < GLA_DK) if hd % 2 == 0 else (lane >= GLA_DK)
        qdm_s[hd] = jnp.where(mine, qd[:, pair], 0.0).astype(BF16)
        kem_s[hd] = jnp.where(mine, ke[:, pair], 0.0).astype(BF16)
    v_s[...] = v.astype(BF16)
    grp = GLA_GROUP * GLA_CHUNK
    kd = k * jnp.exp(-b)
    for p in range(GLA_HEADS // 2):
        kd_t = kd[:, p * LANE:(p + 1) * LANE].T.astype(BF16)
        for j in range(ts // grp):
            kdt_s[p, j] = kd_t[:, j * grp:(j + 1) * grp]
    dec_rows = jnp.concatenate([jnp.exp(b_last[c]) for c in range(nc)]
                               + [jnp.zeros((LANE - nc, KEY_W), F32)], axis=0)
    dec_t = dec_rows.T
    for c in range(nc):
        for p in range(GLA_HEADS // 2):
            dec_s[c, p] = jnp.broadcast_to(dec_t[p * LANE:(p + 1) * LANE, c:c + 1], (LANE, GLA_DV))

    rg = lax.broadcasted_iota(jnp.int32, (grp, grp), 0)
    cg = lax.broadcasted_iota(jnp.int32, (grp, grp), 1)
    tri_g = (rg >= cg) & (rg // GLA_CHUNK == cg // GLA_CHUNK)

    for j in range(nc // GLA_GROUP):
        rows_g = slice(j * grp, (j + 1) * grp)
        for hd in range(GLA_HEADS):
            p = hd // 2
            cols = slice(hd * GLA_DV, (hd + 1) * GLA_DV)
            qm = qdm_s[hd, rows_g, :]
            vh = v_s[rows_g, cols]
            km = kem_s[hd, rows_g, :]
            scores = jnp.where(tri_g, _dot(qm, kdt_s[p, j]), 0.0)
            o_intra = _dot(scores.astype(BF16), vh)
            st = state_s[hd]
            outs = []
            for cc in range(GLA_GROUP):
                c = GLA_GROUP * j + cc
                part = slice(cc * GLA_CHUNK, (cc + 1) * GLA_CHUNK)
                outs.append(o_intra[part] + _dot(qm[part], st.astype(BF16)))
                st = dec_s[c, p] * st + _dot_tn(km[part], vh[part])
            state_s[hd] = st
            o_s[rows_g, cols] = jnp.concatenate(outs, axis=0)

    o_all = o_s[...]
    parts = []
    for hd in range(GLA_HEADS):
        oh = o_all[:, hd * GLA_DV:(hd + 1) * GLA_DV]
        parts.append(oh * lax.rsqrt(jnp.mean(oh * oh, axis=-1, keepdims=True) + NORM_EPS))
    o_n = jnp.concatenate(parts, axis=-1) * glag_ref[...]
    o_g = (o_n * (r * jax.nn.sigmoid(r))).astype(BF16)
    y_gla = _dot(o_g, wglab_ref[...])

    merged = (jax.nn.sigmoid(gates_s[:, :D_MODEL]) * y_gla
              + jax.nn.sigmoid(gates_s[:, D_MODEL:]) * ypool_s[...]).astype(BF16)
    x1 = xf + _dot(merged, wout_ref[...])
    x1_ref[...] = x1

    ms2 = jnp.mean(x1 * x1, axis=-1, keepdims=True)
    h2 = x1 * lax.rsqrt(ms2 + NORM_EPS) * g2_ref[...]
    _store_rows(h2_ref, _pack_rows(h2))
    lt = _dot_nt(wrt_ref[...], h2.astype(BF16)) + brt_ref[:, 0:1]

    l0, l1, l2, l3 = lt[0:1], lt[1:2], lt[2:3], lt[3:4]
    gm = jnp.maximum(jnp.maximum(l0, l1), jnp.maximum(l2, l3))
    gidx = jnp.where(l0 == gm, 0, jnp.where(l1 == gm, 1, jnp.where(l2 == gm, 2, 3)))
    gp = 1.0 / (jnp.exp(l0 - gm) + jnp.exp(l1 - gm) + jnp.exp(l2 - gm) + jnp.exp(l3 - gm))
    el = jnp.where(gidx == 0, lt[8:16],
                   jnp.where(gidx == 1, lt[16:24], jnp.where(gidx == 2, lt[24:32], lt[32:40])))
    row8 = lax.broadcasted_iota(jnp.int32, (EXPERTS_PER_GROUP, ts), 0)
    m1 = jnp.max(el, axis=0, keepdims=True)
    i1 = jnp.min(jnp.where(el == m1, row8, EXPERTS_PER_GROUP), axis=0, keepdims=True)
    el2 = jnp.where(row8 == i1, -jnp.inf, el)
    m2 = jnp.max(el2, axis=0, keepdims=True)
    i2 = jnp.min(jnp.where(el2 == m2, row8, EXPERTS_PER_GROUP), axis=0, keepdims=True)
    dd = jnp.exp(m2 - m1)
    p1 = 1.0 / (1.0 + dd)
    p2 = dd / (1.0 + dd)
    e1 = gidx * EXPERTS_PER_GROUP + i1
    e2 = gidx * EXPERTS_PER_GROUP + i2
    eid_ref[0:1, :] = e1
    eid_ref[1:2, :] = e2
    gate_ref[...] = jnp.concatenate([gp * p1, gp * p2, jnp.zeros((6, ts), F32)], axis=0)

    row32 = lax.broadcasted_iota(jnp.int32, (N_EXPERTS, ts), 0)
    oh1 = row32 == e1
    oh2 = row32 == e2
    member = jnp.where(oh1, 1.0, 0.0) + jnp.where(oh2, 1.0, 0.0)
    before = (lax.broadcasted_iota(jnp.int32, (ts, ts), 0)
              < lax.broadcasted_iota(jnp.int32, (ts, ts), 1))
    prefix = _dot(member.astype(BF16), jnp.where(before, 1.0, 0.0).astype(BF16))
    base = cnt_s[...]
    prefix = prefix + jnp.concatenate([base] * (ts // LANE), axis=1)
    rank_ref[0:1, :] = jnp.sum(jnp.where(oh1, prefix, 0.0), axis=0, keepdims=True).astype(jnp.int32)
    rank_ref[1:2, :] = jnp.sum(jnp.where(oh2, prefix, 0.0), axis=0, keepdims=True).astype(jnp.int32)
    new_cnt = base + jnp.sum(member, axis=1, keepdims=True)
    cnt_s[...] = new_cnt
    cnt_ref[...] = new_cnt


def _const_spec(shape):
    nd = len(shape)
    return pl.BlockSpec(shape, lambda b, s, _nd=nd: (0,) * _nd, pipeline_mode=pl.Buffered(1))


def _mixer_call(x, wts, ts):
    B, S, D = x.shape
    T = B * S
    ns = S // ts
    tok_spec = pl.BlockSpec((None, ts, D), lambda b, s: (b, s, 0))
    flat_tok = lambda rows: pl.BlockSpec((rows, ts), lambda b, s: (0, b * ns + s))
    in_specs = [tok_spec] + [_const_spec(w.shape) for w in wts]
    out_shape = (
        jax.ShapeDtypeStruct((B, S, D), F32),
        jax.ShapeDtypeStruct((B, S * ROW_TILES, LANE), jnp.int32),
        jax.ShapeDtypeStruct((2, T), jnp.int32),
        jax.ShapeDtypeStruct((8, T), F32),
        jax.ShapeDtypeStruct((2, T), jnp.int32),
        jax.ShapeDtypeStruct((N_EXPERTS, LANE), F32),
    )
    packed_spec = pl.BlockSpec((None, ts * ROW_TILES, LANE), lambda b, s: (b, s, 0))
    out_specs = (tok_spec, packed_spec, flat_tok(2), flat_tok(8), flat_tok(2),
                 pl.BlockSpec((N_EXPERTS, LANE), lambda b, s: (0, 0)))
    grp = GLA_GROUP * GLA_CHUNK
    scratch = [
        pltpu.VMEM((GLA_HEADS, LANE, GLA_DV), F32),
        pltpu.VMEM((POOL_HALO, POOL_W), F32),
        pltpu.VMEM((N_EXPERTS, LANE), F32),
        pltpu.VMEM((GLA_HEADS, ts, LANE), BF16),
        pltpu.VMEM((GLA_HEADS, ts, LANE), BF16),
        pltpu.VMEM((GLA_HEADS // 2, ts // grp, LANE, grp), BF16),
        pltpu.VMEM((ts, VAL_W), BF16),
        pltpu.VMEM((ts // GLA_CHUNK, GLA_HEADS // 2, LANE, GLA_DV), F32),
        pltpu.VMEM((ts, VAL_W), F32),
        pltpu.VMEM((ts, D), F32),
        pltpu.VMEM((ts, 2 * D), F32),
    ]
    return pl.pallas_call(
        _mixer_kernel,
        grid=(B, ns),
        in_specs=in_specs,
        out_specs=out_specs,
        out_shape=out_shape,
        scratch_shapes=scratch,
        compiler_params=pltpu.CompilerParams(
            dimension_semantics=("arbitrary", "arbitrary"), vmem_limit_bytes=VMEM_LIMIT),
        name="mixer",
    )(x, *wts)


_PAD_BITS = tuple(p for p in (1 << i for i in reversed(range((EXP_BM - 1).bit_length()))) if p >= SUBLANE)


def _dispatch_kernel(pst_ref, cnt_ref, dest_ref, h2_ref, xs_ref, zero_s, sem, zsem):
    td = h2_ref.shape[0]
    i = pl.program_id(0)

    def zero_copy(off, p):
        return pltpu.make_async_copy(zero_s.at[pl.ds(0, p)], xs_ref.at[pl.ds(off, p)], zsem)

    def for_each_pad_piece(e, fn):
        cnt = cnt_ref[e]
        start = pst_ref[e] + cnt
        end = start + (EXP_BM - cnt % EXP_BM) % EXP_BM
        aligned = jnp.minimum((start + SUBLANE - 1) // SUBLANE * SUBLANE, end)
        for j in range(SUBLANE - 1):
            @pl.when(start + j < aligned)
            def _(j=j):
                fn(zero_copy(start + j, 1))

        npad = end - aligned
        off = aligned
        for p in _PAD_BITS:
            hit = (npad & p) != 0

            @pl.when(hit)
            def _(off=off, p=p):
                fn(zero_copy(pl.multiple_of(off, SUBLANE), p))

            off = off + jnp.where(hit, p, 0)

    @pl.when(i == 0)
    def _():
        zero_s[...] = jnp.zeros_like(zero_s)

        def start_e(e, c):
            for_each_pad_piece(e, lambda cp: cp.start())
            return c

        def wait_e(e, c):
            for_each_pad_piece(e, lambda cp: cp.wait())
            return c

        lax.fori_loop(0, N_EXPERTS, start_e, 0)
        lax.fori_loop(0, N_EXPERTS, wait_e, 0)

        zrows = zero_s.shape[0]
        last_cnt = cnt_ref[N_EXPERTS - 1]
        used = pst_ref[N_EXPERTS - 1] + (last_cnt + EXP_BM - 1) // EXP_BM * EXP_BM
        first_piece = used // zrows
        n_pieces = xs_ref.shape[0] // zrows

        def tail_copy(t):
            return zero_copy(pl.multiple_of(t * zrows, zrows), zrows)

        lax.fori_loop(first_piece, n_pieces, lambda t, c: (tail_copy(t).start(), c)[1], 0)
        lax.fori_loop(first_piece, n_pieces, lambda t, c: (tail_copy(t).wait(), c)[1], 0)

    def issue(g, c):
        for u in range(SUBLANE):
            for kk in range(2):
                d = dest_ref[kk * td + g * SUBLANE + u]
                pltpu.make_async_copy(h2_ref.at[g * SUBLANE + u], xs_ref.at[d], sem).start(priority=kk)
        return c

    lax.fori_loop(0, td // SUBLANE, issue, 0)
    for kk in range(2):
        pltpu.make_async_copy(xs_ref.at[pl.ds(0, td)], xs_ref.at[pl.ds(td, td)], sem).wait()


def _dispatch_call(h2, dest3, pstarts, counts, n_rows):
    T, RT, _ = h2.shape
    nt, _, td2 = dest3.shape
    td = td2 // 2
    grid_spec = pltpu.PrefetchScalarGridSpec(
        num_scalar_prefetch=2,
        grid=(nt,),
        in_specs=[
            pl.BlockSpec((None, None, 2 * td), lambda i, *_: (i, 0, 0), memory_space=pltpu.SMEM),
            pl.BlockSpec((td, RT, LANE), lambda i, *_: (i, 0, 0)),
        ],
        out_specs=pl.BlockSpec(memory_space=pl.ANY),
        scratch_shapes=[
            pltpu.VMEM((_PAD_BITS[0], RT, LANE), h2.dtype),
            pltpu.SemaphoreType.DMA(()),
            pltpu.SemaphoreType.DMA(()),
        ],
    )
    return pl.pallas_call(
        _dispatch_kernel,
        grid_spec=grid_spec,
        out_shape=jax.ShapeDtypeStruct((n_rows, RT, LANE), h2.dtype),
        compiler_params=pltpu.CompilerParams(
            dimension_semantics=("arbitrary",), vmem_limit_bytes=VMEM_LIMIT, has_side_effects=True),
        name="dispatch",
    )(pstarts, counts, dest3, h2)


def _expert_kernel(be_ref, nu_ref, x_ref, wg_ref, wu_ref, wd_ref, y_ref, wg_s, wu_s, wd_s):
    b = pl.program_id(0)
    used = b < nu_ref[0]

    @pl.when(used & ((b == 0) | (be_ref[b] != be_ref[jnp.maximum(b - 1, 0)])))
    def _():
        wg_s[...] = wg_ref[...].astype(BF16)
        wu_s[...] = wu_ref[...].astype(BF16)
        wd_s[...] = wd_ref[...].astype(BF16)

    @pl.when(used)
    def _():
        lo, hi = _unpack_rows(_load_rows(x_ref))
        xb = jnp.concatenate([lo.astype(BF16), hi.astype(BF16)], axis=1)
        g = _dot(xb, wg_s[...])
        u = _dot(xb, wu_s[...])
        hmid = (g * jax.nn.sigmoid(g) * u).astype(BF16)
        _store_rows(y_ref, _pack_rows(_dot(hmid, wd_s[...])))

    @pl.when(b >= nu_ref[0])
    def _():
        y_ref[...] = jnp.zeros_like(y_ref)


def _expert_call(xs, block_e, n_used, wg, wu, wd):
    R = xs.shape[0] // ROW_TILES
    D = D_MODEL
    nb = R // EXP_BM
    blk = (EXP_BM * ROW_TILES, LANE)
    last = lambda b, nu: jnp.minimum(b, nu[0] - 1)
    grid_spec = pltpu.PrefetchScalarGridSpec(
        num_scalar_prefetch=2,
        grid=(nb,),
        in_specs=[
            pl.BlockSpec(blk, lambda b, be, nu: (last(b, nu), 0)),
            pl.BlockSpec((None, D, D_EXPERT), lambda b, be, nu: (be[last(b, nu)], 0, 0)),
            pl.BlockSpec((None, D, D_EXPERT), lambda b, be, nu: (be[last(b, nu)], 0, 0)),
            pl.BlockSpec((None, D_EXPERT, D), lambda b, be, nu: (be[last(b, nu)], 0, 0)),
        ],
        out_specs=pl.BlockSpec(blk, lambda b, be, nu: (b, 0)),
        scratch_shapes=[pltpu.VMEM((D, D_EXPERT), BF16), pltpu.VMEM((D, D_EXPERT), BF16),
                        pltpu.VMEM((D_EXPERT, D), BF16)],
    )
    return pl.pallas_call(
        _expert_kernel,
        grid_spec=grid_spec,
        out_shape=jax.ShapeDtypeStruct(xs.shape, xs.dtype),
        compiler_params=pltpu.CompilerParams(
            dimension_semantics=("arbitrary",), vmem_limit_bytes=VMEM_LIMIT),
        name="experts",
    )(block_e, n_used, xs, wg, wu, wd)


def _combine_kernel(dcur_ref, dnext_ref, x1_ref, gate_ref, gf_ref, y_ref, out_ref, ybuf, sem):
    tf = x1_ref.shape[0]
    i = pl.program_id(0)
    n = pl.num_programs(0)
    tile_rows = SUBLANE * ROW_TILES

    def issue_all(dref, slot):
        def group(g, c):
            for u in range(SUBLANE):
                for kk in range(2):
                    d = dref[kk * tf + g * SUBLANE + u]
                    slab = pl.ds(pl.multiple_of(g * tile_rows, tile_rows) + u * ROW_TILES, ROW_TILES)
                    pltpu.make_async_copy(y_ref.at[d], ybuf.at[slot, kk, slab], sem.at[slot]).start(priority=kk)
            return c

        lax.fori_loop(0, tf // SUBLANE, group, 0)

    @pl.when(i == 0)
    def _():
        issue_all(dcur_ref, 0)

    for par in range(2):
        @pl.when((i % 2 == par) & (i + 1 < n))
        def _(par=par):
            issue_all(dnext_ref, 1 - par)

    slot = i % 2
    for kk in range(2):
        pltpu.make_async_copy(y_ref.at[pl.ds(0, tf)], y_ref.at[pl.ds(tf, tf)], sem.at[slot]).wait()

    g = jnp.concatenate([gate_ref[...], jnp.zeros((LANE - 8, tf), F32)], axis=0)
    gt = g.T
    w = ROW_TILES * LANE
    y0_lo, y0_hi = _unpack_rows(_load_rows(ybuf.at[slot, 0]))
    y1_lo, y1_hi = _unpack_rows(_load_rows(ybuf.at[slot, 1]))
    g0 = gt[:, 0:1]
    g1 = gt[:, 1:2]
    xo_lo = x1_ref[:, :w] + (g0 * y0_lo + g1 * y1_lo)
    xo_hi = x1_ref[:, w:] + (g0 * y0_hi + g1 * y1_hi)
    ms = (jnp.sum(xo_lo * xo_lo, axis=-1, keepdims=True)
          + jnp.sum(xo_hi * xo_hi, axis=-1, keepdims=True)) * (1.0 / (2 * w))
    scale = lax.rsqrt(ms + NORM_EPS)
    out_ref[:, :w] = xo_lo * scale * gf_ref[:, :w]
    out_ref[:, w:] = xo_hi * scale * gf_ref[:, w:]


def _combine_call(x1, dest3, gates, gf, ys):
    T, D = x1.shape
    nt, _, tf2 = dest3.shape
    tf = tf2 // 2
    return pl.pallas_call(
        _combine_kernel,
        grid=(nt,),
        in_specs=[
            pl.BlockSpec((None, None, 2 * tf), lambda i: (i, 0, 0), memory_space=pltpu.SMEM),
            pl.BlockSpec((None, None, 2 * tf), lambda i: (jnp.minimum(i + 1, nt - 1), 0, 0),
                         memory_space=pltpu.SMEM),
            pl.BlockSpec((tf, D), lambda i: (i, 0)),
            pl.BlockSpec((8, tf), lambda i: (0, i)),
            pl.BlockSpec((1, D), lambda i: (0, 0)),
            pl.BlockSpec(memory_space=pl.ANY),
        ],
        out_specs=pl.BlockSpec((tf, D), lambda i: (i, 0)),
        out_shape=jax.ShapeDtypeStruct((T, D), F32),
        scratch_shapes=[pltpu.VMEM((2, 2, tf * ROW_TILES, LANE), ys.dtype),
                        pltpu.SemaphoreType.DMA((2,))],
        compiler_params=pltpu.CompilerParams(
            dimension_semantics=("arbitrary",), vmem_limit_bytes=VMEM_LIMIT),
        name="combine",
    )(dest3, dest3, x1, gates, gf, ys)


def _pick_tile(n, want):
    t = min(want, n)
    while n % t:
        t //= 2
    return t


def _mixer_weights(norm1_g, w_in, w_alpha_up, b_alpha, gla_norm_g, w_gla_branch, pool_w, pool_scale,
                   w_pool_branch, w_out, norm2_g, w_rg, b_rg, w_re, b_re):
    c0 = 2 * KEY_W + 2 * VAL_W
    c1 = c0 + GATE_RANK
    c2 = c1 + POOL_W
    w_cat = jnp.concatenate([w_in[:, :c0], jnp.pad(w_in[:, c0:c1], ((0, 0), (0, LANE - GATE_RANK))),
                             w_in[:, c1:]], axis=1).astype(BF16)
    w_alpha = jnp.pad(w_alpha_up, ((0, LANE - GATE_RANK), (0, 0))).astype(BF16)
    w_re_t = jnp.transpose(w_re, (0, 2, 1)).reshape(N_EXPERTS, D_MODEL)
    wrt = jnp.zeros((ROUTER_ROWS, D_MODEL), F32)
    wrt = wrt.at[0:N_GROUPS].set(w_rg.T).at[8:8 + N_EXPERTS].set(w_re_t).astype(BF16)
    brt = jnp.zeros((ROUTER_ROWS,), F32).at[0:N_GROUPS].set(b_rg).at[8:8 + N_EXPERTS].set(b_re.reshape(-1))
    brt = jnp.broadcast_to(brt[:, None], (ROUTER_ROWS, LANE))
    row = lambda a: a.reshape(1, -1).astype(F32)
    return (row(norm1_g), w_cat, w_alpha, row(b_alpha), row(gla_norm_g),
            w_gla_branch.astype(BF16), pool_w.astype(BF16), row(pool_scale), w_pool_branch.astype(BF16),
            w_out.astype(BF16), row(norm2_g), wrt, brt)


def kernel(x, norm1_g, w_in, w_alpha_up, b_alpha, gla_norm_g, w_gla_branch, pool_w, pool_scale,
           w_pool_branch, w_out, norm2_g, w_router_group, b_router_group, w_router_expert,
           b_router_expert, w_exp_gate, w_exp_up, w_exp_down, norm_f_g):
    B, S, D = x.shape
    T = B * S
    depth = w_in.shape[0]
    ts = _pick_tile(S, MIX_TS)
    td = _pick_tile(T, DISPATCH_TILE)
    tc = _pick_tile(T, COMBINE_TILE)
    n_assign = 2 * T
    n_blocks = -(-(n_assign + N_EXPERTS * (EXP_BM - 1)) // EXP_BM)
    n_rows = n_blocks * EXP_BM

    assert depth == 1, "kernel supports the problem's DEPTH=1"
    for l in range(depth):
        wts = _mixer_weights(norm1_g[l], w_in[l], w_alpha_up[l], b_alpha[l], gla_norm_g[l], w_gla_branch[l],
                             pool_w[l], pool_scale[l], w_pool_branch[l], w_out[l], norm2_g[l],
                             w_router_group[l], b_router_group[l], w_router_expert[l], b_router_expert[l])
        x1, h2, eid, gates, rank, cnt = _mixer_call(x, wts, ts)

        counts = cnt[:, 0].astype(jnp.int32)
        padded = ((counts + EXP_BM - 1) // EXP_BM) * EXP_BM
        pends = jnp.cumsum(padded)
        pstarts = pends - padded
        expert_col = jnp.arange(N_EXPERTS, dtype=jnp.int32)[:, None, None]
        dest = jnp.sum(jnp.where(eid[None] == expert_col, pstarts[:, None, None], 0), axis=0) + rank
        tile_dest = lambda t: dest.reshape(2, T // t, t).transpose(1, 0, 2).reshape(T // t, 1, 2 * t)
        dest_d = tile_dest(td)
        dest_c = tile_dest(tc)
        blk_start = jnp.arange(n_blocks, dtype=jnp.int32) * EXP_BM
        block_e = jnp.minimum(jnp.sum(pends[None, :] <= blk_start[:, None], axis=1), N_EXPERTS - 1).astype(jnp.int32)
        n_used = (pends[-1:] // EXP_BM).astype(jnp.int32)

        xs = _dispatch_call(h2.reshape(T, ROW_TILES, LANE), dest_d, pstarts.astype(jnp.int32), counts, n_rows)
        ys = _expert_call(xs.reshape(n_rows * ROW_TILES, LANE), block_e, n_used,
                          w_exp_gate[l], w_exp_up[l], w_exp_down[l])
        out = _combine_call(x1.reshape(T, D), dest_c, gates, norm_f_g.reshape(1, D).astype(F32),
                            ys.reshape(n_rows, ROW_TILES, LANE))
        x = out.reshape(B, S, D)
    return x
```

```python
import jax
import jax.numpy as jnp
from jax import lax
from jax.experimental import pallas as pl
from jax.experimental.pallas import tpu as pltpu

F32 = jnp.float32
BF16 = jnp.bfloat16

D_MODEL = 1024
GLA_HEADS = 4
GLA_DK = 64
GLA_DV = 128
KEY_W = GLA_HEADS * GLA_DK
VAL_W = GLA_HEADS * GLA_DV
GATE_RANK = 16
GATE_NORMALIZER = 16.0
GLA_CHUNK = 64
POOL_WINDOWS = (2, 4, 8, 16)
POOL_W = 512
POOL_GW = 128
N_GROUPS = 4
EXPERTS_PER_GROUP = 8
N_EXPERTS = 32
D_EXPERT = 256
NORM_EPS = 1e-6

LANE = 128
SUBLANE = 8
POOL_HALO = 16
ROUTER_ROWS = 48

MIX_TS = 512
DISPATCH_TILE = 4096
COMBINE_TILE = 512
GLA_GROUP = 2
EXP_BM = 1024
V7X_VMEM_BYTES = 64 * 1024 * 1024
VMEM_LIMIT = V7X_VMEM_BYTES * 7 // 8


def _dot(a, b):
    return jnp.dot(a, b, preferred_element_type=F32)


def _dot_nt(a, b):
    return lax.dot_general(a, b, (((1,), (1,)), ((), ())), preferred_element_type=F32)


def _dot_tn(a, b):
    return lax.dot_general(a, b, (((0,), (0,)), ((), ())), preferred_element_type=F32)


def _bf16_bits(x):
    return lax.bitcast_convert_type(x.astype(BF16).astype(F32), jnp.int32)


def _pack_rows(x):
    w = x.shape[1] // 2
    return lax.shift_right_logical(_bf16_bits(x[:, :w]), 16) | _bf16_bits(x[:, w:])


def _unpack_rows(words):
    lo = lax.bitcast_convert_type(lax.shift_left(words, 16), F32)
    hi = lax.bitcast_convert_type(words & -65536, F32)
    return lo, hi


ROW_TILES = D_MODEL // 2 // LANE


def _store_rows(ref, words):
    n = words.shape[0]
    for t in range(ROW_TILES):
        ref[pl.ds(t, n, stride=ROW_TILES), :] = words[:, t * LANE:(t + 1) * LANE]


def _load_rows(ref):
    n = ref.shape[0] // ROW_TILES
    return jnp.concatenate([ref[pl.ds(t, n, stride=ROW_TILES), :] for t in range(ROW_TILES)], axis=1)


def _chunk_cumsum(x, chunk):
    n, w = x.shape
    pos = lax.broadcasted_iota(jnp.int32, (n, w), 0) % chunk
    step = 1
    while step < chunk:
        if step < 8:
            shifted = pltpu.roll(x, step, axis=0)
        else:
            shifted = jnp.concatenate([jnp.zeros((step, w), x.dtype), x[:n - step]], axis=0)
        x = x + jnp.where(pos >= step, shifted, 0.0)
        step *= 2
    return x


def _mixer_kernel(x_ref, g1_ref, win_ref, walpha_ref, balpha_ref,
                  glag_ref, wglab_ref, poolw_ref, pscale_ref, wpoolb_ref, wout_ref, g2_ref,
                  wrt_ref, brt_ref,
                  x1_ref, h2_ref, eid_ref, gate_ref, rank_ref, cnt_ref,
                  state_s, carry_s, cnt_s, qdm_s, kem_s, kdt_s, v_s, dec_s, o_s, ypool_s, gates_s):
    ts = x_ref.shape[0]
    b_idx = pl.program_id(0)
    s_idx = pl.program_id(1)

    @pl.when(s_idx == 0)
    def _():
        state_s[...] = jnp.zeros_like(state_s)
        carry_s[...] = jnp.zeros_like(carry_s)

    @pl.when((b_idx == 0) & (s_idx == 0))
    def _():
        cnt_s[...] = jnp.zeros_like(cnt_s)

    xf = x_ref[...]
    ms = jnp.mean(xf * xf, axis=-1, keepdims=True)
    h = (xf * lax.rsqrt(ms + NORM_EPS) * g1_ref[...]).astype(BF16)

    c_a = 2 * KEY_W + 2 * VAL_W
    c_u = c_a + LANE
    c_g = c_u + POOL_W
    qkvr = _dot(h, win_ref[:, 0:c_a])
    q = qkvr[:, 0:KEY_W]
    k = qkvr[:, KEY_W:2 * KEY_W]
    v = qkvr[:, 2 * KEY_W:2 * KEY_W + VAL_W]
    r = qkvr[:, 2 * KEY_W + VAL_W:]

    a_low = _dot(h, win_ref[:, c_a:c_u])
    z = _dot(a_low.astype(BF16), walpha_ref[...]) + balpha_ref[...]
    log_a = (jnp.minimum(z, 0.0) - jnp.log1p(jnp.exp(-jnp.abs(z)))) * (1.0 / GATE_NORMALIZER)

    u = _dot(h, win_ref[:, c_u:c_g])
    ext = jnp.concatenate([carry_s[...], u], axis=0)
    carry_s[...] = u[ts - POOL_HALO:, :]
    pos = (s_idx * ts + lax.broadcasted_iota(jnp.int32, (ts, 1), 0)).astype(F32)
    mixed = []
    for gi, w in enumerate(POOL_WINDOWS):
        a = ext[:, gi * POOL_GW:(gi + 1) * POOL_GW]
        step = 1
        while step < w:
            a = a + pltpu.roll(a, step, axis=0)
            step *= 2
        pooled = a[POOL_HALO:, :] / jnp.minimum(pos + 1.0, float(w))
        diff = pooled - u[:, gi * POOL_GW:(gi + 1) * POOL_GW]
        mixed.append(_dot(diff.astype(BF16), poolw_ref[gi]))
    pm = (jnp.concatenate(mixed, axis=-1) * pscale_ref[...]).astype(BF16)
    ypool_s[...] = _dot(pm, wpoolb_ref[...])
    gates_s[...] = _dot(h, win_ref[:, c_g:])

    nc = ts // GLA_CHUNK
    b = _chunk_cumsum(log_a, GLA_CHUNK)
    b3 = b.reshape(nc, GLA_CHUNK, KEY_W)
    b_last = b3[:, GLA_CHUNK - 1:GLA_CHUNK, :]
    lane = lax.broadcasted_iota(jnp.int32, (ts, LANE), 1)
    qd = q * jnp.exp(b) * (GLA_DK ** -0.5)
    ke = (k.reshape(nc, GLA_CHUNK, KEY_W) * jnp.exp(b_last - b3)).reshape(ts, KEY_W)
    for hd in range(GLA_HEADS):
        pair = slice((hd // 2) * LANE, (hd // 2 + 1) * LANE)
        mine = (lane < GLA_DK) if hd % 2 == 0 else (lane >= GLA_DK)
        qdm_s[hd] = jnp.where(mine, qd[:, pair], 0.0).astype(BF16)
        kem_s[hd] = jnp.where(mine, ke[:, pair], 0.0).astype(BF16)
    v_s[...] = v.astype(BF16)
    grp = GLA_GROUP * GLA_CHUNK
    kd = k * jnp.exp(-b)
    for p in range(GLA_HEADS // 2):
        kd_t = kd[:, p * LANE:(p + 1) * LANE].T.astype(BF16)
        for j in range(ts // grp):
            kdt_s[p, j] = kd_t[:, j * grp:(j + 1) * grp]
    dec_rows = jnp.concatenate([jnp.exp(b_last[c]) for c in range(nc)]
                               + [jnp.zeros((LANE - nc, KEY_W), F32)], axis=0)
    dec_t = dec_rows.T
    for c in range(nc):
        for p in range(GLA_HEADS // 2):
            dec_s[c, p] = jnp.broadcast_to(dec_t[p * LANE:(p + 1) * LANE, c:c + 1], (LANE, GLA_DV))

    rg = lax.broadcasted_iota(jnp.int32, (grp, grp), 0)
    cg = lax.broadcasted_iota(jnp.int32, (grp, grp), 1)
    tri_g = (rg >= cg) & (rg // GLA_CHUNK == cg // GLA_CHUNK)

    for j in range(nc // GLA_GROUP):
        rows_g = slice(j * grp, (j + 1) * grp)
        for hd in range(GLA_HEADS):
            p = hd // 2
            cols = slice(hd * GLA_DV, (hd + 1) * GLA_DV)
            qm = qdm_s[hd, rows_g, :]
            vh = v_s[rows_g, cols]
            km = kem_s[hd, rows_g, :]
            scores = jnp.where(tri_g, _dot(qm, kdt_s[p, j]), 0.0)
            o_intra = _dot(scores.astype(BF16), vh)
            st = state_s[hd]
            outs = []
            for cc in range(GLA_GROUP):
                c = GLA_GROUP * j + cc
                part = slice(cc * GLA_CHUNK, (cc + 1) * GLA_CHUNK)
                outs.append(o_intra[part] + _dot(qm[part], st.astype(BF16)))
                st = dec_s[c, p] * st + _dot_tn(km[part], vh[part])
            state_s[hd] = st
            o_s[rows_g, cols] = jnp.concatenate(outs, axis=0)

    o_all = o_s[...]
    parts = []
    for hd in range(GLA_HEADS):
        oh = o_all[:, hd * GLA_DV:(hd + 1) * GLA_DV]
        parts.append(oh * lax.rsqrt(jnp.mean(oh * oh, axis=-1, keepdims=True) + NORM_EPS))
    o_n = jnp.concatenate(parts, axis=-1) * glag_ref[...]
    o_g = (o_n * (r * jax.nn.sigmoid(r))).astype(BF16)
    y_gla = _dot(o_g, wglab_ref[...])

    merged = (jax.nn.sigmoid(gates_s[:, :D_MODEL]) * y_gla
              + jax.nn.sigmoid(gates_s[:, D_MODEL:]) * ypool_s[...]).astype(BF16)
    x1 = xf + _dot(merged, wout_ref[...])
    x1_ref[...] = x1

    ms2 = jnp.mean(x1 * x1, axis=-1, keepdims=True)
    h2 = x1 * lax.rsqrt(ms2 + NORM_EPS) * g2_ref[...]
    _store_rows(h2_ref, _pack_rows(h2))
    lt = _dot_nt(wrt_ref[...], h2.astype(BF16)) + brt_ref[:, 0:1]

    l0, l1, l2, l3 = lt[0:1], lt[1:2], lt[2:3], lt[3:4]
    gm = jnp.maximum(jnp.maximum(l0, l1), jnp.maximum(l2, l3))
    gidx = jnp.where(l0 == gm, 0, jnp.where(l1 == gm, 1, jnp.where(l2 == gm, 2, 3)))
    gp = 1.0 / (jnp.exp(l0 - gm) + jnp.exp(l1 - gm) + jnp.exp(l2 - gm) + jnp.exp(l3 - gm))
    el = jnp.where(gidx == 0, lt[8:16],
                   jnp.where(gidx == 1, lt[16:24], jnp.where(gidx == 2, lt[24:32], lt[32:40])))
    row8 = lax.broadcasted_iota(jnp.int32, (EXPERTS_PER_GROUP, ts), 0)
    m1 = jnp.max(el, axis=0, keepdims=True)
    i1 = jnp.min(jnp.where(el == m1, row8, EXPERTS_PER_GROUP), axis=0, keepdims=True)
    el2 = jnp.where(row8 == i1, -jnp.inf, el)
    m2 = jnp.max(el2, axis=0, keepdims=True)
    i2 = jnp.min(jnp.where(el2 == m2, row8, EXPERTS_PER_GROUP), axis=0, keepdims=True)
    dd = jnp.exp(m2 - m1)
    p1 = 1.0 / (1.0 + dd)
    p2 = dd / (1.0 + dd)
    e1 = gidx * EXPERTS_PER_GROUP + i1
    e2 = gidx * EXPERTS_PER_GROUP + i2
    eid_ref[0:1, :] = e1
    eid_ref[1:2, :] = e2
    gate_ref[...] = jnp.concatenate([gp * p1, gp * p2, jnp.zeros((6, ts), F32)], axis=0)

    row32 = lax.broadcasted_iota(jnp.int32, (N_EXPERTS, ts), 0)
    oh1 = row32 == e1
    oh2 = row32 == e2
    member = jnp.where(oh1, 1.0, 0.0) + jnp.where(oh2, 1.0, 0.0)
    before = (lax.broadcasted_iota(jnp.int32, (ts, ts), 0)
              < lax.broadcasted_iota(jnp.int32, (ts, ts), 1))
    prefix = _dot(member.astype(BF16), jnp.where(before, 1.0, 0.0).astype(BF16))
    base = cnt_s[...]
    prefix = prefix + jnp.concatenate([base] * (ts // LANE), axis=1)
    rank_ref[0:1, :] = jnp.sum(jnp.where(oh1, prefix, 0.0), axis=0, keepdims=True).astype(jnp.int32)
    rank_ref[1:2, :] = jnp.sum(jnp.where(oh2, prefix, 0.0), axis=0, keepdims=True).astype(jnp.int32)
    new_cnt = base + jnp.sum(member, axis=1, keepdims=True)
    cnt_s[...] = new_cnt
    cnt_ref[...] = new_cnt


def _const_spec(shape):
    nd = len(shape)
    return pl.BlockSpec(shape, lambda b, s, _nd=nd: (0,) * _nd, pipeline_mode=pl.Buffered(1))


def _mixer_call(x, wts, ts):
    B, S, D = x.shape
    T = B * S
    ns = S // ts
    tok_spec = pl.BlockSpec((None, ts, D), lambda b, s: (b, s, 0))
    flat_tok = lambda rows: pl.BlockSpec((rows, ts), lambda b, s: (0, b * ns + s))
    in_specs = [tok_spec] + [_const_spec(w.shape) for w in wts]
    out_shape = (
        jax.ShapeDtypeStruct((B, S, D), F32),
        jax.ShapeDtypeStruct((B, S * ROW_TILES, LANE), jnp.int32),
        jax.ShapeDtypeStruct((2, T), jnp.int32),
        jax.ShapeDtypeStruct((8, T), F32),
        jax.ShapeDtypeStruct((2, T), jnp.int32),
        jax.ShapeDtypeStruct((N_EXPERTS, LANE), F32),
    )
    packed_spec = pl.BlockSpec((None, ts * ROW_TILES, LANE), lambda b, s: (b, s, 0))
    out_specs = (tok_spec, packed_spec, flat_tok(2), flat_tok(8), flat_tok(2),
                 pl.BlockSpec((N_EXPERTS, LANE), lambda b, s: (0, 0)))
    grp = GLA_GROUP * GLA_CHUNK
    scratch = [
        pltpu.VMEM((GLA_HEADS, LANE, GLA_DV), F32),
        pltpu.VMEM((POOL_HALO, POOL_W), F32),
        pltpu.VMEM((N_EXPERTS, LANE), F32),
        pltpu.VMEM((GLA_HEADS, ts, LANE), BF16),
        pltpu.VMEM((GLA_HEADS, ts, LANE), BF16),
        pltpu.VMEM((GLA_HEADS // 2, ts // grp, LANE, grp), BF16),
        pltpu.VMEM((ts, VAL_W), BF16),
        pltpu.VMEM((ts // GLA_CHUNK, GLA_HEADS // 2, LANE, GLA_DV), F32),
        pltpu.VMEM((ts, VAL_W), F32),
        pltpu.VMEM((ts, D), F32),
        pltpu.VMEM((ts, 2 * D), F32),
    ]
    return pl.pallas_call(
        _mixer_kernel,
        grid=(B, ns),
        in_specs=in_specs,
        out_specs=out_specs,
        out_shape=out_shape,
        scratch_shapes=scratch,
        compiler_params=pltpu.CompilerParams(
            dimension_semantics=("arbitrary", "arbitrary"), vmem_limit_bytes=VMEM_LIMIT),
        name="mixer",
    )(x, *wts)


_PAD_BITS = tuple(p for p in (1 << i for i in reversed(range((EXP_BM - 1).bit_length()))) if p >= SUBLANE)


def _dispatch_kernel(pst_ref, cnt_ref, dest_ref, h2_ref, xs_ref, zero_s, sem, zsem):
    td = h2_ref.shape[0]
    i = pl.program_id(0)

    def zero_copy(off, p):
        return pltpu.make_async_copy(zero_s.at[pl.ds(0, p)], xs_ref.at[pl.ds(off, p)], zsem)

    def for_each_pad_piece(e, fn):
        cnt = cnt_ref[e]
        start = pst_ref[e] + cnt
        end = start + (EXP_BM - cnt % EXP_BM) % EXP_BM
        aligned = jnp.minimum((start + SUBLANE - 1) // SUBLANE * SUBLANE, end)
        for j in range(SUBLANE - 1):
            @pl.when(start + j < aligned)
            def _(j=j):
                fn(zero_copy(start + j, 1))

        npad = end - aligned
        off = aligned
        for p in _PAD_BITS:
            hit = (npad & p) != 0

            @pl.when(hit)
            def _(off=off, p=p):
                fn(zero_copy(pl.multiple_of(off, SUBLANE), p))

            off = off + jnp.where(hit, p, 0)

    @pl.when(i == 0)
    def _():
        zero_s[...] = jnp.zeros_like(zero_s)

        def start_e(e, c):
            for_each_pad_piece(e, lambda cp: cp.start())
            return c

        def wait_e(e, c):
            for_each_pad_piece(e, lambda cp: cp.wait())
            return c

        lax.fori_loop(0, N_EXPERTS, start_e, 0)
        lax.fori_loop(0, N_EXPERTS, wait_e, 0)

        zrows = zero_s.shape[0]
        last_cnt = cnt_ref[N_EXPERTS - 1]
        used = pst_ref[N_EXPERTS - 1] + (last_cnt + EXP_BM - 1) // EXP_BM * EXP_BM
        first_piece = used // zrows
        n_pieces = xs_ref.shape[0] // zrows

        def tail_copy(t):
            return zero_copy(pl.multiple_of(t * zrows, zrows), zrows)

        lax.fori_loop(first_piece, n_pieces, lambda t, c: (tail_copy(t).start(), c)[1], 0)
        lax.fori_loop(first_piece, n_pieces, lambda t, c: (tail_copy(t).wait(), c)[1], 0)

    def issue(g, c):
        for u in range(SUBLANE):
            for kk in range(2):
                d = dest_ref[kk * td + g * SUBLANE + u]
                pltpu.make_async_copy(h2_ref.at[g * SUBLANE + u], xs_ref.at[d], sem).start(priority=kk)
        return c

    lax.fori_loop(0, td // SUBLANE, issue, 0)
    for kk in range(2):
        pltpu.make_async_copy(xs_ref.at[pl.ds(0, td)], xs_ref.at[pl.ds(td, td)], sem).wait()


def _dispatch_call(h2, dest3, pstarts, counts, n_rows):
    T, RT, _ = h2.shape
    nt, _, td2 = dest3.shape
    td = td2 // 2
    grid_spec = pltpu.PrefetchScalarGridSpec(
        num_scalar_prefetch=2,
        grid=(nt,),
        in_specs=[
            pl.BlockSpec((None, None, 2 * td), lambda i, *_: (i, 0, 0), memory_space=pltpu.SMEM),
            pl.BlockSpec((td, RT, LANE), lambda i, *_: (i, 0, 0)),
        ],
        out_specs=pl.BlockSpec(memory_space=pl.ANY),
        scratch_shapes=[
            pltpu.VMEM((_PAD_BITS[0], RT, LANE), h2.dtype),
            pltpu.SemaphoreType.DMA(()),
            pltpu.SemaphoreType.DMA(()),
        ],
    )
    return pl.pallas_call(
        _dispatch_kernel,
        grid_spec=grid_spec,
        out_shape=jax.ShapeDtypeStruct((n_rows, RT, LANE), h2.dtype),
        compiler_params=pltpu.CompilerParams(
            dimension_semantics=("arbitrary",), vmem_limit_bytes=VMEM_LIMIT, has_side_effects=True),
        name="dispatch",
    )(pstarts, counts, dest3, h2)


def _expert_kernel(be_ref, nu_ref, x_ref, wg_ref, wu_ref, wd_ref, y_ref, wg_s, wu_s, wd_s):
    b = pl.program_id(0)
    used = b < nu_ref[0]

    @pl.when(used & ((b == 0) | (be_ref[b] != be_ref[jnp.maximum(b - 1, 0)])))
    def _():
        wg_s[...] = wg_ref[...].astype(BF16)
        wu_s[...] = wu_ref[...].astype(BF16)
        wd_s[...] = wd_ref[...].astype(BF16)

    @pl.when(used)
    def _():
        lo, hi = _unpack_rows(_load_rows(x_ref))
        xb = jnp.concatenate([lo.astype(BF16), hi.astype(BF16)], axis=1)
        g = _dot(xb, wg_s[...])
        u = _dot(xb, wu_s[...])
        hmid = (g * jax.nn.sigmoid(g) * u).astype(BF16)
        _store_rows(y_ref, _pack_rows(_dot(hmid, wd_s[...])))

    @pl.when(b >= nu_ref[0])
    def _():
        y_ref[...] = jnp.zeros_like(y_ref)


def _expert_call(xs, block_e, n_used, wg, wu, wd):
    R = xs.shape[0] // ROW_TILES
    D = D_MODEL
    nb = R // EXP_BM
    blk = (EXP_BM * ROW_TILES, LANE)
    last = lambda b, nu: jnp.minimum(b, nu[0] - 1)
    grid_spec = pltpu.PrefetchScalarGridSpec(
        num_scalar_prefetch=2,
        grid=(nb,),
        in_specs=[
            pl.BlockSpec(blk, lambda b, be, nu: (last(b, nu), 0)),
            pl.BlockSpec((None, D, D_EXPERT), lambda b, be, nu: (be[last(b, nu)], 0, 0)),
            pl.BlockSpec((None, D, D_EXPERT), lambda b, be, nu: (be[last(b, nu)], 0, 0)),
            pl.BlockSpec((None, D_EXPERT, D), lambda b, be, nu: (be[last(b, nu)], 0, 0)),
        ],
        out_specs=pl.BlockSpec(blk, lambda b, be, nu: (b, 0)),
        scratch_shapes=[pltpu.VMEM((D, D_EXPERT), BF16), pltpu.VMEM((D, D_EXPERT), BF16),
                        pltpu.VMEM((D_EXPERT, D), BF16)],
    )
    return pl.pallas_call(
        _expert_kernel,
        grid_spec=grid_spec,
        out_shape=jax.ShapeDtypeStruct(xs.shape, xs.dtype),
        compiler_params=pltpu.CompilerParams(
            dimension_semantics=("arbitrary",), vmem_limit_bytes=VMEM_LIMIT),
        name="experts",
    )(block_e, n_used, xs, wg, wu, wd)


def _combine_kernel(dcur_ref, dnext_ref, x1_ref, gate_ref, gf_ref, y_ref, out_ref, ybuf, sem):
    tf = x1_ref.shape[0]
    i = pl.program_id(0)
    n = pl.num_programs(0)
    tile_rows = SUBLANE * ROW_TILES

    def issue_all(dref, slot):
        def group(g, c):
            for u in range(SUBLANE):
                for kk in range(2):
                    d = dref[kk * tf + g * SUBLANE + u]
                    slab = pl.ds(pl.multiple_of(g * tile_rows, tile_rows) + u * ROW_TILES, ROW_TILES)
                    pltpu.make_async_copy(y_ref.at[d], ybuf.at[slot, kk, slab], sem.at[slot]).start(priority=kk)
            return c

        lax.fori_loop(0, tf // SUBLANE, group, 0)

    @pl.when(i == 0)
    def _():
        issue_all(dcur_ref, 0)

    for par in range(2):
        @pl.when((i % 2 == par) & (i + 1 < n))
        def _(par=par):
            issue_all(dnext_ref, 1 - par)

    slot = i % 2
    for kk in range(2):
        pltpu.make_async_copy(y_ref.at[pl.ds(0, tf)], y_ref.at[pl.ds(tf, tf)], sem.at[slot]).wait()

    g = jnp.concatenate([gate_ref[...], jnp.zeros((LANE - 8, tf), F32)], axis=0)
    gt = g.T
    w = ROW_TILES * LANE
    y0_lo, y0_hi = _unpack_rows(_load_rows(ybuf.at[slot, 0]))
    y1_lo, y1_hi = _unpack_rows(_load_rows(ybuf.at[slot, 1]))
    g0 = gt[:, 0:1]
    g1 = gt[:, 1:2]
    xo_lo = x1_ref[:, :w] + (g0 * y0_lo + g1 * y1_lo)
    xo_hi = x1_ref[:, w:] + (g0 * y0_hi + g1 * y1_hi)
    ms = (jnp.sum(xo_lo * xo_lo, axis=-1, keepdims=True)
          + jnp.sum(xo_hi * xo_hi, axis=-1, keepdims=True)) * (1.0 / (2 * w))
    scale = lax.rsqrt(ms + NORM_EPS)
    out_ref[:, :w] = xo_lo * scale * gf_ref[:, :w]
    out_ref[:, w:] = xo_hi * scale * gf_ref[:, w:]


def _combine_call(x1, dest3, gates, gf, ys):
    T, D = x1.shape
    nt, _, tf2 = dest3.shape
    tf = tf2 // 2
    return pl.pallas_call(
        _combine_kernel,
        grid=(nt,),
        in_specs=[
            pl.BlockSpec((None, None, 2 * tf), lambda i: (i, 0, 0), memory_space=pltpu.SMEM),
            pl.BlockSpec((None, None, 2 * tf), lambda i: (jnp.minimum(i + 1, nt - 1), 0, 0),
                         memory_space=pltpu.SMEM),
            pl.BlockSpec((tf, D), lambda i: (i, 0)),
            pl.BlockSpec((8, tf), lambda i: (0, i)),
            pl.BlockSpec((1, D), lambda i: (0, 0)),
            pl.BlockSpec(memory_space=pl.ANY),
        ],
        out_specs=pl.BlockSpec((tf, D), lambda i: (i, 0)),
        out_shape=jax.ShapeDtypeStruct((T, D), F32),
        scratch_shapes=[pltpu.VMEM((2, 2, tf * ROW_TILES, LANE), ys.dtype),
                        pltpu.SemaphoreType.DMA((2,))],
        compiler_params=pltpu.CompilerParams(
            dimension_semantics=("arbitrary",), vmem_limit_bytes=VMEM_LIMIT),
        name="combine",
    )(dest3, dest3, x1, gates, gf, ys)


def _pick_tile(n, want):
    t = min(want, n)
    while n % t:
        t //= 2
    return t


def _mixer_weights(norm1_g, w_in, w_alpha_up, b_alpha, gla_norm_g, w_gla_branch, pool_w, pool_scale,
                   w_pool_branch, w_out, norm2_g, w_rg, b_rg, w_re, b_re):
    c0 = 2 * KEY_W + 2 * VAL_W
    c1 = c0 + GATE_RANK
    c2 = c1 + POOL_W
    w_cat = jnp.concatenate([w_in[:, :c0], jnp.pad(w_in[:, c0:c1], ((0, 0), (0, LANE - GATE_RANK))),
                             w_in[:, c1:]], axis=1).astype(BF16)
    w_alpha = jnp.pad(w_alpha_up, ((0, LANE - GATE_RANK), (0, 0))).astype(BF16)
    w_re_t = jnp.transpose(w_re, (0, 2, 1)).reshape(N_EXPERTS, D_MODEL)
    wrt = jnp.zeros((ROUTER_ROWS, D_MODEL), F32)
    wrt = wrt.at[0:N_GROUPS].set(w_rg.T).at[8:8 + N_EXPERTS].set(w_re_t).astype(BF16)
    brt = jnp.zeros((ROUTER_ROWS,), F32).at[0:N_GROUPS].set(b_rg).at[8:8 + N_EXPERTS].set(b_re.reshape(-1))
    brt = jnp.broadcast_to(brt[:, None], (ROUTER_ROWS, LANE))
    row = lambda a: a.reshape(1, -1).astype(F32)
    return (row(norm1_g), w_cat, w_alpha, row(b_alpha), row(gla_norm_g),
            w_gla_branch.astype(BF16), pool_w.astype(BF16), row(pool_scale), w_pool_branch.astype(BF16),
            w_out.astype(BF16), row(norm2_g), wrt, brt)


def kernel(x, norm1_g, w_in, w_alpha_up, b_alpha, gla_norm_g, w_gla_branch, pool_w, pool_scale,
           w_pool_branch, w_out, norm2_g, w_router_group, b_router_group, w_router_expert,
           b_router_expert, w_exp_gate, w_exp_up, w_exp_down, norm_f_g):
    B, S, D = x.shape
    T = B * S
    depth = w_in.shape[0]
    ts = _pick_tile(S, MIX_TS)
    td = _pick_tile(T, DISPATCH_TILE)
    tc = _pick_tile(T, COMBINE_TILE)
    n_assign = 2 * T
    n_blocks = -(-(n_assign + N_EXPERTS * (EXP_BM - 1)) // EXP_BM)
    n_rows = n_blocks * EXP_BM

    assert depth == 1, "kernel supports the problem's DEPTH=1"
    for l in range(depth):
        wts = _mixer_weights(norm1_g[l], w_in[l], w_alpha_up[l], b_alpha[l], gla_norm_g[l], w_gla_branch[l],
                             pool_w[l], pool_scale[l], w_pool_branch[l], w_out[l], norm2_g[l],
                             w_router_group[l], b_router_group[l], w_router_expert[l], b_router_expert[l])
        x1, h2, eid, gates, rank, cnt = _mixer_call(x, wts, ts)

        counts = cnt[:, 0].astype(jnp.int32)
        padded = ((counts + EXP_BM - 1) // EXP_BM) * EXP_BM
        pends = jnp.cumsum(padded)
        pstarts = pends - padded
        expert_col = jnp.arange(N_EXPERTS, dtype=jnp.int32)[:, None, None]
        dest = jnp.sum(jnp.where(eid[None] == expert_col, pstarts[:, None, None], 0), axis=0) + rank
        tile_dest = lambda t: dest.reshape(2, T // t, t).transpose(1, 0, 2).reshape(T // t, 1, 2 * t)
        dest_d = tile_dest(td)
        dest_c = tile_dest(tc)
        blk_start = jnp.arange(n_blocks, dtype=jnp.int32) * EXP_BM
        block_e = jnp.minimum(jnp.sum(pends[None, :] <= blk_start[:, None], axis=1), N_EXPERTS - 1).astype(jnp.int32)
        n_used = (pends[-1:] // EXP_BM).astype(jnp.int32)

        xs = _dispatch_call(h2.reshape(T, ROW_TILES, LANE), dest_d, pstarts.astype(jnp.int32), counts, n_rows)
        ys = _expert_call(xs.reshape(n_rows * ROW_TILES, LANE), block_e, n_used,
                          w_exp_gate[l], w_exp_up[l], w_exp_down[l])
        out = _combine_call(x1.reshape(T, D), dest_c, gates, norm_f_g.reshape(1, D).astype(F32),
                            ys.reshape(n_rows, ROW_TILES, LANE))
        x = out.reshape(B, S, D)
    return x
```
